```python
import jax, jax.numpy as jnp
from jax import lax
import numpy as np

D_MODEL = 1024
BATCH = 32
SEQ = 256
DEPTH = 4
DEC_BATCH = 4
DEC_SEQ = 1024
PAST_LEN = 512

GRID_W = 64
N_MIXERS = 4
N_HGRN = (DEPTH + 3) // 4
N_SCONV = (DEPTH + 2) // 4
N_RGLRU = (DEPTH + 1) // 4
N_MLA = DEPTH // 4
DEEPNORM_ALPHA = (2 * DEPTH) ** 0.25
DEEPNORM_BETA = (8 * DEPTH) ** -0.25

HG_HEADS = 8
HG_KDIM = 128
HG_VDIM = D_MODEL // HG_HEADS
HG_F = HG_HEADS * HG_KDIM
HG_V = HG_HEADS * HG_VDIM
HG_CHUNK = 32
SC_WIDTH = D_MODEL
SC_KERNEL = 3
RG_WIDTH = D_MODEL
RG_HEADS = 4
RG_BLOCK = RG_WIDTH // RG_HEADS
RG_KERNEL = 4
RG_C = 8.0
MLA_HEADS = 16
MLA_Q_RANK = 384
MLA_KV_RANK = 256
MLA_NOPE = 64
MLA_ROPE = 32
MLA_VDIM = 64
ROPE_BASE = 10000.0
Q_BLOCK = 128

kernel_name = 'hybrid_flow_backbone_step'

f32 = jnp.float32


def layer_norm(x, g, b, eps=1e-5):
    xf = x.astype(f32)
    mu = jnp.mean(xf, -1, keepdims=True)
    var = jnp.mean(jnp.square(xf - mu), -1, keepdims=True)
    return ((xf - mu) * lax.rsqrt(var + eps) * g.astype(f32) + b.astype(f32)).astype(x.dtype)


def rms_norm(x, g, eps=1e-6):
    xf = x.astype(f32)
    y = xf * lax.rsqrt(jnp.mean(xf * xf, -1, keepdims=True) + eps)
    return (y * g.astype(f32)).astype(x.dtype)


def modulation(cvec, w, b):
    return jnp.split(jax.nn.silu(cvec) @ w + b, 3, axis=-1)


def depthwise_conv(x, w, b, pad):
    y = lax.conv_general_dilated(x, w[:, None, :].astype(x.dtype), (1,), [pad],
                                 dimension_numbers=('NWC', 'WIO', 'NWC'),
                                 feature_group_count=x.shape[-1])
    return y + b


def axial_rope(x):
    n = x.shape[1]
    rows = n // GRID_W
    row = jnp.repeat(jnp.arange(rows), GRID_W).astype(f32)
    col = jnp.tile(jnp.arange(GRID_W), rows).astype(f32)
    half = MLA_ROPE // 2
    inv_freq = ROPE_BASE ** (-jnp.arange(0, half, 2, dtype=f32) / half)
    bshape = (n,) + (1,) * (x.ndim - 3) + (half // 2,)
    parts = []
    for pos, xa in ((row, x[..., :half]), (col, x[..., half:])):
        ang = (pos[:, None] * inv_freq[None]).reshape(bshape)
        cos, sin = jnp.cos(ang).astype(x.dtype), jnp.sin(ang).astype(x.dtype)
        x1, x2 = xa[..., :half // 2], xa[..., half // 2:]
        parts += [x1 * cos - x2 * sin, x1 * sin + x2 * cos]
    return jnp.concatenate(parts, -1)


def blocked_attention(q, k, v, scale):
    b_, nq, hh, dq = q.shape
    qb = q.reshape(b_, nq // Q_BLOCK, Q_BLOCK, hh, dq).swapaxes(0, 1)

    def one_block(qi):
        s = jnp.einsum('bqhd,bkhd->bhqk', qi, k).astype(f32) * scale
        p = jax.nn.softmax(s, axis=-1).astype(v.dtype)
        return jnp.einsum('bhqk,bkhd->bqhd', p, v)

    o = lax.map(one_block, qb)
    return o.swapaxes(0, 1).reshape(b_, nq, hh, v.shape[-1])


def gla_chunk(q, k, v, logf, s0):
    b_, n, hh, dk = q.shape
    dv = v.shape[-1]
    nc = n // HG_CHUNK
    q, k, v, logf = [t.reshape(b_, nc, HG_CHUNK, hh, t.shape[-1]) for t in (q, k, v, logf)]
    cum = jnp.cumsum(logf, axis=2)
    ref = cum[:, :, HG_CHUNK // 2 - 1:HG_CHUNK // 2]
    q_rel = q * jnp.exp(cum - ref)
    k_rel = k * jnp.exp(ref - cum)
    causal = jnp.tril(jnp.ones((HG_CHUNK, HG_CHUNK), dtype=bool))
    att = jnp.where(causal, jnp.einsum('bnthk,bnshk->bnhts', q_rel, k_rel), 0)
    o = jnp.einsum('bnhts,bnshv->bnthv', att.astype(v.dtype), v)
    total = cum[:, :, -1]
    upd = jnp.einsum('bnshk,bnshv->bnhkv', k * jnp.exp(total[:, :, None] - cum), v)

    def step(s, inp):
        dec, u = inp
        return dec[..., None] * s + u, s

    s_fin, s_prev = lax.scan(step, s0, (jnp.exp(total).swapaxes(0, 1), upd.swapaxes(0, 1)))
    o = o + jnp.einsum('bnthk,nbhkv->bnthv', q * jnp.exp(cum), s_prev)
    return o.reshape(b_, n, hh, dv), s_fin


def hgrn2_mixer(h, w_in, lb, norm_g, w_out, s0):
    b_, n, _ = h.shape
    q, f_fw, f_bw, i, g = jnp.split(h @ w_in, [HG_F, 2 * HG_F, 3 * HG_F, 3 * HG_F + HG_V], axis=-1)
    q = (jax.nn.silu(q) * HG_KDIM ** -0.5).reshape(b_, n, HG_HEADS, HG_KDIM)
    v = i.reshape(b_, n, HG_HEADS, HG_VDIM)
    outs, finals = [], []
    for d, fl in enumerate((f_fw, f_bw)):
        f = (lb[d] + (1 - lb[d]) * jax.nn.sigmoid(fl)).reshape(b_, n, HG_HEADS, HG_KDIM)
        args = (q, 1 - f, v, jnp.log(f))
        if d == 1:
            args = tuple(jnp.flip(t, 1) for t in args)
        o, s = gla_chunk(*args, s0[:, d])
        outs.append(jnp.flip(o, 1) if d == 1 else o)
        finals.append(s)
    o = rms_norm(outs[0] + outs[1], norm_g.reshape(HG_HEADS, HG_VDIM))
    y = (o.reshape(b_, n, HG_V) * jax.nn.silu(g)) @ w_out
    return y, jnp.stack(finals, axis=1)


def short_conv_mixer(h, w_in, conv_w, conv_b, w_out):
    bg, cg, v, g = jnp.split(h @ w_in, 4, axis=-1)
    z = depthwise_conv(cg * v, conv_w, conv_b, (SC_KERNEL // 2, SC_KERNEL // 2))
    return (jax.nn.silu(g) * bg * z) @ w_out


def rglru_mixer(h, w_in, conv_w, conv_b, w_gate, b_gate, lam, w_out, h0):
    b_, n, _ = h.shape
    u, g = jnp.split(h @ w_in, 2, axis=-1)
    u = depthwise_conv(u, conv_w, conv_b, (RG_KERNEL // 2, RG_KERNEL - 1 - RG_KERNEL // 2))
    ub = u.reshape(b_, n, RG_HEADS, RG_BLOCK)
    gates = jax.nn.sigmoid(jnp.einsum('bnhi,dhio->bndho', ub, w_gate) + b_gate)
    r = gates[..., :RG_BLOCK].reshape(b_, n, 2, RG_WIDTH)
    i = gates[..., RG_BLOCK:].reshape(b_, n, 2, RG_WIDTH)
    a = jnp.exp(-RG_C * jax.nn.softplus(-lam) * r)
    xin = jnp.sqrt(1 - a * a) * (i * u[:, :, None])

    def comb(lhs, rhs):
        return lhs[0] * rhs[0], rhs[0] * lhs[1] + rhs[1]

    hs = []
    for d in range(2):
        a_cum, b_cum = lax.associative_scan(comb, (a[:, :, d], xin[:, :, d]), axis=1, reverse=(d == 1))
        hs.append(a_cum * h0[:, d, None] + b_cum)
    y = ((hs[0] + hs[1]) * jax.nn.silu(g)) @ w_out
    return y, jnp.stack([hs[0][:, -1], hs[1][:, 0]], axis=1)


def mla_kv(ckv, kpe, w_kvb):
    b_, n, _ = ckv.shape
    kv = (ckv @ w_kvb).reshape(b_, n, MLA_HEADS, MLA_NOPE + MLA_VDIM)
    k = jnp.concatenate([kv[..., :MLA_NOPE],
                         jnp.broadcast_to(kpe[:, :, None], (b_, n, MLA_HEADS, MLA_ROPE))], -1)
    return k, kv[..., MLA_NOPE:]


def mla_mixer(h, w_in, q_norm, kv_norm, w_qb, w_kvb, w_out, ctx_ckv=None, ctx_kpe=None):
    b_, n, _ = h.shape
    cq, ckv, kpe, g = jnp.split(h @ w_in, [MLA_Q_RANK, MLA_Q_RANK + MLA_KV_RANK,
                                           MLA_Q_RANK + MLA_KV_RANK + MLA_ROPE], axis=-1)
    q = (rms_norm(cq, q_norm) @ w_qb).reshape(b_, n, MLA_HEADS, MLA_NOPE + MLA_ROPE)
    ckv = rms_norm(ckv, kv_norm)
    if ctx_ckv is None:
        k, v = mla_kv(ckv, kpe, w_kvb)
    else:
        q = jnp.concatenate([q[..., :MLA_NOPE], axial_rope(q[..., MLA_NOPE:])], -1)
        k_lat, v_lat = mla_kv(ckv, axial_rope(kpe), w_kvb)
        k_ctx, v_ctx = mla_kv(ctx_ckv, ctx_kpe, w_kvb)
        k = jnp.concatenate([k_ctx, k_lat], axis=1)
        v = jnp.concatenate([v_ctx, v_lat], axis=1)
    o = blocked_attention(q, k, v, (MLA_NOPE + MLA_ROPE) ** -0.5)
    y = (o.reshape(b_, n, MLA_HEADS * MLA_VDIM) * jax.nn.silu(g)) @ w_out
    return y, ckv, kpe


def setup_inputs(seed: int = 0) -> dict:
    key = jax.random.key(seed)
    ks = iter(jax.random.split(key, 48))
    D = D_MODEL

    def nrm(shape, s):
        return jax.random.normal(next(ks), shape, f32) * s

    u = jax.random.uniform(next(ks), (N_RGLRU, 2, RG_WIDTH), f32, 0.9, 0.999)
    base = u ** (1.0 / RG_C)
    rg_lambda = jnp.log(base) - jnp.log1p(-base)
    beta = DEEPNORM_BETA
    return {
        'x_prompt': nrm((BATCH, SEQ, D), 1.0),
        'x_sample': nrm((DEC_BATCH, DEC_SEQ, D), 1.0),
        'state_hgrn': nrm((DEC_BATCH, N_HGRN, 2, HG_HEADS, HG_KDIM, HG_VDIM), 0.5),
        'state_rglru': nrm((DEC_BATCH, N_RGLRU, 2, RG_WIDTH), 0.5),
        'cache_mla_ckv': nrm((DEC_BATCH, N_MLA, PAST_LEN, MLA_KV_RANK), 1.0),
        'cache_mla_kpe': nrm((DEC_BATCH, N_MLA, PAST_LEN, MLA_ROPE), 1.0),
        'c': nrm((DEC_BATCH, D), 1.0),
        'c_ctx': nrm((D,), 1.0),
        'ada_w': nrm((DEPTH, D, 3 * D), D ** -0.5),
        'ada_b': nrm((DEPTH, 3 * D), 0.02),
        'ln_g': 1.0 + nrm((DEPTH, D), 0.02),
        'ln_b': nrm((DEPTH, D), 0.02),
        'hg_w_in': nrm((N_HGRN, D, 3 * HG_F + 2 * HG_V), D ** -0.5),
        'hg_lb_logits': nrm((2, DEPTH + 1, HG_F), 0.1),
        'hg_norm_g': 1.0 + nrm((N_HGRN, HG_V), 0.02),
        'hg_w_out': nrm((N_HGRN, HG_V, D), beta * HG_V ** -0.5),
        'sc_w_in': nrm((N_SCONV, D, 4 * SC_WIDTH), D ** -0.5),
        'sc_conv_w': nrm((N_SCONV, SC_KERNEL, SC_WIDTH), SC_KERNEL ** -0.5),
        'sc_conv_b': nrm((N_SCONV, SC_WIDTH), 0.02),
        'sc_w_out': nrm((N_SCONV, SC_WIDTH, D), beta * SC_WIDTH ** -0.5),
        'rg_w_in': nrm((N_RGLRU, D, 2 * RG_WIDTH), D ** -0.5),
        'rg_conv_w': nrm((N_RGLRU, RG_KERNEL, RG_WIDTH), RG_KERNEL ** -0.5),
        'rg_conv_b': nrm((N_RGLRU, RG_WIDTH), 0.02),
        'rg_w_gate': nrm((N_RGLRU, 2, RG_HEADS, RG_BLOCK, 2 * RG_BLOCK), RG_BLOCK ** -0.5),
        'rg_b_gate': nrm((N_RGLRU, 2, RG_HEADS, 2 * RG_BLOCK), 0.02),
        'rg_lambda': rg_lambda,
        'rg_w_out': nrm((N_RGLRU, RG_WIDTH, D), beta * RG_WIDTH ** -0.5),
        'mla_w_in': nrm((N_MLA, D, MLA_Q_RANK + MLA_KV_RANK + MLA_ROPE + MLA_HEADS * MLA_VDIM), D ** -0.5),
        'mla_q_norm': 1.0 + nrm((N_MLA, MLA_Q_RANK), 0.02),
        'mla_kv_norm': 1.0 + nrm((N_MLA, MLA_KV_RANK), 0.02),
        'mla_w_qb': nrm((N_MLA, MLA_Q_RANK, MLA_HEADS * (MLA_NOPE + MLA_ROPE)), MLA_Q_RANK ** -0.5),
        'mla_w_kvb': nrm((N_MLA, MLA_KV_RANK, MLA_HEADS * (MLA_NOPE + MLA_VDIM)), MLA_KV_RANK ** -0.5),
        'mla_w_out': nrm((N_MLA, MLA_HEADS * MLA_VDIM, D), beta * (MLA_HEADS * MLA_VDIM) ** -0.5),
    }


def reference(x_prompt, x_sample, state_hgrn, state_rglru, cache_mla_ckv, cache_mla_kpe, c, c_ctx,
              ada_w, ada_b, ln_g, ln_b, hg_w_in, hg_lb_logits, hg_norm_g, hg_w_out,
              sc_w_in, sc_conv_w, sc_conv_b, sc_w_out,
              rg_w_in, rg_conv_w, rg_conv_b, rg_w_gate, rg_b_gate, rg_lambda, rg_w_out,
              mla_w_in, mla_q_norm, mla_kv_norm, mla_w_qb, mla_w_kvb, mla_w_out):
    hg_lb = jnp.cumsum(jax.nn.softmax(hg_lb_logits.astype(f32), axis=1), axis=1)

    x = x_prompt
    bsz = x.shape[0]
    hg_new, rg_new, ckv_new, kpe_new = [], [], [], []
    for l in range(DEPTH):
        m, j = l % N_MIXERS, l // N_MIXERS
        shift, scale, gate = modulation(c_ctx, ada_w[l], ada_b[l])
        h = x * (1 + scale) + shift
        if m == 0:
            s0 = jnp.zeros((bsz, 2, HG_HEADS, HG_KDIM, HG_VDIM), x.dtype)
            out, s = hgrn2_mixer(h, hg_w_in[j], hg_lb[:, l].astype(x.dtype), hg_norm_g[j], hg_w_out[j], s0)
            hg_new.append(s)
        elif m == 1:
            out = short_conv_mixer(h, sc_w_in[j], sc_conv_w[j], sc_conv_b[j], sc_w_out[j])
        elif m == 2:
            h0 = jnp.zeros((bsz, 2, RG_WIDTH), x.dtype)
            out, s = rglru_mixer(h, rg_w_in[j], rg_conv_w[j], rg_conv_b[j], rg_w_gate[j], rg_b_gate[j],
                                 rg_lambda[j], rg_w_out[j], h0)
            rg_new.append(s)
        else:
            out, ckv, kpe = mla_mixer(h, mla_w_in[j], mla_q_norm[j], mla_kv_norm[j], mla_w_qb[j],
                                      mla_w_kvb[j], mla_w_out[j])
            ckv_new.append(ckv)
            kpe_new.append(kpe)
        x = layer_norm(DEEPNORM_ALPHA * x + gate * out, ln_g[l], ln_b[l])
    y_prompt = x

    x = x_sample
    for l in range(DEPTH):
        m, j = l % N_MIXERS, l // N_MIXERS
        shift, scale, gate = [t[:, None] for t in modulation(c, ada_w[l], ada_b[l])]
        h = x * (1 + scale) + shift
        if m == 0:
            out, _ = hgrn2_mixer(h, hg_w_in[j], hg_lb[:, l].astype(x.dtype), hg_norm_g[j], hg_w_out[j],
                                 state_hgrn[:, j])
        elif m == 1:
            out = short_conv_mixer(h, sc_w_in[j], sc_conv_w[j], sc_conv_b[j], sc_w_out[j])
        elif m == 2:
            out, _ = rglru_mixer(h, rg_w_in[j], rg_conv_w[j], rg_conv_b[j], rg_w_gate[j], rg_b_gate[j],
                                 rg_lambda[j], rg_w_out[j], state_rglru[:, j])
        else:
            out, _, _ = mla_mixer(h, mla_w_in[j], mla_q_norm[j], mla_kv_norm[j], mla_w_qb[j],
                                  mla_w_kvb[j], mla_w_out[j], cache_mla_ckv[:, j], cache_mla_kpe[:, j])
        x = layer_norm(DEEPNORM_ALPHA * x + gate * out, ln_g[l], ln_b[l])
    y_sample = x

    new_state_hgrn = jnp.stack(hg_new, axis=1)
    new_state_rglru = jnp.stack(rg_new, axis=1)
    new_cache_mla_ckv = jnp.stack(ckv_new, axis=1)
    new_cache_mla_kpe = jnp.stack(kpe_new, axis=1)
    return (y_prompt, y_sample, new_state_hgrn, new_state_rglru, new_cache_mla_ckv, new_cache_mla_kpe)
```

```python
import functools

import jax
import jax.numpy as jnp
import numpy as np
from jax import lax
from jax.experimental import pallas as pl
from jax.experimental.pallas import tpu as pltpu

F32 = jnp.float32
BF16 = jnp.bfloat16

LANES = 128
SUBLANES = 8
VMEM_LIMIT_BYTES = 60 * 1024 * 1024

DEPTH = 4
DEEPNORM_ALPHA = (2 * DEPTH) ** 0.25
LN_EPS = 1e-5
RMS_EPS = 1e-6

ROW_TILE = 256

HG_HEADS = 8
HG_KDIM = 128
HG_VDIM = 128
HG_CHUNK = 32
SC_KERNEL = 3
RG_HEADS = 4
RG_KERNEL = 4
RG_C = 8.0
MLA_HEADS = 16
MLA_Q_RANK = 384
MLA_KV_RANK = 256
MLA_NOPE = 64
MLA_ROPE = 32
MLA_VDIM = 64
MLA_HEAD_PAD = 128
ROPE_BASE = 10000.0
GRID_W = 64


def _mm(a, b):
    return jnp.dot(a.astype(BF16), b.astype(BF16), preferred_element_type=F32)


def _mm_nt(a, b):
    return lax.dot_general(a.astype(BF16), b.astype(BF16), (((1,), (1,)), ((), ())),
                           preferred_element_type=F32)


def _mm_tn(a, b):
    return lax.dot_general(a.astype(BF16), b.astype(BF16), (((0,), (0,)), ((), ())),
                           preferred_element_type=F32)


def _silu(x):
    return x * jax.nn.sigmoid(x)


def _mod_rows(mod_ref, row):
    if isinstance(row, int):
        return tuple(mod_ref[0, j, row:row + 1, :] for j in range(3))
    rows = lax.broadcasted_iota(jnp.int32, (SUBLANES, 1), 0)
    return tuple(jnp.sum(jnp.where(rows == row, mod_ref[0, j], 0.0), axis=0, keepdims=True)
                 for j in range(3))


def _residual_ln(x, y, gate, g, b):
    z = DEEPNORM_ALPHA * x + gate * y
    mu = jnp.mean(z, axis=-1, keepdims=True)
    zc = z - mu
    var = jnp.mean(zc * zc, axis=-1, keepdims=True)
    return zc * lax.rsqrt(var + LN_EPS) * g + b


def _const_spec(shape):
    nd = len(shape)
    return pl.BlockSpec(shape, lambda *_: (0,) * nd, pipeline_mode=pl.Buffered(1))


def _seq_spec(n, d):
    mode = dict(pipeline_mode=pl.Buffered(1)) if n > ROW_TILE else {}
    return pl.BlockSpec((1, n, d), lambda b: (b, 0, 0), **mode)


def _compiler_params(n_grid):
    return pltpu.CompilerParams(dimension_semantics=("arbitrary",) * n_grid,
                                vmem_limit_bytes=VMEM_LIMIT_BYTES)


def _mod_kernel(cond_ref, w_ref, b_ref, out_ref):
    s = _silu(cond_ref[...])
    out_ref[0, 0] = _mm(s, w_ref[0]) + b_ref[0, 0]


def _modulation(cond, ada_w, ada_b):
    n_layers, d, _ = ada_w.shape
    rows = cond.shape[0]
    return pl.pallas_call(
        _mod_kernel,
        grid=(n_layers, 3),
        in_specs=[
            pl.BlockSpec((rows, d), lambda l, j: (0, 0)),
            pl.BlockSpec((1, d, d), lambda l, j: (l, 0, j)),
            pl.BlockSpec((1, 1, 1, d), lambda l, j: (l, j, 0, 0)),
        ],
        out_specs=pl.BlockSpec((1, 1, rows, d), lambda l, j: (l, j, 0, 0)),
        out_shape=jax.ShapeDtypeStruct((n_layers, 3, rows, d), F32),
        compiler_params=_compiler_params(2),
        name="adaln_modulation",
    )(cond, ada_w, ada_b.reshape(n_layers, 3, 1, d))


def _sconv_kernel(x_ref, mod_ref, lng_ref, lnb_ref, w_in_ref, cw_ref, cb_ref, w_out_ref,
                  out_ref, u_scr, *, row_of_batch):
    n, d = x_ref.shape[1], x_ref.shape[2]
    width = w_out_ref.shape[0]
    row = pl.program_id(0) + 1 if row_of_batch else 0
    shift, scale, gate = _mod_rows(mod_ref, row)
    x = x_ref[0]
    h = (x * (1.0 + scale) + shift).astype(BF16)
    t = lax.broadcasted_iota(jnp.int32, (n, 1), 0)
    chunk = 2 * LANES
    for j in range(width // chunk):
        c0 = j * chunk
        bg = _mm(h, w_in_ref[:, c0:c0 + chunk])
        cg = _mm(h, w_in_ref[:, width + c0:width + c0 + chunk])
        v = _mm(h, w_in_ref[:, 2 * width + c0:2 * width + c0 + chunk])
        g = _mm(h, w_in_ref[:, 3 * width + c0:3 * width + c0 + chunk])
        p = cg * v
        p_prev = jnp.where(t >= 1, pltpu.roll(p, 1, 0), 0.0)
        p_next = jnp.where(t < n - 1, pltpu.roll(p, n - 1, 0), 0.0)
        z = (cw_ref[0:1, c0:c0 + chunk] * p_prev + cw_ref[1:2, c0:c0 + chunk] * p
             + cw_ref[2:3, c0:c0 + chunk] * p_next + cb_ref[:, c0:c0 + chunk])
        u_scr[:, c0:c0 + chunk] = (_silu(g) * bg * z).astype(BF16)
    y = _mm(u_scr[...], w_out_ref[...])
    out_ref[0] = _residual_ln(x, y, gate, lng_ref[0], lnb_ref[0])


def _sconv_layer(x, mod, layer, ln_g, ln_b, w_in, conv_w, conv_b, w_out, *, row_of_batch):
    bsz, n, d = x.shape
    width = w_out.shape[0]
    kernel = functools.partial(_sconv_kernel, row_of_batch=row_of_batch)
    return pl.pallas_call(
        kernel,
        grid=(bsz,),
        in_specs=[
            _seq_spec(n, d),
            pl.BlockSpec((1, 3, SUBLANES, d), lambda b: (layer, 0, 0, 0)),
            pl.BlockSpec((1, 1, d), lambda b: (layer, 0, 0)),
            pl.BlockSpec((1, 1, d), lambda b: (layer, 0, 0)),
            _const_spec(w_in.shape),
            _const_spec(conv_w.shape),
            _const_spec(conv_b.shape),
            _const_spec(w_out.shape),
        ],
        out_specs=_seq_spec(n, d),
        out_shape=jax.ShapeDtypeStruct((bsz, n, d), F32),
        scratch_shapes=[pltpu.VMEM((n, width), BF16)],
        compiler_params=_compiler_params(1),
        name="sconv_layer",
    )(x, mod, ln_g, ln_b, w_in, conv_w, conv_b, w_out)


def _chunk_cumsum(x, reverse):
    n = x.shape[0]
    pos = lax.broadcasted_iota(jnp.int32, (n, 1), 0) % HG_CHUNK
    s = 1
    while s < HG_CHUNK:
        if reverse:
            x = x + jnp.where(pos < HG_CHUNK - s, pltpu.roll(x, n - s, 0), 0.0)
        else:
            x = x + jnp.where(pos >= s, pltpu.roll(x, s, 0), 0.0)
        s *= 2
    return x


def _hgrn_kernel(*refs, row_of_batch, layer, has_s0, emit_state):
    it = iter(refs)
    x_ref, mod_ref, lng_ref, lnb_ref, w_in_ref, lbl_ref, ng_ref, w_out_ref = (next(it) for _ in range(8))
    s0_ref = next(it) if has_s0 else None
    out_ref = next(it)
    s_out_ref = next(it) if emit_state else None
    h_scr, qrel_scr, krel_scr, qp_scr, kp_scr, v_scr, dec_scr, o_scr, u_scr = it

    n, d = x_ref.shape[1], x_ref.shape[2]
    hk = HG_HEADS * HG_KDIM
    n_tiles = n // ROW_TILE
    chunks_per_tile = ROW_TILE // HG_CHUNK
    n_chunks = n // HG_CHUNK
    pair = 2 * LANES

    row = pl.program_id(0) + 1 if row_of_batch else 0
    shift, scale, gate = _mod_rows(mod_ref, row)
    x = x_ref[0]
    h_scr[...] = (x * (1.0 + scale) + shift).astype(BF16)

    lbs = []
    for dr in range(2):
        z = lbl_ref[dr]
        e = jnp.exp(z - jnp.max(z, axis=0, keepdims=True))
        lbs.append(jnp.sum(e[:layer + 1], axis=0, keepdims=True) / jnp.sum(e, axis=0, keepdims=True))

    ri = lax.broadcasted_iota(jnp.int32, (ROW_TILE, ROW_TILE), 0)
    ci = lax.broadcasted_iota(jnp.int32, (ROW_TILE, ROW_TILE), 1)
    same_chunk = (ri // HG_CHUNK) == (ci // HG_CHUNK)
    masks = (same_chunk & (ci <= ri), same_chunk & (ci >= ri))

    for hp in range(hk // pair):
        c0 = hp * pair
        h = h_scr[...]
        q = _silu(_mm(h, w_in_ref[:, c0:c0 + pair])) * (HG_KDIM ** -0.5)
        v_scr[...] = _mm(h, w_in_ref[:, 3 * hk + c0:3 * hk + c0 + pair]).astype(BF16)
        q3 = q.reshape(n_chunks, HG_CHUNK, pair)
        for dr in range(2):
            fl = _mm(h, w_in_ref[:, (1 + dr) * hk + c0:(1 + dr) * hk + c0 + pair])
            lb = lbs[dr][:, c0:c0 + pair]
            f = lb + (1.0 - lb) * jax.nn.sigmoid(fl)
            k3 = (1.0 - f).reshape(n_chunks, HG_CHUNK, pair)
            cum = _chunk_cumsum(jnp.log(f), reverse=(dr == 1)).reshape(n_chunks, HG_CHUNK, pair)
            mid = HG_CHUNK // 2 - 1
            if dr == 0:
                ref, tot = cum[:, mid:mid + 1], cum[:, HG_CHUNK - 1:HG_CHUNK]
            else:
                ref, tot = cum[:, HG_CHUNK - 1 - mid:HG_CHUNK - mid], cum[:, 0:1]
            q_rel = q3 * jnp.exp(cum - ref)
            k_rel = k3 * jnp.exp(ref - cum)
            qrel_scr[...] = q_rel.reshape(n, pair).astype(BF16)
            krel_scr[...] = k_rel.reshape(n, pair).astype(BF16)
            qp_scr[...] = (q_rel * jnp.exp(ref)).reshape(n, pair).astype(BF16)
            kp_scr[...] = (k_rel * jnp.exp(tot - ref)).reshape(n, pair).astype(BF16)
            dec_scr[...] = jnp.exp(tot)

            for h2 in range(2):
                l0 = h2 * LANES
                head = hp * 2 + h2
                if has_s0:
                    s_t0 = s0_ref[0, 0, dr, head].T
                else:
                    s_t0 = jnp.zeros((HG_VDIM, HG_KDIM), F32)

                def tile_body(i, s_t, dr=dr, l0=l0):
                    ti = i if dr == 0 else n_tiles - 1 - i
                    r0 = pl.multiple_of(ti * ROW_TILE, ROW_TILE)
                    rows = pl.ds(r0, ROW_TILE)
                    lanes = slice(l0, l0 + LANES)
                    att = _mm_nt(qrel_scr[rows, lanes], krel_scr[rows, lanes])
                    att = jnp.where(masks[dr], att, 0.0)
                    o_intra = _mm(att, v_scr[rows, lanes])
                    o_inter = [None] * chunks_per_tile
                    order = range(chunks_per_tile) if dr == 0 else range(chunks_per_tile - 1, -1, -1)
                    for c in order:
                        crow = pl.ds(pl.multiple_of(r0 + c * HG_CHUNK, HG_CHUNK), HG_CHUNK)
                        o_inter[c] = _mm_nt(qp_scr[crow, lanes], s_t)
                        upd = _mm_tn(v_scr[crow, lanes], kp_scr[crow, lanes])
                        dec = dec_scr[ti * chunks_per_tile + c][:, lanes]
                        s_t = s_t * dec + upd
                    o_tile = o_intra + jnp.concatenate(o_inter, axis=0)
                    if dr == 0:
                        o_scr[rows, lanes] = o_tile
                    else:
                        o_scr[rows, lanes] += o_tile
                    return s_t

                if n_tiles == 1:
                    s_t = tile_body(0, s_t0)
                else:
                    s_t = lax.fori_loop(0, n_tiles, tile_body, s_t0)
                if emit_state:
                    s_out_ref[0, 0, dr, head] = s_t.T

        g = _mm(h, w_in_ref[:, 4 * hk + c0:4 * hk + c0 + pair])
        for h2 in range(2):
            lanes = slice(h2 * LANES, (h2 + 1) * LANES)
            o = o_scr[:, lanes]
            ms = jnp.mean(o * o, axis=-1, keepdims=True)
            on = o * lax.rsqrt(ms + RMS_EPS) * ng_ref[:, c0 + h2 * LANES:c0 + (h2 + 1) * LANES]
            u_scr[:, c0 + h2 * LANES:c0 + (h2 + 1) * LANES] = (on * _silu(g[:, lanes])).astype(BF16)

    y = _mm(u_scr[...], w_out_ref[...])
    out_ref[0] = _residual_ln(x, y, gate, lng_ref[0], lnb_ref[0])


def _hgrn_layer(x, mod, layer, ln_g, ln_b, w_in, lb_logits, norm_g, w_out, state, *, row_of_batch):
    bsz, n, d = x.shape
    hv = w_out.shape[0]
    has_s0 = state is not None
    emit_state = not has_s0
    pair = 2 * LANES
    state_block = (1, 1, 2, HG_HEADS, HG_KDIM, HG_VDIM)
    state_spec = pl.BlockSpec(state_block, lambda b: (b, 0, 0, 0, 0, 0))
    in_specs = [
        _seq_spec(n, d),
        pl.BlockSpec((1, 3, SUBLANES, d), lambda b: (layer, 0, 0, 0)),
        pl.BlockSpec((1, 1, d), lambda b: (layer, 0, 0)),
        pl.BlockSpec((1, 1, d), lambda b: (layer, 0, 0)),
        _const_spec(w_in.shape),
        _const_spec(lb_logits.shape),
        _const_spec(norm_g.shape),
        _const_spec(w_out.shape),
    ]
    args = [x, mod, ln_g, ln_b, w_in, lb_logits, norm_g, w_out]
    out_specs = [_seq_spec(n, d)]
    out_shape = [jax.ShapeDtypeStruct((bsz, n, d), F32)]
    if has_s0:
        in_specs.append(state_spec)
        args.append(state)
    if emit_state:
        out_specs.append(state_spec)
        out_shape.append(jax.ShapeDtypeStruct((bsz,) + state_block[1:], F32))
    kernel = functools.partial(_hgrn_kernel, row_of_batch=row_of_batch, layer=layer,
                               has_s0=has_s0, emit_state=emit_state)
    res = pl.pallas_call(
        kernel,
        grid=(bsz,),
        in_specs=in_specs,
        out_specs=out_specs,
        out_shape=out_shape,
        scratch_shapes=[
            pltpu.VMEM((n, d), BF16),
            pltpu.VMEM((n, pair), BF16),
            pltpu.VMEM((n, pair), BF16),
            pltpu.VMEM((n, pair), BF16),
            pltpu.VMEM((n, pair), BF16),
            pltpu.VMEM((n, pair), BF16),
            pltpu.VMEM((n // HG_CHUNK, 1, pair), F32),
            pltpu.VMEM((n, pair), F32),
            pltpu.VMEM((n, hv), BF16),
        ],
        compiler_params=_compiler_params(1),
        name="hgrn2_layer",
    )(*args)
    return res[0] if has_s0 else tuple(res)


def _block_scan(a, x, reverse):
    n, c = a.shape
    nb = n // SUBLANES
    a = a.reshape(nb, SUBLANES, c)
    x = x.reshape(nb, SUBLANES, c)
    pos = lax.broadcasted_iota(jnp.int32, (1, SUBLANES, 1), 1)
    s = 1
    while s < SUBLANES:
        if reverse:
            keep = pos < SUBLANES - s
            shift = SUBLANES - s
        else:
            keep = pos >= s
            shift = s
        a_in = jnp.where(keep, pltpu.roll(a, shift, 1), 1.0)
        x_in = jnp.where(keep, pltpu.roll(x, shift, 1), 0.0)
        x = a * x_in + x
        a = a * a_in
        s *= 2
    return a.reshape(n, c), x.reshape(n, c)


def _rglru_kernel(*refs, row_of_batch, has_s0, emit_state):
    it = iter(refs)
    (x_ref, mod_ref, lng_ref, lnb_ref, w_in_ref, cw_ref, cb_ref, wg_ref, bg_ref, lam_ref,
     w_out_ref) = (next(it) for _ in range(11))
    s0_ref = next(it) if has_s0 else None
    out_ref = next(it)
    s_out_ref = next(it) if emit_state else None
    h_scr, a_scr, b_scr, gs_scr = it

    n, d = x_ref.shape[1], x_ref.shape[2]
    width = w_out_ref.shape[0]
    blk = width // RG_HEADS
    nb = n // SUBLANES

    row = pl.program_id(0) + 1 if row_of_batch else 0
    shift, scale, gate = _mod_rows(mod_ref, row)
    h_scr[...] = (x_ref[0] * (1.0 + scale) + shift).astype(BF16)

    z = -lam_ref[...]
    coef = -RG_C * (jnp.maximum(z, 0.0) + jnp.log1p(jnp.exp(-jnp.abs(z))))

    t = lax.broadcasted_iota(jnp.int32, (n, 1), 0)
    for hh in range(RG_HEADS):
        cols = slice(hh * blk, (hh + 1) * blk)
        h = h_scr[...]
        u_pre = _mm(h, w_in_ref[:, cols])
        u = (cw_ref[0:1, cols] * jnp.where(t >= 2, pltpu.roll(u_pre, 2, 0), 0.0)
             + cw_ref[1:2, cols] * jnp.where(t >= 1, pltpu.roll(u_pre, 1, 0), 0.0)
             + cw_ref[2:3, cols] * u_pre
             + cw_ref[3:4, cols] * jnp.where(t < n - 1, pltpu.roll(u_pre, n - 1, 0), 0.0)
             + cb_ref[:, cols])
        gs_scr[:, cols] = _silu(_mm(h, w_in_ref[:, width + hh * blk:width + (hh + 1) * blk]))
        for dr in range(2):
            gates = jax.nn.sigmoid(_mm(u, wg_ref[dr, hh]) + bg_ref[dr, hh])
            a = jnp.exp(coef[dr:dr + 1, cols] * gates[:, :blk])
            xin = jnp.sqrt(1.0 - a * a) * (gates[:, blk:] * u)
            a_blk, h_blk = _block_scan(a, xin, reverse=(dr == 1))
            a_scr[dr, :, cols] = a_blk
            b_scr[dr, :, cols] = h_blk

    if has_s0:
        carry0 = (s0_ref[0, 0, 0:1, :], s0_ref[0, 0, 1:2, :])
    else:
        carry0 = (jnp.zeros((1, width), F32), jnp.zeros((1, width), F32))

    def carry_body(i, carry):
        cf, cb = carry
        rf = pl.ds(pl.multiple_of(i * SUBLANES, SUBLANES), SUBLANES)
        rb = pl.ds(pl.multiple_of((nb - 1 - i) * SUBLANES, SUBLANES), SUBLANES)
        hf = a_scr[0, rf, :] * cf + b_scr[0, rf, :]
        hb = a_scr[1, rb, :] * cb + b_scr[1, rb, :]
        b_scr[0, rf, :] = hf
        b_scr[1, rb, :] = hb
        return hf[SUBLANES - 1:SUBLANES, :], hb[0:1, :]

    cf, cb = lax.fori_loop(0, nb, carry_body, carry0)
    if emit_state:
        s_out_ref[0, 0, 0:1, :] = cf
        s_out_ref[0, 0, 1:2, :] = cb

    for r0 in range(0, n, ROW_TILE):
        rows = slice(r0, r0 + ROW_TILE)
        mix = ((b_scr[0, rows, :] + b_scr[1, rows, :]) * gs_scr[rows, :]).astype(BF16)
        y = _mm(mix, w_out_ref[...])
        out_ref[0, rows, :] = _residual_ln(x_ref[0, rows, :], y, gate, lng_ref[0], lnb_ref[0])


def _rglru_layer(x, mod, layer, ln_g, ln_b, w_in, conv_w, conv_b, w_gate, b_gate, lam, w_out, state,
                 *, row_of_batch):
    bsz, n, d = x.shape
    width = w_out.shape[0]
    has_s0 = state is not None
    emit_state = not has_s0
    b_gate = b_gate.reshape(2, RG_HEADS, 1, b_gate.shape[-1])
    state_block = (1, 1, 2, width)
    state_spec = pl.BlockSpec(state_block, lambda b: (b, 0, 0, 0))
    in_specs = [
        _seq_spec(n, d),
        pl.BlockSpec((1, 3, SUBLANES, d), lambda b: (layer, 0, 0, 0)),
        pl.BlockSpec((1, 1, d), lambda b: (layer, 0, 0)),
        pl.BlockSpec((1, 1, d), lambda b: (layer, 0, 0)),
        _const_spec(w_in.shape),
        _const_spec(conv_w.shape),
        _const_spec(conv_b.shape),
        _const_spec(w_gate.shape),
        _const_spec(b_gate.shape),
        _const_spec(lam.shape),
        _const_spec(w_out.shape),
    ]
    args = [x, mod, ln_g, ln_b, w_in, conv_w, conv_b, w_gate, b_gate, lam, w_out]
    out_specs = [_seq_spec(n, d)]
    out_shape = [jax.ShapeDtypeStruct((bsz, n, d), F32)]
    if has_s0:
        in_specs.append(state_spec)
        args.append(state)
    if emit_state:
        out_specs.append(state_spec)
        out_shape.append(jax.ShapeDtypeStruct((bsz,) + state_block[1:], F32))
    kernel = functools.partial(_rglru_kernel, row_of_batch=row_of_batch, has_s0=has_s0,
                               emit_state=emit_state)
    res = pl.pallas_call(
        kernel,
        grid=(bsz,),
        in_specs=in_specs,
        out_specs=out_specs,
        out_shape=out_shape,
        scratch_shapes=[
            pltpu.VMEM((n, d), BF16),
            pltpu.VMEM((2, n, width), F32),
            pltpu.VMEM((2, n, width), F32),
            pltpu.VMEM((n, width), F32),
        ],
        compiler_params=_compiler_params(1),
        name="rglru_layer",
    )(*args)
    return res[0] if has_s0 else tuple(res)


MLA_QK = MLA_NOPE + MLA_ROPE


def _mla_weights(w_in, w_qb, w_kvb, w_out):
    d = w_in.shape[0]
    a = MLA_Q_RANK + MLA_KV_RANK
    pad_l, pad_r = MLA_NOPE, MLA_HEAD_PAD - MLA_QK
    w_in_r = jnp.concatenate(
        [w_in[:, :a], jnp.zeros((d, pad_l), w_in.dtype), w_in[:, a:a + MLA_ROPE],
         jnp.zeros((d, pad_r), w_in.dtype), w_in[:, a + MLA_ROPE:]], axis=1)
    wq = w_qb.reshape(MLA_Q_RANK, MLA_HEADS, MLA_QK)
    wq = jnp.pad(wq, ((0, 0), (0, 0), (0, pad_r))).reshape(MLA_Q_RANK, MLA_HEADS * MLA_HEAD_PAD)
    wkv = w_kvb.reshape(MLA_KV_RANK, MLA_HEADS, MLA_NOPE + MLA_VDIM)
    wk = jnp.pad(wkv[:, :, :MLA_NOPE], ((0, 0), (0, 0), (0, MLA_HEAD_PAD - MLA_NOPE)))
    wk = wk.reshape(MLA_KV_RANK, MLA_HEADS * MLA_HEAD_PAD)
    wv = wkv[:, :, MLA_NOPE:].reshape(MLA_KV_RANK, MLA_HEADS * MLA_VDIM)
    return tuple(w.astype(BF16) for w in (w_in_r, wq, wk, wv, w_out))


def _rope_tables(n):
    half = MLA_ROPE // 2
    quarter = half // 2
    t = np.arange(n)
    inv_freq = ROPE_BASE ** (-np.arange(0, half, 2, dtype=np.float32) / half)
    cos = np.ones((n, MLA_HEAD_PAD), np.float32)
    sin = np.zeros((n, MLA_HEAD_PAD), np.float32)
    for k, pos in enumerate((t // GRID_W, t % GRID_W)):
        ang = pos[:, None].astype(np.float32) * inv_freq[None].astype(np.float32)
        c, s = np.cos(ang), np.sin(ang)
        base = MLA_NOPE + k * half
        cos[:, base:base + quarter] = c
        cos[:, base + quarter:base + half] = c
        sin[:, base:base + quarter] = -s
        sin[:, base + quarter:base + half] = s
    return jnp.asarray(cos), jnp.asarray(sin)


def _rope(x, cos, sin):
    quarter = MLA_ROPE // 4
    lane = lax.broadcasted_iota(jnp.int32, (1, MLA_HEAD_PAD), 1)
    first = ((lane - MLA_NOPE) % (2 * quarter)) < quarter
    partner = jnp.where(first, pltpu.roll(x, MLA_HEAD_PAD - quarter, 1), pltpu.roll(x, quarter, 1))
    return x * cos + partner * sin


def _rms(x, g):
    return x * lax.rsqrt(jnp.mean(x * x, axis=-1, keepdims=True) + RMS_EPS) * g


def _mla_kernel(*refs, row_of_batch, has_ctx):
    it = iter(refs)
    (x_ref, mod_ref, lng_ref, lnb_ref, w_in_ref, qn_ref, kvn_ref, wq_ref, wk_ref, wv_ref,
     w_out_ref) = (next(it) for _ in range(11))
    if has_ctx:
        ctx_ckv_ref, ctx_kpe_ref, cos_ref, sin_ref = (next(it) for _ in range(4))
    out_ref = next(it)
    if not has_ctx:
        ckv_out_ref, kpe_out_ref = next(it), next(it)
    h_scr, q_scr, k_scr, v_scr, o_scr = it

    n = x_ref.shape[1]
    n_ctx = k_scr.shape[0] - n
    hp = MLA_HEAD_PAD
    c_ckv = MLA_Q_RANK
    c_kpe = c_ckv + MLA_KV_RANK
    c_g = c_kpe + hp

    row = pl.program_id(0) + 1 if row_of_batch else 0
    shift, scale, gate = _mod_rows(mod_ref, row)
    sm_scale = MLA_QK ** -0.5
    n_tiles = n // ROW_TILE

    def store_keys(ckv, kpe, krows):
        k_all = _mm(ckv, wk_ref[...])
        v_scr[krows, :] = _mm(ckv, wv_ref[...]).astype(BF16)
        for hd in range(MLA_HEADS):
            cols = slice(hd * hp, (hd + 1) * hp)
            k_scr[krows, cols] = (k_all[:, cols] + kpe).astype(BF16)

    def proj_body(ti, carry):
        r0 = pl.multiple_of(ti * ROW_TILE, ROW_TILE)
        rows = pl.ds(r0, ROW_TILE)
        h = (x_ref[0, rows, :] * (1.0 + scale) + shift).astype(BF16)
        h_scr[rows, :] = h
        cq = _rms(_mm(h, w_in_ref[:, :c_ckv]), qn_ref[...])
        ckv = _rms(_mm(h, w_in_ref[:, c_ckv:c_kpe]), kvn_ref[...])
        kpe = _mm(h, w_in_ref[:, c_kpe:c_g])
        if has_ctx:
            cos, sin = cos_ref[rows, :], sin_ref[rows, :]
            kpe = _rope(kpe, cos, sin)
        else:
            ckv_out_ref[0, 0, rows, :] = ckv
            kpe_out_ref[0, 0, rows, :] = kpe[:, MLA_NOPE:MLA_QK]
        q_all = _mm(cq, wq_ref[...])
        for hd in range(MLA_HEADS):
            cols = slice(hd * hp, (hd + 1) * hp)
            qh = q_all[:, cols]
            if has_ctx:
                qh = _rope(qh, cos, sin)
            q_scr[rows, cols] = (qh * sm_scale).astype(BF16)
        store_keys(ckv, kpe, pl.ds(pl.multiple_of(n_ctx + r0, ROW_TILE), ROW_TILE))
        return carry

    def ctx_body(ti, carry):
        rows = pl.ds(pl.multiple_of(ti * ROW_TILE, ROW_TILE), ROW_TILE)
        store_keys(ctx_ckv_ref[0, 0, rows, :], ctx_kpe_ref[0, 0, rows, :], rows)
        return carry

    if n_tiles == 1:
        proj_body(0, 0)
    else:
        lax.fori_loop(0, n_tiles, proj_body, 0)
    if has_ctx:
        lax.fori_loop(0, n_ctx // ROW_TILE, ctx_body, 0)

    def tile_body(ti, carry):
        rows = pl.ds(pl.multiple_of(ti * ROW_TILE, ROW_TILE), ROW_TILE)
        for hd2 in range(MLA_HEADS // 2):
            o_pair = []
            for hd in (2 * hd2, 2 * hd2 + 1):
                cols = slice(hd * hp, (hd + 1) * hp)
                s = _mm_nt(q_scr[rows, cols], k_scr[:, cols])
                e = jnp.exp(s - jnp.max(s, axis=-1, keepdims=True))
                denom = jnp.sum(e, axis=-1, keepdims=True)
                o_pair.append(_mm(e, v_scr[:, hd * MLA_VDIM:(hd + 1) * MLA_VDIM]) / denom)
            o_scr[:, hd2 * 2 * MLA_VDIM:(hd2 + 1) * 2 * MLA_VDIM] = jnp.concatenate(o_pair, axis=-1)
        g = _mm(h_scr[rows, :], w_in_ref[:, c_g:])
        y = _mm(o_scr[...] * _silu(g), w_out_ref[...])
        out_ref[0, rows, :] = _residual_ln(x_ref[0, rows, :], y, gate, lng_ref[0], lnb_ref[0])
        return carry

    if n_tiles == 1:
        tile_body(0, 0)
    else:
        lax.fori_loop(0, n_tiles, tile_body, 0)


def _mla_layer(x, mod, layer, ln_g, ln_b, weights, q_norm, kv_norm, ctx_ckv, ctx_kpe, *, row_of_batch):
    bsz, n, d = x.shape
    w_in, wq, wk, wv, w_out = weights
    has_ctx = ctx_ckv is not None
    in_specs = [
        _seq_spec(n, d),
        pl.BlockSpec((1, 3, SUBLANES, d), lambda b: (layer, 0, 0, 0)),
        pl.BlockSpec((1, 1, d), lambda b: (layer, 0, 0)),
        pl.BlockSpec((1, 1, d), lambda b: (layer, 0, 0)),
        _const_spec(w_in.shape),
        _const_spec(q_norm.shape),
        _const_spec(kv_norm.shape),
        _const_spec(wq.shape),
        _const_spec(wk.shape),
        _const_spec(wv.shape),
        _const_spec(w_out.shape),
    ]
    args = [x, mod, ln_g, ln_b, w_in, q_norm, kv_norm, wq, wk, wv, w_out]
    out_specs = [_seq_spec(n, d)]
    out_shape = [jax.ShapeDtypeStruct((bsz, n, d), F32)]
    n_ctx = 0
    if has_ctx:
        n_ctx = ctx_ckv.shape[2]
        ctx_kpe = jnp.pad(ctx_kpe, ((0, 0), (0, 0), (0, 0), (MLA_NOPE, MLA_HEAD_PAD - MLA_QK)))
        cos, sin = _rope_tables(n)
        in_specs += [
            pl.BlockSpec((1, 1, n_ctx, MLA_KV_RANK), lambda b: (b, 0, 0, 0)),
            pl.BlockSpec((1, 1, n_ctx, MLA_HEAD_PAD), lambda b: (b, 0, 0, 0)),
            _const_spec(cos.shape),
            _const_spec(sin.shape),
        ]
        args += [ctx_ckv, ctx_kpe, cos, sin]
    else:
        out_specs += [pl.BlockSpec((1, 1, n, MLA_KV_RANK), lambda b: (b, 0, 0, 0)),
                      pl.BlockSpec((1, 1, n, MLA_ROPE), lambda b: (b, 0, 0, 0))]
        out_shape += [jax.ShapeDtypeStruct((bsz, 1, n, MLA_KV_RANK), F32),
                      jax.ShapeDtypeStruct((bsz, 1, n, MLA_ROPE), F32)]
    kernel = functools.partial(_mla_kernel, row_of_batch=row_of_batch, has_ctx=has_ctx)
    res = pl.pallas_call(
        kernel,
        grid=(bsz,),
        in_specs=in_specs,
        out_specs=out_specs,
        out_shape=out_shape,
        scratch_shapes=[
            pltpu.VMEM((n, d), BF16),
            pltpu.VMEM((n, MLA_HEADS * MLA_HEAD_PAD), BF16),
            pltpu.VMEM((n_ctx + n, MLA_HEADS * MLA_HEAD_PAD), BF16),
            pltpu.VMEM((n_ctx + n, MLA_HEADS * MLA_VDIM), BF16),
            pltpu.VMEM((ROW_TILE, MLA_HEADS * MLA_VDIM), F32),
        ],
        compiler_params=_compiler_params(1),
        name="mla_layer",
    )(*args)
    return res[0] if has_ctx else tuple(res)


def kernel(x_prompt, x_sample, state_hgrn, state_rglru, cache_mla_ckv, cache_mla_kpe, c, c_ctx, ada_w, ada_b, ln_g, ln_b, hg_w_in, hg_lb_logits, hg_norm_g, hg_w_out, sc_w_in, sc_conv_w, sc_conv_b, sc_w_out, rg_w_in, rg_conv_w, rg_conv_b, rg_w_gate, rg_b_gate, rg_lambda, rg_w_out, mla_w_in, mla_q_norm, mla_kv_norm, mla_w_qb, mla_w_kvb, mla_w_out):
    d = x_prompt.shape[-1]
    n_dec = c.shape[0]
    cond = jnp.zeros((SUBLANES, d), F32).at[0].set(c_ctx).at[1:1 + n_dec].set(c)
    mod = _modulation(cond, ada_w, ada_b)
    lng = ln_g.reshape(DEPTH, 1, d)
    lnb = ln_b.reshape(DEPTH, 1, d)
    hg_wi, hg_wo = hg_w_in[0].astype(BF16), hg_w_out[0].astype(BF16)
    sc_wi, sc_wo = sc_w_in[0].astype(BF16), sc_w_out[0].astype(BF16)
    rg_wi, rg_wg, rg_wo = rg_w_in[0].astype(BF16), rg_w_gate[0].astype(BF16), rg_w_out[0].astype(BF16)
    mla_w = _mla_weights(mla_w_in[0], mla_w_qb[0], mla_w_kvb[0], mla_w_out[0])

    def run(x, rob, st_hg, st_rg, ctx_ckv, ctx_kpe):
        r0 = _hgrn_layer(x, mod, 0, lng, lnb, hg_wi, hg_lb_logits, hg_norm_g, hg_wo, st_hg, row_of_batch=rob)
        x, new_hg = r0 if st_hg is None else (r0, None)
        x = _sconv_layer(x, mod, 1, lng, lnb, sc_wi, sc_conv_w[0], sc_conv_b, sc_wo, row_of_batch=rob)
        r2 = _rglru_layer(x, mod, 2, lng, lnb, rg_wi, rg_conv_w[0], rg_conv_b, rg_wg, rg_b_gate[0],
                          rg_lambda[0], rg_wo, st_rg, row_of_batch=rob)
        x, new_rg = r2 if st_rg is None else (r2, None)
        r3 = _mla_layer(x, mod, 3, lng, lnb, mla_w, mla_q_norm, mla_kv_norm, ctx_ckv, ctx_kpe,
                        row_of_batch=rob)
        if ctx_ckv is None:
            return (r3[0], new_hg, new_rg, r3[1], r3[2])
        return (r3,)

    y_prompt, new_hg, new_rg, new_ckv, new_kpe = run(x_prompt, False, None, None, None, None)
    (y_sample,) = run(x_sample, True, state_hgrn, state_rglru, cache_mla_ckv, cache_mla_kpe)
    return (y_prompt, y_sample, new_hg, new_rg, new_ckv, new_kpe)
```

```python
import functools

import jax
import jax.numpy as jnp
import numpy as np
from jax import lax
from jax.experimental import pallas as pl
from jax.experimental.pallas import tpu as pltpu

F32 = jnp.float32
BF16 = jnp.bfloat16

LANES = 128
SUBLANES = 8
VMEM_LIMIT_BYTES = 60 * 1024 * 1024

DEPTH = 4
DEEPNORM_ALPHA = (2 * DEPTH) ** 0.25
LN_EPS = 1e-5
RMS_EPS = 1e-6

ROW_TILE = 256

HG_HEADS = 8
HG_KDIM = 128
HG_VDIM = 128
HG_CHUNK = 32
SC_KERNEL = 3
RG_HEADS = 4
RG_KERNEL = 4
RG_C = 8.0
MLA_HEADS = 16
MLA_Q_RANK = 384
MLA_KV_RANK = 256
MLA_NOPE = 64
MLA_ROPE = 32
MLA_VDIM = 64
MLA_HEAD_PAD = 128
ROPE_BASE = 10000.0
GRID_W = 64


def _mm(a, b):
    return jnp.dot(a.astype(BF16), b.astype(BF16), preferred_element_type=F32)


def _mm_nt(a, b):
    return lax.dot_general(a.astype(BF16), b.astype(BF16), (((1,), (1,)), ((), ())),
                           preferred_element_type=F32)


def _mm_tn(a, b):
    return lax.dot_general(a.astype(BF16), b.astype(BF16), (((0,), (0,)), ((), ())),
                           preferred_element_type=F32)


def _silu(x):
    return x * jax.nn.sigmoid(x)


def _mod_rows(mod_ref, row):
    if isinstance(row, int):
        return tuple(mod_ref[0, j, row:row + 1, :] for j in range(3))
    rows = lax.broadcasted_iota(jnp.int32, (SUBLANES, 1), 0)
    return tuple(jnp.sum(jnp.where(rows == row, mod_ref[0, j], 0.0), axis=0, keepdims=True)
                 for j in range(3))


def _residual_ln(x, y, gate, g, b):
    z = DEEPNORM_ALPHA * x + gate * y
    mu = jnp.mean(z, axis=-1, keepdims=True)
    zc = z - mu
    var = jnp.mean(zc * zc, axis=-1, keepdims=True)
    return zc * lax.rsqrt(var + LN_EPS) * g + b


def _const_spec(shape):
    nd = len(shape)
    return pl.BlockSpec(shape, lambda *_: (0,) * nd, pipeline_mode=pl.Buffered(1))


def _seq_spec(n, d):
    mode = dict(pipeline_mode=pl.Buffered(1)) if n > ROW_TILE else {}
    return pl.BlockSpec((1, n, d), lambda b: (b, 0, 0), **mode)


def _compiler_params(n_grid):
    return pltpu.CompilerParams(dimension_semantics=("arbitrary",) * n_grid,
                                vmem_limit_bytes=VMEM_LIMIT_BYTES)


def _mod_kernel(cond_ref, w_ref, b_ref, out_ref):
    s = _silu(cond_ref[...])
    out_ref[0, 0] = _mm(s, w_ref[0]) + b_ref[0, 0]


def _modulation(cond, ada_w, ada_b):
    n_layers, d, _ = ada_w.shape
    rows = cond.shape[0]
    return pl.pallas_call(
        _mod_kernel,
        grid=(n_layers, 3),
        in_specs=[
            pl.BlockSpec((rows, d), lambda l, j: (0, 0)),
            pl.BlockSpec((1, d, d), lambda l, j: (l, 0, j)),
            pl.BlockSpec((1, 1, 1, d), lambda l, j: (l, j, 0, 0)),
        ],
        out_specs=pl.BlockSpec((1, 1, rows, d), lambda l, j: (l, j, 0, 0)),
        out_shape=jax.ShapeDtypeStruct((n_layers, 3, rows, d), F32),
        compiler_params=_compiler_params(2),
        name="adaln_modulation",
    )(cond, ada_w, ada_b.reshape(n_layers, 3, 1, d))


def _sconv_kernel(x_ref, mod_ref, lng_ref, lnb_ref, w_in_ref, cw_ref, cb_ref, w_out_ref,
                  out_ref, u_scr, *, row_of_batch):
    n, d = x_ref.shape[1], x_ref.shape[2]
    width = w_out_ref.shape[0]
    row = pl.program_id(0) + 1 if row_of_batch else 0
    shift, scale, gate = _mod_rows(mod_ref, row)
    x = x_ref[0]
    h = (x * (1.0 + scale) + shift).astype(BF16)
    t = lax.broadcasted_iota(jnp.int32, (n, 1), 0)
    chunk = 2 * LANES
    for j in range(width // chunk):
        c0 = j * chunk
        bg = _mm(h, w_in_ref[:, c0:c0 + chunk])
        cg = _mm(h, w_in_ref[:, width + c0:width + c0 + chunk])
        v = _mm(h, w_in_ref[:, 2 * width + c0:2 * width + c0 + chunk])
        g = _mm(h, w_in_ref[:, 3 * width + c0:3 * width + c0 + chunk])
        p = cg * v
        p_prev = jnp.where(t >= 1, pltpu.roll(p, 1, 0), 0.0)
        p_next = jnp.where(t < n - 1, pltpu.roll(p, n - 1, 0), 0.0)
        z = (cw_ref[0:1, c0:c0 + chunk] * p_prev + cw_ref[1:2, c0:c0 + chunk] * p
             + cw_ref[2:3, c0:c0 + chunk] * p_next + cb_ref[:, c0:c0 + chunk])
        u_scr[:, c0:c0 + chunk] = (_silu(g) * bg * z).astype(BF16)
    y = _mm(u_scr[...], w_out_ref[...])
    out_ref[0] = _residual_ln(x, y, gate, lng_ref[0], lnb_ref[0])


def _sconv_layer(x, mod, layer, ln_g, ln_b, w_in, conv_w, conv_b, w_out, *, row_of_batch):
    bsz, n, d = x.shape
    width = w_out.shape[0]
    kernel = functools.partial(_sconv_kernel, row_of_batch=row_of_batch)
    return pl.pallas_call(
        kernel,
        grid=(bsz,),
        in_specs=[
            _seq_spec(n, d),
            pl.BlockSpec((1, 3, SUBLANES, d), lambda b: (layer, 0, 0, 0)),
            pl.BlockSpec((1, 1, d), lambda b: (layer, 0, 0)),
            pl.BlockSpec((1, 1, d), lambda b: (layer, 0, 0)),
            _const_spec(w_in.shape),
            _const_spec(conv_w.shape),
            _const_spec(conv_b.shape),
            _const_spec(w_out.shape),
        ],
        out_specs=_seq_spec(n, d),
        out_shape=jax.ShapeDtypeStruct((bsz, n, d), F32),
        scratch_shapes=[pltpu.VMEM((n, width), BF16)],
        compiler_params=_compiler_params(1),
        name="sconv_layer",
    )(x, mod, ln_g, ln_b, w_in, conv_w, conv_b, w_out)


def _chunk_cumsum(x, reverse):
    n = x.shape[0]
    pos = lax.broadcasted_iota(jnp.int32, (n, 1), 0) % HG_CHUNK
    s = 1
    while s < HG_CHUNK:
        if reverse:
            x = x + jnp.where(pos < HG_CHUNK - s, pltpu.roll(x, n - s, 0), 0.0)
        else:
            x = x + jnp.where(pos >= s, pltpu.roll(x, s, 0), 0.0)
        s *= 2
    return x


def _hgrn_kernel(*refs, row_of_batch, layer, has_s0, emit_state):
    it = iter(refs)
    x_ref, mod_ref, lng_ref, lnb_ref, w_in_ref, lbl_ref, ng_ref, w_out_ref = (next(it) for _ in range(8))
    s0_ref = next(it) if has_s0 else None
    out_ref = next(it)
    s_out_ref = next(it) if emit_state else None
    h_scr, qrel_scr, krel_scr, qp_scr, kp_scr, v_scr, dec_scr, o_scr, u_scr = it

    n, d = x_ref.shape[1], x_ref.shape[2]
    hk = HG_HEADS * HG_KDIM
    n_tiles = n // ROW_TILE
    chunks_per_tile = ROW_TILE // HG_CHUNK
    n_chunks = n // HG_CHUNK
    pair = 2 * LANES

    row = pl.program_id(0) + 1 if row_of_batch else 0
    shift, scale, gate = _mod_rows(mod_ref, row)
    x = x_ref[0]
    h_scr[...] = (x * (1.0 + scale) + shift).astype(BF16)

    lbs = []
    for dr in range(2):
        z = lbl_ref[dr]
        e = jnp.exp(z - jnp.max(z, axis=0, keepdims=True))
        lbs.append(jnp.sum(e[:layer + 1], axis=0, keepdims=True) / jnp.sum(e, axis=0, keepdims=True))

    ri = lax.broadcasted_iota(jnp.int32, (ROW_TILE, ROW_TILE), 0)
    ci = lax.broadcasted_iota(jnp.int32, (ROW_TILE, ROW_TILE), 1)
    same_chunk = (ri // HG_CHUNK) == (ci // HG_CHUNK)
    masks = (same_chunk & (ci <= ri), same_chunk & (ci >= ri))

    row_chunk = (lax.broadcasted_iota(jnp.int32, (ROW_TILE, LANES), 0) // HG_CHUNK).astype(BF16)

    def chunk_blocks(a):
        zero = jnp.zeros_like(a)
        return jnp.concatenate([jnp.where(row_chunk == c, a, zero) for c in range(chunks_per_tile)], axis=1)

    for hp in range(hk // pair):
        c0 = hp * pair
        slot = hp % 2
        h = h_scr[...]
        q = _silu(_mm(h, w_in_ref[:, c0:c0 + pair])) * (HG_KDIM ** -0.5)
        v_scr[slot] = _mm(h, w_in_ref[:, 3 * hk + c0:3 * hk + c0 + pair]).astype(BF16)
        q3 = q.reshape(n_chunks, HG_CHUNK, pair)
        for dr in range(2):
            fl = _mm(h, w_in_ref[:, (1 + dr) * hk + c0:(1 + dr) * hk + c0 + pair])
            lb = lbs[dr][:, c0:c0 + pair]
            f = lb + (1.0 - lb) * jax.nn.sigmoid(fl)
            k3 = (1.0 - f).reshape(n_chunks, HG_CHUNK, pair)
            cum = _chunk_cumsum(jnp.log(f), reverse=(dr == 1)).reshape(n_chunks, HG_CHUNK, pair)
            mid = HG_CHUNK // 2 - 1
            if dr == 0:
                ref, tot = cum[:, mid:mid + 1], cum[:, HG_CHUNK - 1:HG_CHUNK]
            else:
                ref, tot = cum[:, HG_CHUNK - 1 - mid:HG_CHUNK - mid], cum[:, 0:1]
            q_rel = q3 * jnp.exp(cum - ref)
            k_rel = k3 * jnp.exp(ref - cum)
            qrel_scr[slot, dr] = q_rel.reshape(n, pair).astype(BF16)
            krel_scr[slot, dr] = k_rel.reshape(n, pair).astype(BF16)
            qp_scr[slot, dr] = (q_rel * jnp.exp(ref)).reshape(n, pair).astype(BF16)
            kp_scr[slot, dr] = (k_rel * jnp.exp(tot - ref)).reshape(n, pair).astype(BF16)
            dec_scr[slot, dr] = jnp.exp(tot)

            for h2 in range(2):
                l0 = h2 * LANES
                head = hp * 2 + h2
                if has_s0:
                    s_t0 = s0_ref[0, 0, dr, head].T
                else:
                    s_t0 = jnp.zeros((HG_VDIM, HG_KDIM), F32)

                def tile_body(i, s_t, dr=dr, l0=l0, slot=slot):
                    ti = i if dr == 0 else n_tiles - 1 - i
                    rows = pl.ds(pl.multiple_of(ti * ROW_TILE, ROW_TILE), ROW_TILE)
                    lanes = slice(l0, l0 + LANES)
                    v = v_scr[slot, rows, lanes]
                    att = _mm_nt(qrel_scr[slot, dr, rows, lanes], krel_scr[slot, dr, rows, lanes])
                    o_tile = _mm(jnp.where(masks[dr], att, 0.0), v)
                    upd = _mm_tn(v, chunk_blocks(kp_scr[slot, dr, rows, lanes]))
                    s_prev = [None] * chunks_per_tile
                    order = range(chunks_per_tile) if dr == 0 else range(chunks_per_tile - 1, -1, -1)
                    for c in order:
                        s_prev[c] = s_t.astype(BF16)
                        dec = dec_scr[slot, dr, ti * chunks_per_tile + c][:, lanes]
                        s_t = s_t * dec + upd[:, c * LANES:(c + 1) * LANES]
                    o_tile += _mm_nt(chunk_blocks(qp_scr[slot, dr, rows, lanes]),
                                     jnp.concatenate(s_prev, axis=1))
                    if dr == 0:
                        o_scr[slot, rows, lanes] = o_tile
                    else:
                        o_scr[slot, rows, lanes] += o_tile
                    return s_t

                if n_tiles == 1:
                    s_t = tile_body(0, s_t0)
                else:
                    s_t = lax.fori_loop(0, n_tiles, tile_body, s_t0)
                if emit_state:
                    s_out_ref[0, 0, dr, head] = s_t.T

        g = _mm(h, w_in_ref[:, 4 * hk + c0:4 * hk + c0 + pair])
        for h2 in range(2):
            lanes = slice(h2 * LANES, (h2 + 1) * LANES)
            o = o_scr[slot, :, lanes]
            ms = jnp.mean(o * o, axis=-1, keepdims=True)
            on = o * lax.rsqrt(ms + RMS_EPS) * ng_ref[:, c0 + h2 * LANES:c0 + (h2 + 1) * LANES]
            u_scr[:, c0 + h2 * LANES:c0 + (h2 + 1) * LANES] = (on * _silu(g[:, lanes])).astype(BF16)

    y = _mm(u_scr[...], w_out_ref[...])
    out_ref[0] = _residual_ln(x, y, gate, lng_ref[0], lnb_ref[0])


def _hgrn_layer(x, mod, layer, ln_g, ln_b, w_in, lb_logits, norm_g, w_out, state, *, row_of_batch):
    bsz, n, d = x.shape
    hv = w_out.shape[0]
    has_s0 = state is not None
    emit_state = not has_s0
    pair = 2 * LANES
    state_block = (1, 1, 2, HG_HEADS, HG_KDIM, HG_VDIM)
    state_spec = pl.BlockSpec(state_block, lambda b: (b, 0, 0, 0, 0, 0))
    in_specs = [
        _seq_spec(n, d),
        pl.BlockSpec((1, 3, SUBLANES, d), lambda b: (layer, 0, 0, 0)),
        pl.BlockSpec((1, 1, d), lambda b: (layer, 0, 0)),
        pl.BlockSpec((1, 1, d), lambda b: (layer, 0, 0)),
        _const_spec(w_in.shape),
        _const_spec(lb_logits.shape),
        _const_spec(norm_g.shape),
        _const_spec(w_out.shape),
    ]
    args = [x, mod, ln_g, ln_b, w_in, lb_logits, norm_g, w_out]
    out_specs = [_seq_spec(n, d)]
    out_shape = [jax.ShapeDtypeStruct((bsz, n, d), F32)]
    if has_s0:
        in_specs.append(state_spec)
        args.append(state)
    if emit_state:
        out_specs.append(state_spec)
        out_shape.append(jax.ShapeDtypeStruct((bsz,) + state_block[1:], F32))
    kernel = functools.partial(_hgrn_kernel, row_of_batch=row_of_batch, layer=layer,
                               has_s0=has_s0, emit_state=emit_state)
    res = pl.pallas_call(
        kernel,
        grid=(bsz,),
        in_specs=in_specs,
        out_specs=out_specs,
        out_shape=out_shape,
        scratch_shapes=[
            pltpu.VMEM((n, d), BF16),
            pltpu.VMEM((2, 2, n, pair), BF16),
            pltpu.VMEM((2, 2, n, pair), BF16),
            pltpu.VMEM((2, 2, n, pair), BF16),
            pltpu.VMEM((2, 2, n, pair), BF16),
            pltpu.VMEM((2, n, pair), BF16),
            pltpu.VMEM((2, 2, n // HG_CHUNK, 1, pair), F32),
            pltpu.VMEM((2, n, pair), F32),
            pltpu.VMEM((n, hv), BF16),
        ],
        compiler_params=_compiler_params(1),
        name="hgrn2_layer",
    )(*args)
    return res[0] if has_s0 else tuple(res)


def _block_scan(a, x, reverse):
    n, c = a.shape
    nb = n // SUBLANES
    a = a.reshape(nb, SUBLANES, c)
    x = x.reshape(nb, SUBLANES, c)
    pos = lax.broadcasted_iota(jnp.int32, (1, SUBLANES, 1), 1)
    s = 1
    while s < SUBLANES:
        if reverse:
            keep = pos < SUBLANES - s
            shift = SUBLANES - s
        else:
            keep = pos >= s
            shift = s
        a_in = jnp.where(keep, pltpu.roll(a, shift, 1), 1.0)
        x_in = jnp.where(keep, pltpu.roll(x, shift, 1), 0.0)
        x = a * x_in + x
        a = a * a_in
        s *= 2
    return a.reshape(n, c), x.reshape(n, c)


def _rglru_kernel(*refs, row_of_batch, has_s0, emit_state):
    it = iter(refs)
    (x_ref, mod_ref, lng_ref, lnb_ref, w_in_ref, cw_ref, cb_ref, wg_ref, bg_ref, lam_ref,
     w_out_ref) = (next(it) for _ in range(11))
    s0_ref = next(it) if has_s0 else None
    out_ref = next(it)
    s_out_ref = next(it) if emit_state else None
    h_scr, a_scr, b_scr, gs_scr = it

    n, d = x_ref.shape[1], x_ref.shape[2]
    width = w_out_ref.shape[0]
    blk = width // RG_HEADS
    nb = n // SUBLANES

    row = pl.program_id(0) + 1 if row_of_batch else 0
    shift, scale, gate = _mod_rows(mod_ref, row)
    h_scr[...] = (x_ref[0] * (1.0 + scale) + shift).astype(BF16)

    z = -lam_ref[...]
    coef = -RG_C * (jnp.maximum(z, 0.0) + jnp.log1p(jnp.exp(-jnp.abs(z))))

    t = lax.broadcasted_iota(jnp.int32, (n, 1), 0)
    for hh in range(RG_HEADS):
        cols = slice(hh * blk, (hh + 1) * blk)
        h = h_scr[...]
        u_pre = _mm(h, w_in_ref[:, cols])
        u = (cw_ref[0:1, cols] * jnp.where(t >= 2, pltpu.roll(u_pre, 2, 0), 0.0)
             + cw_ref[1:2, cols] * jnp.where(t >= 1, pltpu.roll(u_pre, 1, 0), 0.0)
             + cw_ref[2:3, cols] * u_pre
             + cw_ref[3:4, cols] * jnp.where(t < n - 1, pltpu.roll(u_pre, n - 1, 0), 0.0)
             + cb_ref[:, cols])
        gs_scr[:, cols] = _silu(_mm(h, w_in_ref[:, width + hh * blk:width + (hh + 1) * blk]))
        for dr in range(2):
            gates = jax.nn.sigmoid(_mm(u, wg_ref[dr, hh]) + bg_ref[dr, hh])
            a = jnp.exp(coef[dr:dr + 1, cols] * gates[:, :blk])
            xin = jnp.sqrt(1.0 - a * a) * (gates[:, blk:] * u)
            a_blk, h_blk = _block_scan(a, xin, reverse=(dr == 1))
            a_scr[dr, :, cols] = a_blk
            b_scr[dr, :, cols] = h_blk

    if has_s0:
        carry0 = (s0_ref[0, 0, 0:1, :], s0_ref[0, 0, 1:2, :])
    else:
        carry0 = (jnp.zeros((1, width), F32), jnp.zeros((1, width), F32))

    def carry_body(i, carry):
        cf, cb = carry
        rf = pl.ds(pl.multiple_of(i * SUBLANES, SUBLANES), SUBLANES)
        rb = pl.ds(pl.multiple_of((nb - 1 - i) * SUBLANES, SUBLANES), SUBLANES)
        hf = a_scr[0, rf, :] * cf + b_scr[0, rf, :]
        hb = a_scr[1, rb, :] * cb + b_scr[1, rb, :]
        b_scr[0, rf, :] = hf
        b_scr[1, rb, :] = hb
        return hf[SUBLANES - 1:SUBLANES, :], hb[0:1, :]

    cf, cb = lax.fori_loop(0, nb, carry_body, carry0)
    if emit_state:
        s_out_ref[0, 0, 0:1, :] = cf
        s_out_ref[0, 0, 1:2, :] = cb

    for r0 in range(0, n, ROW_TILE):
        rows = slice(r0, r0 + ROW_TILE)
        mix = ((b_scr[0, rows, :] + b_scr[1, rows, :]) * gs_scr[rows, :]).astype(BF16)
        y = _mm(mix, w_out_ref[...])
        out_ref[0, rows, :] = _residual_ln(x_ref[0, rows, :], y, gate, lng_ref[0], lnb_ref[0])


def _rglru_layer(x, mod, layer, ln_g, ln_b, w_in, conv_w, conv_b, w_gate, b_gate, lam, w_out, state,
                 *, row_of_batch):
    bsz, n, d = x.shape
    width = w_out.shape[0]
    has_s0 = state is not None
    emit_state = not has_s0
    b_gate = b_gate.reshape(2, RG_HEADS, 1, b_gate.shape[-1])
    state_block = (1, 1, 2, width)
    state_spec = pl.BlockSpec(state_block, lambda b: (b, 0, 0, 0))
    in_specs = [
        _seq_spec(n, d),
        pl.BlockSpec((1, 3, SUBLANES, d), lambda b: (layer, 0, 0, 0)),
        pl.BlockSpec((1, 1, d), lambda b: (layer, 0, 0)),
        pl.BlockSpec((1, 1, d), lambda b: (layer, 0, 0)),
        _const_spec(w_in.shape),
        _const_spec(conv_w.shape),
        _const_spec(conv_b.shape),
        _const_spec(w_gate.shape),
        _const_spec(b_gate.shape),
        _const_spec(lam.shape),
        _const_spec(w_out.shape),
    ]
    args = [x, mod, ln_g, ln_b, w_in, conv_w, conv_b, w_gate, b_gate, lam, w_out]
    out_specs = [_seq_spec(n, d)]
    out_shape = [jax.ShapeDtypeStruct((bsz, n, d), F32)]
    if has_s0:
        in_specs.append(state_spec)
        args.append(state)
    if emit_state:
        out_specs.append(state_spec)
        out_shape.append(jax.ShapeDtypeStruct((bsz,) + state_block[1:], F32))
    kernel = functools.partial(_rglru_kernel, row_of_batch=row_of_batch, has_s0=has_s0,
                               emit_state=emit_state)
    res = pl.pallas_call(
        kernel,
        grid=(bsz,),
        in_specs=in_specs,
        out_specs=out_specs,
        out_shape=out_shape,
        scratch_shapes=[
            pltpu.VMEM((n, d), BF16),
            pltpu.VMEM((2, n, width), F32),
            pltpu.VMEM((2, n, width), F32),
            pltpu.VMEM((n, width), F32),
        ],
        compiler_params=_compiler_params(1),
        name="rglru_layer",
    )(*args)
    return res[0] if has_s0 else tuple(res)


MLA_QK = MLA_NOPE + MLA_ROPE


def _mla_weights(w_in, w_qb, w_kvb, w_out):
    d = w_in.shape[0]
    a = MLA_Q_RANK + MLA_KV_RANK
    pad_l, pad_r = MLA_NOPE, MLA_HEAD_PAD - MLA_QK
    w_in_r = jnp.concatenate(
        [w_in[:, :a], jnp.zeros((d, pad_l), w_in.dtype), w_in[:, a:a + MLA_ROPE],
         jnp.zeros((d, pad_r), w_in.dtype), w_in[:, a + MLA_ROPE:]], axis=1)
    wq = w_qb.reshape(MLA_Q_RANK, MLA_HEADS, MLA_QK)
    wq = jnp.pad(wq, ((0, 0), (0, 0), (0, pad_r))).reshape(MLA_Q_RANK, MLA_HEADS * MLA_HEAD_PAD)
    wkv = w_kvb.reshape(MLA_KV_RANK, MLA_HEADS, MLA_NOPE + MLA_VDIM)
    wk = jnp.pad(wkv[:, :, :MLA_NOPE], ((0, 0), (0, 0), (0, MLA_HEAD_PAD - MLA_NOPE)))
    wk = wk.reshape(MLA_KV_RANK, MLA_HEADS * MLA_HEAD_PAD)
    wv = wkv[:, :, MLA_NOPE:].reshape(MLA_KV_RANK, MLA_HEADS * MLA_VDIM)
    return tuple(w.astype(BF16) for w in (w_in_r, wq, wk, wv, w_out))


def _rope_tables(n):
    half = MLA_ROPE // 2
    quarter = half // 2
    t = np.arange(n)
    inv_freq = ROPE_BASE ** (-np.arange(0, half, 2, dtype=np.float32) / half)
    cos = np.ones((n, MLA_HEAD_PAD), np.float32)
    sin = np.zeros((n, MLA_HEAD_PAD), np.float32)
    for k, pos in enumerate((t // GRID_W, t % GRID_W)):
        ang = pos[:, None].astype(np.float32) * inv_freq[None].astype(np.float32)
        c, s = np.cos(ang), np.sin(ang)
        base = MLA_NOPE + k * half
        cos[:, base:base + quarter] = c
        cos[:, base + quarter:base + half] = c
        sin[:, base:base + quarter] = -s
        sin[:, base + quarter:base + half] = s
    return jnp.asarray(cos), jnp.asarray(sin)


def _rope(x, cos, sin):
    quarter = MLA_ROPE // 4
    lane = lax.broadcasted_iota(jnp.int32, (1, MLA_HEAD_PAD), 1)
    first = ((lane - MLA_NOPE) % (2 * quarter)) < quarter
    partner = jnp.where(first, pltpu.roll(x, MLA_HEAD_PAD - quarter, 1), pltpu.roll(x, quarter, 1))
    return x * cos + partner * sin


def _rms(x, g):
    return x * lax.rsqrt(jnp.mean(x * x, axis=-1, keepdims=True) + RMS_EPS) * g


def _mla_kernel(*refs, row_of_batch, has_ctx):
    it = iter(refs)
    (x_ref, mod_ref, lng_ref, lnb_ref, w_in_ref, qn_ref, kvn_ref, wq_ref, wk_ref, wv_ref,
     w_out_ref) = (next(it) for _ in range(11))
    if has_ctx:
        ctx_ckv_ref, ctx_kpe_ref, cos_ref, sin_ref = (next(it) for _ in range(4))
    out_ref = next(it)
    if not has_ctx:
        ckv_out_ref, kpe_out_ref = next(it), next(it)
    h_scr, q_scr, k_scr, v_scr, o_scr = it

    n = x_ref.shape[1]
    n_ctx = k_scr.shape[0] - n
    hp = MLA_HEAD_PAD
    c_ckv = MLA_Q_RANK
    c_kpe = c_ckv + MLA_KV_RANK
    c_g = c_kpe + hp

    row = pl.program_id(0) + 1 if row_of_batch else 0
    shift, scale, gate = _mod_rows(mod_ref, row)
    sm_scale = MLA_QK ** -0.5
    n_tiles = n // ROW_TILE

    def store_keys(ckv, kpe, krows):
        k_all = _mm(ckv, wk_ref[...])
        v_scr[krows, :] = _mm(ckv, wv_ref[...]).astype(BF16)
        for hd in range(MLA_HEADS):
            cols = slice(hd * hp, (hd + 1) * hp)
            k_scr[krows, cols] = (k_all[:, cols] + kpe).astype(BF16)

    def proj_body(ti, carry):
        r0 = pl.multiple_of(ti * ROW_TILE, ROW_TILE)
        rows = pl.ds(r0, ROW_TILE)
        h = (x_ref[0, rows, :] * (1.0 + scale) + shift).astype(BF16)
        h_scr[rows, :] = h
        cq = _rms(_mm(h, w_in_ref[:, :c_ckv]), qn_ref[...])
        ckv = _rms(_mm(h, w_in_ref[:, c_ckv:c_kpe]), kvn_ref[...])
        kpe = _mm(h, w_in_ref[:, c_kpe:c_g])
        if has_ctx:
            cos, sin = cos_ref[rows, :], sin_ref[rows, :]
            kpe = _rope(kpe, cos, sin)
        else:
            ckv_out_ref[0, 0, rows, :] = ckv
            kpe_out_ref[0, 0, rows, :] = kpe[:, MLA_NOPE:MLA_QK]
        q_all = _mm(cq, wq_ref[...])
        for hd in range(MLA_HEADS):
            cols = slice(hd * hp, (hd + 1) * hp)
            qh = q_all[:, cols]
            if has_ctx:
                qh = _rope(qh, cos, sin)
            q_scr[rows, cols] = (qh * sm_scale).astype(BF16)
        store_keys(ckv, kpe, pl.ds(pl.multiple_of(n_ctx + r0, ROW_TILE), ROW_TILE))
        return carry

    def ctx_body(ti, carry):
        rows = pl.ds(pl.multiple_of(ti * ROW_TILE, ROW_TILE), ROW_TILE)
        store_keys(ctx_ckv_ref[0, 0, rows, :], ctx_kpe_ref[0, 0, rows, :], rows)
        return carry

    if n_tiles == 1:
        proj_body(0, 0)
    else:
        lax.fori_loop(0, n_tiles, proj_body, 0)
    if has_ctx:
        lax.fori_loop(0, n_ctx // ROW_TILE, ctx_body, 0)

    def tile_body(ti, carry):
        rows = pl.ds(pl.multiple_of(ti * ROW_TILE, ROW_TILE), ROW_TILE)
        for hd2 in range(MLA_HEADS // 2):
            o_pair = []
            for hd in (2 * hd2, 2 * hd2 + 1):
                cols = slice(hd * hp, (hd + 1) * hp)
                s = _mm_nt(q_scr[rows, cols], k_scr[:, cols])
                e = jnp.exp(s - jnp.max(s, axis=-1, keepdims=True))
                denom = jnp.sum(e, axis=-1, keepdims=True)
                o_pair.append(_mm(e, v_scr[:, hd * MLA_VDIM:(hd + 1) * MLA_VDIM]) / denom)
            o_scr[:, hd2 * 2 * MLA_VDIM:(hd2 + 1) * 2 * MLA_VDIM] = jnp.concatenate(o_pair, axis=-1)
        g = _mm(h_scr[rows, :], w_in_ref[:, c_g:])
        y = _mm(o_scr[...] * _silu(g), w_out_ref[...])
        out_ref[0, rows, :] = _residual_ln(x_ref[0, rows, :], y, gate, lng_ref[0], lnb_ref[0])
        return carry

    if n_tiles == 1:
        tile_body(0, 0)
    else:
        lax.fori_loop(0, n_tiles, tile_body, 0)


def _mla_layer(x, mod, layer, ln_g, ln_b, weights, q_norm, kv_norm, ctx_ckv, ctx_kpe, *, row_of_batch):
    bsz, n, d = x.shape
    w_in, wq, wk, wv, w_out = weights
    has_ctx = ctx_ckv is not None
    in_specs = [
        _seq_spec(n, d),
        pl.BlockSpec((1, 3, SUBLANES, d), lambda b: (layer, 0, 0, 0)),
        pl.BlockSpec((1, 1, d), lambda b: (layer, 0, 0)),
        pl.BlockSpec((1, 1, d), lambda b: (layer, 0, 0)),
        _const_spec(w_in.shape),
        _const_spec(q_norm.shape),
        _const_spec(kv_norm.shape),
        _const_spec(wq.shape),
        _const_spec(wk.shape),
        _const_spec(wv.shape),
        _const_spec(w_out.shape),
    ]
    args = [x, mod, ln_g, ln_b, w_in, q_norm, kv_norm, wq, wk, wv, w_out]
    out_specs = [_seq_spec(n, d)]
    out_shape = [jax.ShapeDtypeStruct((bsz, n, d), F32)]
    n_ctx = 0
    if has_ctx:
        n_ctx = ctx_ckv.shape[2]
        ctx_kpe = jnp.pad(ctx_kpe, ((0, 0), (0, 0), (0, 0), (MLA_NOPE, MLA_HEAD_PAD - MLA_QK)))
        cos, sin = _rope_tables(n)
        in_specs += [
            pl.BlockSpec((1, 1, n_ctx, MLA_KV_RANK), lambda b: (b, 0, 0, 0)),
            pl.BlockSpec((1, 1, n_ctx, MLA_HEAD_PAD), lambda b: (b, 0, 0, 0)),
            _const_spec(cos.shape),
            _const_spec(sin.shape),
        ]
        args += [ctx_ckv, ctx_kpe, cos, sin]
    else:
        out_specs += [pl.BlockSpec((1, 1, n, MLA_KV_RANK), lambda b: (b, 0, 0, 0)),
                      pl.BlockSpec((1, 1, n, MLA_ROPE), lambda b: (b, 0, 0, 0))]
        out_shape += [jax.ShapeDtypeStruct((bsz, 1, n, MLA_KV_RANK), F32),
                      jax.ShapeDtypeStruct((bsz, 1, n, MLA_ROPE), F32)]
    kernel = functools.partial(_mla_kernel, row_of_batch=row_of_batch, has_ctx=has_ctx)
    res = pl.pallas_call(
        kernel,
        grid=(bsz,),
        in_specs=in_specs,
        out_specs=out_specs,
        out_shape=out_shape,
        scratch_shapes=[
            pltpu.VMEM((n, d), BF16),
            pltpu.VMEM((n, MLA_HEADS * MLA_HEAD_PAD), BF16),
            pltpu.VMEM((n_ctx + n, MLA_HEADS * MLA_HEAD_PAD), BF16),
            pltpu.VMEM((n_ctx + n, MLA_HEADS * MLA_VDIM), BF16),
            pltpu.VMEM((ROW_TILE, MLA_HEADS * MLA_VDIM), F32),
        ],
        compiler_params=_compiler_params(1),
        name="mla_layer",
    )(*args)
    return res[0] if has_ctx else tuple(res)


def kernel(x_prompt, x_sample, state_hgrn, state_rglru, cache_mla_ckv, cache_mla_kpe, c, c_ctx, ada_w, ada_b, ln_g, ln_b, hg_w_in, hg_lb_logits, hg_norm_g, hg_w_out, sc_w_in, sc_conv_w, sc_conv_b, sc_w_out, rg_w_in, rg_conv_w, rg_conv_b, rg_w_gate, rg_b_gate, rg_lambda, rg_w_out, mla_w_in, mla_q_norm, mla_kv_norm, mla_w_qb, mla_w_kvb, mla_w_out):
    d = x_prompt.shape[-1]
    n_dec = c.shape[0]
    cond = jnp.zeros((SUBLANES, d), F32).at[0].set(c_ctx).at[1:1 + n_dec].set(c)
    mod = _modulation(cond, ada_w, ada_b)
    lng = ln_g.reshape(DEPTH, 1, d)
    lnb = ln_b.reshape(DEPTH, 1, d)
    hg_wi, hg_wo = hg_w_in[0].astype(BF16), hg_w_out[0].astype(BF16)
    sc_wi, sc_wo = sc_w_in[0].astype(BF16), sc_w_out[0].astype(BF16)
    rg_wi, rg_wg, rg_wo = rg_w_in[0].astype(BF16), rg_w_gate[0].astype(BF16), rg_w_out[0].astype(BF16)
    mla_w = _mla_weights(mla_w_in[0], mla_w_qb[0], mla_w_kvb[0], mla_w_out[0])

    def run(x, rob, st_hg, st_rg, ctx_ckv, ctx_kpe):
        r0 = _hgrn_layer(x, mod, 0, lng, lnb, hg_wi, hg_lb_logits, hg_norm_g, hg_wo, st_hg, row_of_batch=rob)
        x, new_hg = r0 if st_hg is None else (r0, None)
        x = _sconv_layer(x, mod, 1, lng, lnb, sc_wi, sc_conv_w[0], sc_conv_b, sc_wo, row_of_batch=rob)
        r2 = _rglru_layer(x, mod, 2, lng, lnb, rg_wi, rg_conv_w[0], rg_conv_b, rg_wg, rg_b_gate[0],
                          rg_lambda[0], rg_wo, st_rg, row_of_batch=rob)
        x, new_rg = r2 if st_rg is None else (r2, None)
        r3 = _mla_layer(x, mod, 3, lng, lnb, mla_w, mla_q_norm, mla_kv_norm, ctx_ckv, ctx_kpe,
                        row_of_batch=rob)
        if ctx_ckv is None:
            return (r3[0], new_hg, new_rg, r3[1], r3[2])
        return (r3,)

    y_prompt, new_hg, new_rg, new_ckv, new_kpe = run(x_prompt, False, None, None, None, None)
    (y_sample,) = run(x_sample, True, state_hgrn, state_rglru, cache_mla_ckv, cache_mla_kpe)
    return (y_prompt, y_sample, new_hg, new_rg, new_ckv, new_kpe)
```

```python
import functools

import jax
import jax.numpy as jnp
import numpy as np
from jax import lax
from jax.experimental import pallas as pl
from jax.experimental.pallas import tpu as pltpu

F32 = jnp.float32
BF16 = jnp.bfloat16

LANES = 128
SUBLANES = 8
VMEM_LIMIT_BYTES = 60 * 1024 * 1024

DEPTH = 4
DEEPNORM_ALPHA = (2 * DEPTH) ** 0.25
LN_EPS = 1e-5
RMS_EPS = 1e-6

ROW_TILE = 256

HG_HEADS = 8
HG_KDIM = 128
HG_VDIM = 128
HG_CHUNK = 32
SC_KERNEL = 3
RG_HEADS = 4
RG_KERNEL = 4
RG_C = 8.0
MLA_HEADS = 16
MLA_Q_RANK = 384
MLA_KV_RANK = 256
MLA_NOPE = 64
MLA_ROPE = 32
MLA_VDIM = 64
MLA_HEAD_PAD = 128
MLA_HEAD_GROUP = 4
ROPE_BASE = 10000.0
GRID_W = 64


def _mm(a, b):
    return jnp.dot(a.astype(BF16), b.astype(BF16), preferred_element_type=F32)


def _mm_nt(a, b):
    return lax.dot_general(a.astype(BF16), b.astype(BF16), (((1,), (1,)), ((), ())),
                           preferred_element_type=F32)


def _mm_tn(a, b):
    return lax.dot_general(a.astype(BF16), b.astype(BF16), (((0,), (0,)), ((), ())),
                           preferred_element_type=F32)


def _silu(x):
    return x * jax.nn.sigmoid(x)


def _mod_rows(mod_ref, row):
    if isinstance(row, int):
        return tuple(mod_ref[0, j, row:row + 1, :] for j in range(3))
    rows = lax.broadcasted_iota(jnp.int32, (SUBLANES, 1), 0)
    return tuple(jnp.sum(jnp.where(rows == row, mod_ref[0, j], 0.0), axis=0, keepdims=True)
                 for j in range(3))


def _residual_ln(x, y, gate, g, b):
    z = DEEPNORM_ALPHA * x + gate * y
    mu = jnp.mean(z, axis=-1, keepdims=True)
    zc = z - mu
    var = jnp.mean(zc * zc, axis=-1, keepdims=True)
    return zc * lax.rsqrt(var + LN_EPS) * g + b


def _const_spec(shape):
    nd = len(shape)
    return pl.BlockSpec(shape, lambda *_: (0,) * nd, pipeline_mode=pl.Buffered(1))


def _seq_spec(n, d):
    mode = dict(pipeline_mode=pl.Buffered(1)) if n > ROW_TILE else {}
    return pl.BlockSpec((1, n, d), lambda b: (b, 0, 0), **mode)


def _compiler_params(n_grid):
    return pltpu.CompilerParams(dimension_semantics=("arbitrary",) * n_grid,
                                vmem_limit_bytes=VMEM_LIMIT_BYTES)


def _mod_kernel(cond_ref, w_ref, b_ref, out_ref):
    s = _silu(cond_ref[...])
    out_ref[0, 0] = _mm(s, w_ref[0]) + b_ref[0, 0]


def _modulation(cond, ada_w, ada_b):
    n_layers, d, _ = ada_w.shape
    rows = cond.shape[0]
    return pl.pallas_call(
        _mod_kernel,
        grid=(n_layers, 3),
        in_specs=[
            pl.BlockSpec((rows, d), lambda l, j: (0, 0)),
            pl.BlockSpec((1, d, d), lambda l, j: (l, 0, j)),
            pl.BlockSpec((1, 1, 1, d), lambda l, j: (l, j, 0, 0)),
        ],
        out_specs=pl.BlockSpec((1, 1, rows, d), lambda l, j: (l, j, 0, 0)),
        out_shape=jax.ShapeDtypeStruct((n_layers, 3, rows, d), F32),
        compiler_params=_compiler_params(2),
        name="adaln_modulation",
    )(cond, ada_w, ada_b.reshape(n_layers, 3, 1, d))


def _sconv_kernel(x_ref, mod_ref, lng_ref, lnb_ref, w_in_ref, cw_ref, cb_ref, w_out_ref,
                  out_ref, u_scr, *, row_of_batch):
    n, d = x_ref.shape[1], x_ref.shape[2]
    width = w_out_ref.shape[0]
    row = pl.program_id(0) + 1 if row_of_batch else 0
    shift, scale, gate = _mod_rows(mod_ref, row)
    x = x_ref[0]
    h = (x * (1.0 + scale) + shift).astype(BF16)
    t = lax.broadcasted_iota(jnp.int32, (n, 1), 0)
    chunk = 2 * LANES
    def project(j):
        return tuple(_mm(h, w_in_ref[:, i * width + j * chunk:i * width + (j + 1) * chunk]) for i in range(4))

    nxt = project(0)
    for j in range(width // chunk):
        c0 = j * chunk
        bg, cg, v, g = nxt
        if j + 1 < width // chunk:
            nxt = project(j + 1)
        p = cg * v
        p_prev = jnp.where(t >= 1, pltpu.roll(p, 1, 0), 0.0)
        p_next = jnp.where(t < n - 1, pltpu.roll(p, n - 1, 0), 0.0)
        z = (cw_ref[0:1, c0:c0 + chunk] * p_prev + cw_ref[1:2, c0:c0 + chunk] * p
             + cw_ref[2:3, c0:c0 + chunk] * p_next + cb_ref[:, c0:c0 + chunk])
        u_scr[:, c0:c0 + chunk] = (_silu(g) * bg * z).astype(BF16)
    y = _mm(u_scr[...], w_out_ref[...])
    out_ref[0] = _residual_ln(x, y, gate, lng_ref[0], lnb_ref[0])


def _sconv_layer(x, mod, layer, ln_g, ln_b, w_in, conv_w, conv_b, w_out, *, row_of_batch):
    bsz, n, d = x.shape
    width = w_out.shape[0]
    kernel = functools.partial(_sconv_kernel, row_of_batch=row_of_batch)
    return pl.pallas_call(
        kernel,
        grid=(bsz,),
        in_specs=[
            _seq_spec(n, d),
            pl.BlockSpec((1, 3, SUBLANES, d), lambda b: (layer, 0, 0, 0)),
            pl.BlockSpec((1, 1, d), lambda b: (layer, 0, 0)),
            pl.BlockSpec((1, 1, d), lambda b: (layer, 0, 0)),
            _const_spec(w_in.shape),
            _const_spec(conv_w.shape),
            _const_spec(conv_b.shape),
            _const_spec(w_out.shape),
        ],
        out_specs=_seq_spec(n, d),
        out_shape=jax.ShapeDtypeStruct((bsz, n, d), F32),
        scratch_shapes=[pltpu.VMEM((n, width), BF16)],
        compiler_params=_compiler_params(1),
        name="sconv_layer",
    )(x, mod, ln_g, ln_b, w_in, conv_w, conv_b, w_out)


def _chunk_cumsum(x, reverse):
    n = x.shape[0]
    pos = lax.broadcasted_iota(jnp.int32, (n, 1), 0) % HG_CHUNK
    s = 1
    while s < HG_CHUNK:
        if reverse:
            x = x + jnp.where(pos < HG_CHUNK - s, pltpu.roll(x, n - s, 0), 0.0)
        else:
            x = x + jnp.where(pos >= s, pltpu.roll(x, s, 0), 0.0)
        s *= 2
    return x


def _hgrn_kernel(*refs, row_of_batch, layer, has_s0, emit_state):
    it = iter(refs)
    x_ref, mod_ref, lng_ref, lnb_ref, w_in_ref, lbl_ref, ng_ref, w_out_ref = (next(it) for _ in range(8))
    s0_ref = next(it) if has_s0 else None
    out_ref = next(it)
    s_out_ref = next(it) if emit_state else None
    h_scr, qrel_scr, krel_scr, qp_scr, kp_scr, v_scr, dec_scr, o_scr, g_scr, u_scr = it

    n, d = x_ref.shape[1], x_ref.shape[2]
    hk = HG_HEADS * HG_KDIM
    n_tiles = n // ROW_TILE
    chunks_per_tile = ROW_TILE // HG_CHUNK
    n_chunks = n // HG_CHUNK
    pair = 2 * LANES

    row = pl.program_id(0) + 1 if row_of_batch else 0
    shift, scale, gate = _mod_rows(mod_ref, row)
    x = x_ref[0]
    h_scr[...] = (x * (1.0 + scale) + shift).astype(BF16)

    lbs = []
    for dr in range(2):
        z = lbl_ref[dr]
        e = jnp.exp(z - jnp.max(z, axis=0, keepdims=True))
        lbs.append(jnp.sum(e[:layer + 1], axis=0, keepdims=True) / jnp.sum(e, axis=0, keepdims=True))

    ri = lax.broadcasted_iota(jnp.int32, (ROW_TILE, ROW_TILE), 0)
    ci = lax.broadcasted_iota(jnp.int32, (ROW_TILE, ROW_TILE), 1)
    same_chunk = (ri // HG_CHUNK) == (ci // HG_CHUNK)
    masks = (same_chunk & (ci <= ri), same_chunk & (ci >= ri))

    row_chunk = (lax.broadcasted_iota(jnp.int32, (ROW_TILE, LANES), 0) // HG_CHUNK).astype(BF16)

    def chunk_blocks(a):
        zero = jnp.zeros_like(a)
        return jnp.concatenate([jnp.where(row_chunk == c, a, zero) for c in range(chunks_per_tile)], axis=1)

    def project(hp):
        c0 = hp * pair
        slot = hp % 2
        h = h_scr[...]
        q = _silu(_mm(h, w_in_ref[:, c0:c0 + pair])) * (HG_KDIM ** -0.5)
        v_scr[slot] = _mm(h, w_in_ref[:, 3 * hk + c0:3 * hk + c0 + pair]).astype(BF16)
        q3 = q.reshape(n_chunks, HG_CHUNK, pair)
        for dr in range(2):
            fl = _mm(h, w_in_ref[:, (1 + dr) * hk + c0:(1 + dr) * hk + c0 + pair])
            lb = lbs[dr][:, c0:c0 + pair]
            f = lb + (1.0 - lb) * jax.nn.sigmoid(fl)
            k3 = (1.0 - f).reshape(n_chunks, HG_CHUNK, pair)
            cum = _chunk_cumsum(jnp.log(f), reverse=(dr == 1)).reshape(n_chunks, HG_CHUNK, pair)
            mid = HG_CHUNK // 2 - 1
            if dr == 0:
                ref, tot = cum[:, mid:mid + 1], cum[:, HG_CHUNK - 1:HG_CHUNK]
            else:
                ref, tot = cum[:, HG_CHUNK - 1 - mid:HG_CHUNK - mid], cum[:, 0:1]
            q_rel = q3 * jnp.exp(cum - ref)
            k_rel = k3 * jnp.exp(ref - cum)
            qrel_scr[slot, dr] = q_rel.reshape(n, pair).astype(BF16)
            krel_scr[slot, dr] = k_rel.reshape(n, pair).astype(BF16)
            qp_scr[slot, dr] = (q_rel * jnp.exp(ref)).reshape(n, pair).astype(BF16)
            kp_scr[slot, dr] = (k_rel * jnp.exp(tot - ref)).reshape(n, pair).astype(BF16)
            dec_scr[slot, dr] = jnp.exp(tot)
        g_scr[slot] = _silu(_mm(h, w_in_ref[:, 4 * hk + c0:4 * hk + c0 + pair]))

    def recur(hp):
        c0 = hp * pair
        slot = hp % 2
        for dr in range(2):
            for h2 in range(2):
                l0 = h2 * LANES
                head = hp * 2 + h2
                if has_s0:
                    s_t0 = s0_ref[0, 0, dr, head].T
                else:
                    s_t0 = jnp.zeros((HG_VDIM, HG_KDIM), F32)

                def tile_body(i, s_t, dr=dr, l0=l0, slot=slot):
                    ti = i if dr == 0 else n_tiles - 1 - i
                    rows = pl.ds(pl.multiple_of(ti * ROW_TILE, ROW_TILE), ROW_TILE)
                    lanes = slice(l0, l0 + LANES)
                    v = v_scr[slot, rows, lanes]
                    att = _mm_nt(qrel_scr[slot, dr, rows, lanes], krel_scr[slot, dr, rows, lanes])
                    o_tile = _mm(jnp.where(masks[dr], att, 0.0), v)
                    upd = _mm_tn(v, chunk_blocks(kp_scr[slot, dr, rows, lanes]))
                    s_prev = [None] * chunks_per_tile
                    order = range(chunks_per_tile) if dr == 0 else range(chunks_per_tile - 1, -1, -1)
                    for c in order:
                        s_prev[c] = s_t.astype(BF16)
                        dec = dec_scr[slot, dr, ti * chunks_per_tile + c][:, lanes]
                        s_t = s_t * dec + upd[:, c * LANES:(c + 1) * LANES]
                    o_tile += _mm_nt(chunk_blocks(qp_scr[slot, dr, rows, lanes]),
                                     jnp.concatenate(s_prev, axis=1))
                    if dr == 0:
                        o_scr[slot, rows, lanes] = o_tile
                    else:
                        o_scr[slot, rows, lanes] += o_tile
                    return s_t

                if n_tiles == 1:
                    s_t = tile_body(0, s_t0)
                else:
                    s_t = lax.fori_loop(0, n_tiles, tile_body, s_t0)
                if emit_state:
                    s_out_ref[0, 0, dr, head] = s_t.T

        for h2 in range(2):
            lanes = slice(h2 * LANES, (h2 + 1) * LANES)
            o = o_scr[slot, :, lanes]
            ms = jnp.mean(o * o, axis=-1, keepdims=True)
            on = o * lax.rsqrt(ms + RMS_EPS) * ng_ref[:, c0 + h2 * LANES:c0 + (h2 + 1) * LANES]
            u_scr[:, c0 + h2 * LANES:c0 + (h2 + 1) * LANES] = (on * g_scr[slot, :, lanes]).astype(BF16)

    n_pairs = hk // pair
    project(0)
    for hp in range(n_pairs):
        if hp + 1 < n_pairs:
            project(hp + 1)
        recur(hp)

    y = _mm(u_scr[...], w_out_ref[...])
    out_ref[0] = _residual_ln(x, y, gate, lng_ref[0], lnb_ref[0])


def _hgrn_layer(x, mod, layer, ln_g, ln_b, w_in, lb_logits, norm_g, w_out, state, *, row_of_batch):
    bsz, n, d = x.shape
    hv = w_out.shape[0]
    has_s0 = state is not None
    emit_state = not has_s0
    pair = 2 * LANES
    state_block = (1, 1, 2, HG_HEADS, HG_KDIM, HG_VDIM)
    state_spec = pl.BlockSpec(state_block, lambda b: (b, 0, 0, 0, 0, 0))
    in_specs = [
        _seq_spec(n, d),
        pl.BlockSpec((1, 3, SUBLANES, d), lambda b: (layer, 0, 0, 0)),
        pl.BlockSpec((1, 1, d), lambda b: (layer, 0, 0)),
        pl.BlockSpec((1, 1, d), lambda b: (layer, 0, 0)),
        _const_spec(w_in.shape),
        _const_spec(lb_logits.shape),
        _const_spec(norm_g.shape),
        _const_spec(w_out.shape),
    ]
    args = [x, mod, ln_g, ln_b, w_in, lb_logits, norm_g, w_out]
    out_specs = [_seq_spec(n, d)]
    out_shape = [jax.ShapeDtypeStruct((bsz, n, d), F32)]
    if has_s0:
        in_specs.append(state_spec)
        args.append(state)
    if emit_state:
        out_specs.append(state_spec)
        out_shape.append(jax.ShapeDtypeStruct((bsz,) + state_block[1:], F32))
    kernel = functools.partial(_hgrn_kernel, row_of_batch=row_of_batch, layer=layer,
                               has_s0=has_s0, emit_state=emit_state)
    res = pl.pallas_call(
        kernel,
        grid=(bsz,),
        in_specs=in_specs,
        out_specs=out_specs,
        out_shape=out_shape,
        scratch_shapes=[
            pltpu.VMEM((n, d), BF16),
            pltpu.VMEM((2, 2, n, pair), BF16),
            pltpu.VMEM((2, 2, n, pair), BF16),
            pltpu.VMEM((2, 2, n, pair), BF16),
            pltpu.VMEM((2, 2, n, pair), BF16),
            pltpu.VMEM((2, n, pair), BF16),
            pltpu.VMEM((2, 2, n // HG_CHUNK, 1, pair), F32),
            pltpu.VMEM((2, n, pair), F32),
            pltpu.VMEM((2, n, pair), F32),
            pltpu.VMEM((n, hv), BF16),
        ],
        compiler_params=_compiler_params(1),
        name="hgrn2_layer",
    )(*args)
    return res[0] if has_s0 else tuple(res)


def _block_scan(a, x, reverse):
    n, c = a.shape
    nb = n // SUBLANES
    a = a.reshape(nb, SUBLANES, c)
    x = x.reshape(nb, SUBLANES, c)
    pos = lax.broadcasted_iota(jnp.int32, (1, SUBLANES, 1), 1)
    s = 1
    while s < SUBLANES:
        if reverse:
            keep = pos < SUBLANES - s
            shift = SUBLANES - s
        else:
            keep = pos >= s
            shift = s
        a_in = jnp.where(keep, pltpu.roll(a, shift, 1), 1.0)
        x_in = jnp.where(keep, pltpu.roll(x, shift, 1), 0.0)
        x = a * x_in + x
        a = a * a_in
        s *= 2
    return a.reshape(n, c), x.reshape(n, c)


def _rglru_kernel(*refs, row_of_batch, has_s0, emit_state):
    it = iter(refs)
    (x_ref, mod_ref, lng_ref, lnb_ref, w_in_ref, cw_ref, cb_ref, wg_ref, bg_ref, lam_ref,
     w_out_ref) = (next(it) for _ in range(11))
    s0_ref = next(it) if has_s0 else None
    out_ref = next(it)
    s_out_ref = next(it) if emit_state else None
    h_scr, a_scr, b_scr, gs_scr = it

    n, d = x_ref.shape[1], x_ref.shape[2]
    width = w_out_ref.shape[0]
    blk = width // RG_HEADS
    nb = n // SUBLANES

    row = pl.program_id(0) + 1 if row_of_batch else 0
    shift, scale, gate = _mod_rows(mod_ref, row)
    h_scr[...] = (x_ref[0] * (1.0 + scale) + shift).astype(BF16)

    z = -lam_ref[...]
    coef = -RG_C * (jnp.maximum(z, 0.0) + jnp.log1p(jnp.exp(-jnp.abs(z))))

    t = lax.broadcasted_iota(jnp.int32, (n, 1), 0)

    def project(hh):
        h = h_scr[...]
        return (_mm(h, w_in_ref[:, hh * blk:(hh + 1) * blk]),
                _mm(h, w_in_ref[:, width + hh * blk:width + (hh + 1) * blk]))

    def mix(hh, u_pre, g):
        cols = slice(hh * blk, (hh + 1) * blk)
        u = (cw_ref[0:1, cols] * jnp.where(t >= 2, pltpu.roll(u_pre, 2, 0), 0.0)
             + cw_ref[1:2, cols] * jnp.where(t >= 1, pltpu.roll(u_pre, 1, 0), 0.0)
             + cw_ref[2:3, cols] * u_pre
             + cw_ref[3:4, cols] * jnp.where(t < n - 1, pltpu.roll(u_pre, n - 1, 0), 0.0)
             + cb_ref[:, cols])
        gs_scr[:, cols] = _silu(g)
        gates = [jax.nn.sigmoid(_mm(u, wg_ref[dr, hh]) + bg_ref[dr, hh]) for dr in range(2)]
        for dr in range(2):
            a = jnp.exp(coef[dr:dr + 1, cols] * gates[dr][:, :blk])
            xin = jnp.exp(0.5 * jnp.log(1.0 - a * a)) * (gates[dr][:, blk:] * u)
            a_blk, h_blk = _block_scan(a, xin, reverse=(dr == 1))
            a_scr[dr, :, cols] = a_blk
            b_scr[dr, :, cols] = h_blk

    nxt = project(0)
    for hh in range(RG_HEADS):
        cur = nxt
        if hh + 1 < RG_HEADS:
            nxt = project(hh + 1)
        mix(hh, *cur)

    if has_s0:
        carry0 = (s0_ref[0, 0, 0:1, :], s0_ref[0, 0, 1:2, :])
    else:
        carry0 = (jnp.zeros((1, width), F32), jnp.zeros((1, width), F32))

    def carry_body(i, carry):
        cf, cb = carry
        rf = pl.ds(pl.multiple_of(i * SUBLANES, SUBLANES), SUBLANES)
        rb = pl.ds(pl.multiple_of((nb - 1 - i) * SUBLANES, SUBLANES), SUBLANES)
        hf = a_scr[0, rf, :] * cf + b_scr[0, rf, :]
        hb = a_scr[1, rb, :] * cb + b_scr[1, rb, :]
        b_scr[0, rf, :] = hf
        b_scr[1, rb, :] = hb
        return hf[SUBLANES - 1:SUBLANES, :], hb[0:1, :]

    cf, cb = lax.fori_loop(0, nb, carry_body, carry0)
    if emit_state:
        s_out_ref[0, 0, 0:1, :] = cf
        s_out_ref[0, 0, 1:2, :] = cb

    for r0 in range(0, n, ROW_TILE):
        rows = slice(r0, r0 + ROW_TILE)
        mix = ((b_scr[0, rows, :] + b_scr[1, rows, :]) * gs_scr[rows, :]).astype(BF16)
        y = _mm(mix, w_out_ref[...])
        out_ref[0, rows, :] = _residual_ln(x_ref[0, rows, :], y, gate, lng_ref[0], lnb_ref[0])


def _rglru_layer(x, mod, layer, ln_g, ln_b, w_in, conv_w, conv_b, w_gate, b_gate, lam, w_out, state,
                 *, row_of_batch):
    bsz, n, d = x.shape
    width = w_out.shape[0]
    has_s0 = state is not None
    emit_state = not has_s0
    b_gate = b_gate.reshape(2, RG_HEADS, 1, b_gate.shape[-1])
    state_block = (1, 1, 2, width)
    state_spec = pl.BlockSpec(state_block, lambda b: (b, 0, 0, 0))
    in_specs = [
        _seq_spec(n, d),
        pl.BlockSpec((1, 3, SUBLANES, d), lambda b: (layer, 0, 0, 0)),
        pl.BlockSpec((1, 1, d), lambda b: (layer, 0, 0)),
        pl.BlockSpec((1, 1, d), lambda b: (layer, 0, 0)),
        _const_spec(w_in.shape),
        _const_spec(conv_w.shape),
        _const_spec(conv_b.shape),
        _const_spec(w_gate.shape),
        _const_spec(b_gate.shape),
        _const_spec(lam.shape),
        _const_spec(w_out.shape),
    ]
    args = [x, mod, ln_g, ln_b, w_in, conv_w, conv_b, w_gate, b_gate, lam, w_out]
    out_specs = [_seq_spec(n, d)]
    out_shape = [jax.ShapeDtypeStruct((bsz, n, d), F32)]
    if has_s0:
        in_specs.append(state_spec)
        args.append(state)
    if emit_state:
        out_specs.append(state_spec)
        out_shape.append(jax.ShapeDtypeStruct((bsz,) + state_block[1:], F32))
    kernel = functools.partial(_rglru_kernel, row_of_batch=row_of_batch, has_s0=has_s0,
                               emit_state=emit_state)
    res = pl.pallas_call(
        kernel,
        grid=(bsz,),
        in_specs=in_specs,
        out_specs=out_specs,
        out_shape=out_shape,
        scratch_shapes=[
            pltpu.VMEM((n, d), BF16),
            pltpu.VMEM((2, n, width), F32),
            pltpu.VMEM((2, n, width), F32),
            pltpu.VMEM((n, width), F32),
        ],
        compiler_params=_compiler_params(1),
        name="rglru_layer",
    )(*args)
    return res[0] if has_s0 else tuple(res)


MLA_QK = MLA_NOPE + MLA_ROPE


def _mla_weights(w_in, w_qb, w_kvb, w_out):
    d = w_in.shape[0]
    a = MLA_Q_RANK + MLA_KV_RANK
    pad_l, pad_r = MLA_NOPE, MLA_HEAD_PAD - MLA_QK
    w_in_r = jnp.concatenate(
        [w_in[:, :a], jnp.zeros((d, pad_l), w_in.dtype), w_in[:, a:a + MLA_ROPE],
         jnp.zeros((d, pad_r), w_in.dtype), w_in[:, a + MLA_ROPE:]], axis=1)
    wq = w_qb.reshape(MLA_Q_RANK, MLA_HEADS, MLA_QK)
    wq = jnp.pad(wq, ((0, 0), (0, 0), (0, pad_r))).reshape(MLA_Q_RANK, MLA_HEADS * MLA_HEAD_PAD)
    wkv = w_kvb.reshape(MLA_KV_RANK, MLA_HEADS, MLA_NOPE + MLA_VDIM)
    wk = jnp.pad(wkv[:, :, :MLA_NOPE], ((0, 0), (0, 0), (0, MLA_HEAD_PAD - MLA_NOPE)))
    wk = wk.reshape(MLA_KV_RANK, MLA_HEADS * MLA_HEAD_PAD)
    wv = wkv[:, :, MLA_NOPE:].reshape(MLA_KV_RANK, MLA_HEADS * MLA_VDIM)
    return tuple(w.astype(BF16) for w in (w_in_r, wq, wk, wv, w_out))


def _rope_tables(n):
    half = MLA_ROPE // 2
    quarter = half // 2
    t = np.arange(n)
    inv_freq = ROPE_BASE ** (-np.arange(0, half, 2, dtype=np.float32) / half)
    cos = np.ones((n, MLA_HEAD_PAD), np.float32)
    sin = np.zeros((n, MLA_HEAD_PAD), np.float32)
    for k, pos in enumerate((t // GRID_W, t % GRID_W)):
        ang = pos[:, None].astype(np.float32) * inv_freq[None].astype(np.float32)
        c, s = np.cos(ang), np.sin(ang)
        base = MLA_NOPE + k * half
        cos[:, base:base + quarter] = c
        cos[:, base + quarter:base + half] = c
        sin[:, base:base + quarter] = -s
        sin[:, base + quarter:base + half] = s
    return jnp.asarray(cos), jnp.asarray(sin)


def _rope(x, cos, sin):
    quarter = MLA_ROPE // 4
    lane = lax.broadcasted_iota(jnp.int32, (1, MLA_HEAD_PAD), 1)
    first = ((lane - MLA_NOPE) % (2 * quarter)) < quarter
    partner = jnp.where(first, pltpu.roll(x, MLA_HEAD_PAD - quarter, 1), pltpu.roll(x, quarter, 1))
    return x * cos + partner * sin


def _rms(x, g):
    return x * lax.rsqrt(jnp.mean(x * x, axis=-1, keepdims=True) + RMS_EPS) * g


def _mla_kernel(*refs, row_of_batch, has_ctx):
    it = iter(refs)
    (x_ref, mod_ref, lng_ref, lnb_ref, w_in_ref, qn_ref, kvn_ref, wq_ref, wk_ref, wv_ref,
     w_out_ref) = (next(it) for _ in range(11))
    if has_ctx:
        ctx_ckv_ref, ctx_kpe_ref, cos_ref, sin_ref = (next(it) for _ in range(4))
    out_ref = next(it)
    if not has_ctx:
        ckv_out_ref, kpe_out_ref = next(it), next(it)
    h_scr, q_scr, k_scr, v_scr, o_scr = it

    n = x_ref.shape[1]
    n_ctx = k_scr.shape[0] - n
    hp = MLA_HEAD_PAD
    c_ckv = MLA_Q_RANK
    c_kpe = c_ckv + MLA_KV_RANK
    c_g = c_kpe + hp

    row = pl.program_id(0) + 1 if row_of_batch else 0
    shift, scale, gate = _mod_rows(mod_ref, row)
    sm_scale = MLA_QK ** -0.5
    n_tiles = n // ROW_TILE

    def store_keys(ckv, kpe, krows):
        k_all = _mm(ckv, wk_ref[...])
        v_scr[krows, :] = _mm(ckv, wv_ref[...]).astype(BF16)
        for hd in range(MLA_HEADS):
            cols = slice(hd * hp, (hd + 1) * hp)
            k_scr[krows, cols] = (k_all[:, cols] + kpe).astype(BF16)

    def proj_body(ti, carry):
        r0 = pl.multiple_of(ti * ROW_TILE, ROW_TILE)
        rows = pl.ds(r0, ROW_TILE)
        h = (x_ref[0, rows, :] * (1.0 + scale) + shift).astype(BF16)
        h_scr[rows, :] = h
        cq = _rms(_mm(h, w_in_ref[:, :c_ckv]), qn_ref[...])
        ckv = _rms(_mm(h, w_in_ref[:, c_ckv:c_kpe]), kvn_ref[...])
        kpe = _mm(h, w_in_ref[:, c_kpe:c_g])
        if has_ctx:
            cos, sin = cos_ref[rows, :], sin_ref[rows, :]
            kpe = _rope(kpe, cos, sin)
        else:
            ckv_out_ref[0, 0, rows, :] = ckv
            kpe_out_ref[0, 0, rows, :] = kpe[:, MLA_NOPE:MLA_QK]
        q_all = _mm(cq, wq_ref[...])
        for hd in range(MLA_HEADS):
            cols = slice(hd * hp, (hd + 1) * hp)
            qh = q_all[:, cols]
            if has_ctx:
                qh = _rope(qh, cos, sin)
            q_scr[rows, cols] = (qh * sm_scale).astype(BF16)
        store_keys(ckv, kpe, pl.ds(pl.multiple_of(n_ctx + r0, ROW_TILE), ROW_TILE))
        return carry

    def ctx_body(ti, carry):
        rows = pl.ds(pl.multiple_of(ti * ROW_TILE, ROW_TILE), ROW_TILE)
        store_keys(ctx_ckv_ref[0, 0, rows, :], ctx_kpe_ref[0, 0, rows, :], rows)
        return carry

    if n_tiles == 1:
        proj_body(0, 0)
    else:
        lax.fori_loop(0, n_tiles, proj_body, 0)
    if has_ctx:
        lax.fori_loop(0, n_ctx // ROW_TILE, ctx_body, 0)

    def tile_body(ti, carry):
        rows = pl.ds(pl.multiple_of(ti * ROW_TILE, ROW_TILE), ROW_TILE)
        for g0 in range(0, MLA_HEADS, MLA_HEAD_GROUP):
            heads = range(g0, g0 + MLA_HEAD_GROUP)
            ss = [_mm_nt(q_scr[rows, hd * hp:(hd + 1) * hp], k_scr[:, hd * hp:(hd + 1) * hp]) for hd in heads]
            ms = [jnp.max(s, axis=-1, keepdims=True) for s in ss]
            es = [jnp.exp(s - m) for s, m in zip(ss, ms)]
            ls = [jnp.sum(e, axis=-1, keepdims=True) for e in es]
            os_ = [_mm(e, v_scr[:, hd * MLA_VDIM:(hd + 1) * MLA_VDIM]) / l for e, l, hd in zip(es, ls, heads)]
            for j in range(0, MLA_HEAD_GROUP, 2):
                lo = (g0 + j) * MLA_VDIM
                o_scr[:, lo:lo + 2 * MLA_VDIM] = jnp.concatenate(os_[j:j + 2], axis=-1)
        g = _mm(h_scr[rows, :], w_in_ref[:, c_g:])
        y = _mm(o_scr[...] * _silu(g), w_out_ref[...])
        out_ref[0, rows, :] = _residual_ln(x_ref[0, rows, :], y, gate, lng_ref[0], lnb_ref[0])
        return carry

    if n_tiles == 1:
        tile_body(0, 0)
    else:
        lax.fori_loop(0, n_tiles, tile_body, 0)


def _mla_layer(x, mod, layer, ln_g, ln_b, weights, q_norm, kv_norm, ctx_ckv, ctx_kpe, *, row_of_batch):
    bsz, n, d = x.shape
    w_in, wq, wk, wv, w_out = weights
    has_ctx = ctx_ckv is not None
    in_specs = [
        _seq_spec(n, d),
        pl.BlockSpec((1, 3, SUBLANES, d), lambda b: (layer, 0, 0, 0)),
        pl.BlockSpec((1, 1, d), lambda b: (layer, 0, 0)),
        pl.BlockSpec((1, 1, d), lambda b: (layer, 0, 0)),
        _const_spec(w_in.shape),
        _const_spec(q_norm.shape),
        _const_spec(kv_norm.shape),
        _const_spec(wq.shape),
        _const_spec(wk.shape),
        _const_spec(wv.shape),
        _const_spec(w_out.shape),
    ]
    args = [x, mod, ln_g, ln_b, w_in, q_norm, kv_norm, wq, wk, wv, w_out]
    out_specs = [_seq_spec(n, d)]
    out_shape = [jax.ShapeDtypeStruct((bsz, n, d), F32)]
    n_ctx = 0
    if has_ctx:
        n_ctx = ctx_ckv.shape[2]
        ctx_kpe = jnp.pad(ctx_kpe, ((0, 0), (0, 0), (0, 0), (MLA_NOPE, MLA_HEAD_PAD - MLA_QK)))
        cos, sin = _rope_tables(n)
        in_specs += [
            pl.BlockSpec((1, 1, n_ctx, MLA_KV_RANK), lambda b: (b, 0, 0, 0)),
            pl.BlockSpec((1, 1, n_ctx, MLA_HEAD_PAD), lambda b: (b, 0, 0, 0)),
            _const_spec(cos.shape),
            _const_spec(sin.shape),
        ]
        args += [ctx_ckv, ctx_kpe, cos, sin]
    else:
        out_specs += [pl.BlockSpec((1, 1, n, MLA_KV_RANK), lambda b: (b, 0, 0, 0)),
                      pl.BlockSpec((1, 1, n, MLA_ROPE), lambda b: (b, 0, 0, 0))]
        out_shape += [jax.ShapeDtypeStruct((bsz, 1, n, MLA_KV_RANK), F32),
                      jax.ShapeDtypeStruct((bsz, 1, n, MLA_ROPE), F32)]
    kernel = functools.partial(_mla_kernel, row_of_batch=row_of_batch, has_ctx=has_ctx)
    res = pl.pallas_call(
        kernel,
        grid=(bsz,),
        in_specs=in_specs,
        out_specs=out_specs,
        out_shape=out_shape,
        scratch_shapes=[
            pltpu.VMEM((n, d), BF16),
            pltpu.VMEM((n, MLA_HEADS * MLA_HEAD_PAD), BF16),
            pltpu.VMEM((n_ctx + n, MLA_HEADS * MLA_HEAD_PAD), BF16),
            pltpu.VMEM((n_ctx + n, MLA_HEADS * MLA_VDIM), BF16),
            pltpu.VMEM((ROW_TILE, MLA_HEADS * MLA_VDIM), F32),
        ],
        compiler_params=_compiler_params(1),
        name="mla_layer",
    )(*args)
    return res[0] if has_ctx else tuple(res)


def kernel(x_prompt, x_sample, state_hgrn, state_rglru, cache_mla_ckv, cache_mla_kpe, c, c_ctx, ada_w, ada_b, ln_g, ln_b, hg_w_in, hg_lb_logits, hg_norm_g, hg_w_out, sc_w_in, sc_conv_w, sc_conv_b, sc_w_out, rg_w_in, rg_conv_w, rg_conv_b, rg_w_gate, rg_b_gate, rg_lambda, rg_w_out, mla_w_in, mla_q_norm, mla_kv_norm, mla_w_qb, mla_w_kvb, mla_w_out):
    d = x_prompt.shape[-1]
    n_dec = c.shape[0]
    cond = jnp.zeros((SUBLANES, d), F32).at[0].set(c_ctx).at[1:1 + n_dec].set(c)
    mod = _modulation(cond, ada_w, ada_b)
    lng = ln_g.reshape(DEPTH, 1, d)
    lnb = ln_b.reshape(DEPTH, 1, d)
    hg_wi, hg_wo = hg_w_in[0].astype(BF16), hg_w_out[0].astype(BF16)
    sc_wi, sc_wo = sc_w_in[0].astype(BF16), sc_w_out[0].astype(BF16)
    rg_wi, rg_wg, rg_wo = rg_w_in[0].astype(BF16), rg_w_gate[0].astype(BF16), rg_w_out[0].astype(BF16)
    mla_w = _mla_weights(mla_w_in[0], mla_w_qb[0], mla_w_kvb[0], mla_w_out[0])

    def run(x, rob, st_hg, st_rg, ctx_ckv, ctx_kpe):
        r0 = _hgrn_layer(x, mod, 0, lng, lnb, hg_wi, hg_lb_logits, hg_norm_g, hg_wo, st_hg, row_of_batch=rob)
        x, new_hg = r0 if st_hg is None else (r0, None)
        x = _sconv_layer(x, mod, 1, lng, lnb, sc_wi, sc_conv_w[0], sc_conv_b, sc_wo, row_of_batch=rob)
        r2 = _rglru_layer(x, mod, 2, lng, lnb, rg_wi, rg_conv_w[0], rg_conv_b, rg_wg, rg_b_gate[0],
                          rg_lambda[0], rg_wo, st_rg, row_of_batch=rob)
        x, new_rg = r2 if st_rg is None else (r2, None)
        r3 = _mla_layer(x, mod, 3, lng, lnb, mla_w, mla_q_norm, mla_kv_norm, ctx_ckv, ctx_kpe,
                        row_of_batch=rob)
        if ctx_ckv is None:
            return (r3[0], new_hg, new_rg, r3[1], r3[2])
        return (r3,)

    y_prompt, new_hg, new_rg, new_ckv, new_kpe = run(x_prompt, False, None, None, None, None)
    (y_sample,) = run(x_sample, True, state_hgrn, state_rglru, cache_mla_ckv, cache_mla_kpe)
    return (y_prompt, y_sample, new_hg, new_rg, new_ckv, new_kpe)
```

```python
import functools

import jax
import jax.numpy as jnp
import numpy as np
from jax import lax
from jax.experimental import pallas as pl
from jax.experimental.pallas import tpu as pltpu

F32 = jnp.float32
BF16 = jnp.bfloat16

LANES = 128
SUBLANES = 8
VMEM_LIMIT_BYTES = 60 * 1024 * 1024

DEPTH = 4
DEEPNORM_ALPHA = (2 * DEPTH) ** 0.25
LN_EPS = 1e-5
RMS_EPS = 1e-6

ROW_TILE = 256

HG_HEADS = 8
HG_KDIM = 128
HG_VDIM = 128
HG_CHUNK = 32
SC_KERNEL = 3
RG_HEADS = 4
RG_KERNEL = 4
RG_C = 8.0
MLA_HEADS = 16
MLA_Q_RANK = 384
MLA_KV_RANK = 256
MLA_NOPE = 64
MLA_ROPE = 32
MLA_VDIM = 64
MLA_HEAD_PAD = 128
MLA_GROUP_SCORES = 1 << 21
ROPE_BASE = 10000.0
GRID_W = 64


def _mm(a, b):
    return jnp.dot(a.astype(BF16), b.astype(BF16), preferred_element_type=F32)


def _mm_nt(a, b):
    return lax.dot_general(a.astype(BF16), b.astype(BF16), (((1,), (1,)), ((), ())),
                           preferred_element_type=F32)


def _mm_tn(a, b):
    return lax.dot_general(a.astype(BF16), b.astype(BF16), (((0,), (0,)), ((), ())),
                           preferred_element_type=F32)


def _silu(x):
    return x * jax.nn.sigmoid(x)


def _mod_rows(mod_ref, row):
    if isinstance(row, int):
        return tuple(mod_ref[0, j, row:row + 1, :] for j in range(3))
    rows = lax.broadcasted_iota(jnp.int32, (SUBLANES, 1), 0)
    return tuple(jnp.sum(jnp.where(rows == row, mod_ref[0, j], 0.0), axis=0, keepdims=True)
                 for j in range(3))


def _residual_ln(x, y, gate, g, b):
    z = DEEPNORM_ALPHA * x + gate * y
    mu = jnp.mean(z, axis=-1, keepdims=True)
    zc = z - mu
    var = jnp.mean(zc * zc, axis=-1, keepdims=True)
    return zc * lax.rsqrt(var + LN_EPS) * g + b


def _const_spec(shape):
    nd = len(shape)
    return pl.BlockSpec(shape, lambda *_: (0,) * nd, pipeline_mode=pl.Buffered(1))


def _seq_spec(n, d):
    mode = dict(pipeline_mode=pl.Buffered(1)) if n > ROW_TILE else {}
    return pl.BlockSpec((1, n, d), lambda b: (b, 0, 0), **mode)


def _compiler_params(n_grid):
    return pltpu.CompilerParams(dimension_semantics=("arbitrary",) * n_grid,
                                vmem_limit_bytes=VMEM_LIMIT_BYTES)


def _mod_kernel(cond_ref, w_ref, b_ref, out_ref):
    s = _silu(cond_ref[...])
    out_ref[0, 0] = _mm(s, w_ref[0]) + b_ref[0, 0]


def _modulation(cond, ada_w, ada_b):
    n_layers, d, _ = ada_w.shape
    rows = cond.shape[0]
    return pl.pallas_call(
        _mod_kernel,
        grid=(n_layers, 3),
        in_specs=[
            pl.BlockSpec((rows, d), lambda l, j: (0, 0)),
            pl.BlockSpec((1, d, d), lambda l, j: (l, 0, j)),
            pl.BlockSpec((1, 1, 1, d), lambda l, j: (l, j, 0, 0)),
        ],
        out_specs=pl.BlockSpec((1, 1, rows, d), lambda l, j: (l, j, 0, 0)),
        out_shape=jax.ShapeDtypeStruct((n_layers, 3, rows, d), F32),
        compiler_params=_compiler_params(2),
        name="adaln_modulation",
    )(cond, ada_w, ada_b.reshape(n_layers, 3, 1, d))


def _sconv_kernel(x_ref, mod_ref, lng_ref, lnb_ref, w_in_ref, cw_ref, cb_ref, w_out_ref,
                  out_ref, u_scr, *, row_of_batch):
    n, d = x_ref.shape[1], x_ref.shape[2]
    width = w_out_ref.shape[0]
    row = pl.program_id(0) + 1 if row_of_batch else 0
    shift, scale, gate = _mod_rows(mod_ref, row)
    x = x_ref[0]
    h = (x * (1.0 + scale) + shift).astype(BF16)
    t = lax.broadcasted_iota(jnp.int32, (n, 1), 0)
    chunk = 2 * LANES
    def project(j):
        return tuple(_mm(h, w_in_ref[:, i * width + j * chunk:i * width + (j + 1) * chunk]) for i in range(4))

    nxt = project(0)
    for j in range(width // chunk):
        c0 = j * chunk
        bg, cg, v, g = nxt
        if j + 1 < width // chunk:
            nxt = project(j + 1)
        p = cg * v
        p_prev = jnp.where(t >= 1, pltpu.roll(p, 1, 0), 0.0)
        p_next = jnp.where(t < n - 1, pltpu.roll(p, n - 1, 0), 0.0)
        z = (cw_ref[0:1, c0:c0 + chunk] * p_prev + cw_ref[1:2, c0:c0 + chunk] * p
             + cw_ref[2:3, c0:c0 + chunk] * p_next + cb_ref[:, c0:c0 + chunk])
        u_scr[:, c0:c0 + chunk] = (_silu(g) * bg * z).astype(BF16)
    y = _mm(u_scr[...], w_out_ref[...])
    out_ref[0] = _residual_ln(x, y, gate, lng_ref[0], lnb_ref[0])


def _sconv_layer(x, mod, layer, ln_g, ln_b, w_in, conv_w, conv_b, w_out, *, row_of_batch):
    bsz, n, d = x.shape
    width = w_out.shape[0]
    kernel = functools.partial(_sconv_kernel, row_of_batch=row_of_batch)
    return pl.pallas_call(
        kernel,
        grid=(bsz,),
        in_specs=[
            _seq_spec(n, d),
            pl.BlockSpec((1, 3, SUBLANES, d), lambda b: (layer, 0, 0, 0)),
            pl.BlockSpec((1, 1, d), lambda b: (layer, 0, 0)),
            pl.BlockSpec((1, 1, d), lambda b: (layer, 0, 0)),
            _const_spec(w_in.shape),
            _const_spec(conv_w.shape),
            _const_spec(conv_b.shape),
            _const_spec(w_out.shape),
        ],
        out_specs=_seq_spec(n, d),
        out_shape=jax.ShapeDtypeStruct((bsz, n, d), F32),
        scratch_shapes=[pltpu.VMEM((n, width), BF16)],
        compiler_params=_compiler_params(1),
        name="sconv_layer",
    )(x, mod, ln_g, ln_b, w_in, conv_w, conv_b, w_out)


def _chunk_cumsum(x, reverse):
    n = x.shape[0]
    pos = lax.broadcasted_iota(jnp.int32, (n, 1), 0) % HG_CHUNK
    s = 1
    while s < HG_CHUNK:
        if reverse:
            x = x + jnp.where(pos < HG_CHUNK - s, pltpu.roll(x, n - s, 0), 0.0)
        else:
            x = x + jnp.where(pos >= s, pltpu.roll(x, s, 0), 0.0)
        s *= 2
    return x


def _hgrn_kernel(*refs, row_of_batch, layer, has_s0, emit_state):
    it = iter(refs)
    x_ref, mod_ref, lng_ref, lnb_ref, w_in_ref, lbl_ref, ng_ref, w_out_ref = (next(it) for _ in range(8))
    s0_ref = next(it) if has_s0 else None
    out_ref = next(it)
    s_out_ref = next(it) if emit_state else None
    h_scr, qrel_scr, krel_scr, qp_scr, kp_scr, v_scr, dec_scr, o_scr, g_scr, u_scr = it

    n, d = x_ref.shape[1], x_ref.shape[2]
    hk = HG_HEADS * HG_KDIM
    n_tiles = n // ROW_TILE
    chunks_per_tile = ROW_TILE // HG_CHUNK
    n_chunks = n // HG_CHUNK
    pair = 2 * LANES

    row = pl.program_id(0) + 1 if row_of_batch else 0
    shift, scale, gate = _mod_rows(mod_ref, row)
    x = x_ref[0]
    h_scr[...] = (x * (1.0 + scale) + shift).astype(BF16)

    lbs = []
    for dr in range(2):
        z = lbl_ref[dr]
        e = jnp.exp(z - jnp.max(z, axis=0, keepdims=True))
        lbs.append(jnp.sum(e[:layer + 1], axis=0, keepdims=True) / jnp.sum(e, axis=0, keepdims=True))

    ri = lax.broadcasted_iota(jnp.int32, (ROW_TILE, ROW_TILE), 0)
    ci = lax.broadcasted_iota(jnp.int32, (ROW_TILE, ROW_TILE), 1)
    same_chunk = (ri // HG_CHUNK) == (ci // HG_CHUNK)
    masks = (same_chunk & (ci <= ri), same_chunk & (ci >= ri))

    row_chunk = (lax.broadcasted_iota(jnp.int32, (ROW_TILE, LANES), 0) // HG_CHUNK).astype(BF16)

    def chunk_blocks(a):
        zero = jnp.zeros_like(a)
        return jnp.concatenate([jnp.where(row_chunk == c, a, zero) for c in range(chunks_per_tile)], axis=1)

    def project(hp):
        c0 = hp * pair
        slot = hp % 2
        h = h_scr[...]
        q = _silu(_mm(h, w_in_ref[:, c0:c0 + pair])) * (HG_KDIM ** -0.5)
        v_scr[slot] = _mm(h, w_in_ref[:, 3 * hk + c0:3 * hk + c0 + pair]).astype(BF16)
        q3 = q.reshape(n_chunks, HG_CHUNK, pair)
        for dr in range(2):
            fl = _mm(h, w_in_ref[:, (1 + dr) * hk + c0:(1 + dr) * hk + c0 + pair])
            lb = lbs[dr][:, c0:c0 + pair]
            f = lb + (1.0 - lb) * jax.nn.sigmoid(fl)
            k3 = (1.0 - f).reshape(n_chunks, HG_CHUNK, pair)
            cum = _chunk_cumsum(jnp.log(f), reverse=(dr == 1)).reshape(n_chunks, HG_CHUNK, pair)
            mid = HG_CHUNK // 2 - 1
            if dr == 0:
                ref, tot = cum[:, mid:mid + 1], cum[:, HG_CHUNK - 1:HG_CHUNK]
            else:
                ref, tot = cum[:, HG_CHUNK - 1 - mid:HG_CHUNK - mid], cum[:, 0:1]
            q_rel = q3 * jnp.exp(cum - ref)
            k_rel = k3 * jnp.exp(ref - cum)
            qrel_scr[slot, dr] = q_rel.reshape(n, pair).astype(BF16)
            krel_scr[slot, dr] = k_rel.reshape(n, pair).astype(BF16)
            qp_scr[slot, dr] = (q_rel * jnp.exp(ref)).reshape(n, pair).astype(BF16)
            kp_scr[slot, dr] = (k_rel * jnp.exp(tot - ref)).reshape(n, pair).astype(BF16)
            dec_scr[slot, dr] = jnp.exp(tot)
        g_scr[slot] = _silu(_mm(h, w_in_ref[:, 4 * hk + c0:4 * hk + c0 + pair]))

    def recur(hp):
        c0 = hp * pair
        slot = hp % 2
        for dr in range(2):
            for h2 in range(2):
                l0 = h2 * LANES
                head = hp * 2 + h2
                if has_s0:
                    s_t0 = s0_ref[0, 0, dr, head].T
                else:
                    s_t0 = jnp.zeros((HG_VDIM, HG_KDIM), F32)

                def tile_body(i, s_t, dr=dr, l0=l0, slot=slot):
                    ti = i if dr == 0 else n_tiles - 1 - i
                    rows = pl.ds(pl.multiple_of(ti * ROW_TILE, ROW_TILE), ROW_TILE)
                    lanes = slice(l0, l0 + LANES)
                    v = v_scr[slot, rows, lanes]
                    att = _mm_nt(qrel_scr[slot, dr, rows, lanes], krel_scr[slot, dr, rows, lanes])
                    o_tile = _mm(jnp.where(masks[dr], att, 0.0), v)
                    upd = _mm_tn(v, chunk_blocks(kp_scr[slot, dr, rows, lanes]))
                    s_prev = [None] * chunks_per_tile
                    order = range(chunks_per_tile) if dr == 0 else range(chunks_per_tile - 1, -1, -1)
                    for c in order:
                        s_prev[c] = s_t.astype(BF16)
                        dec = dec_scr[slot, dr, ti * chunks_per_tile + c][:, lanes]
                        s_t = s_t * dec + upd[:, c * LANES:(c + 1) * LANES]
                    o_tile += _mm_nt(chunk_blocks(qp_scr[slot, dr, rows, lanes]),
                                     jnp.concatenate(s_prev, axis=1))
                    if dr == 0:
                        o_scr[slot, rows, lanes] = o_tile
                    else:
                        o_scr[slot, rows, lanes] += o_tile
                    return s_t

                s_t = s_t0
                for i in range(n_tiles):
                    s_t = tile_body(i, s_t)
                if emit_state:
                    s_out_ref[0, 0, dr, head] = s_t.T

        for h2 in range(2):
            lanes = slice(h2 * LANES, (h2 + 1) * LANES)
            o = o_scr[slot, :, lanes]
            ms = jnp.mean(o * o, axis=-1, keepdims=True)
            on = o * lax.rsqrt(ms + RMS_EPS) * ng_ref[:, c0 + h2 * LANES:c0 + (h2 + 1) * LANES]
            u_scr[:, c0 + h2 * LANES:c0 + (h2 + 1) * LANES] = (on * g_scr[slot, :, lanes]).astype(BF16)

    n_pairs = hk // pair
    project(0)
    for hp in range(n_pairs):
        if hp + 1 < n_pairs:
            project(hp + 1)
        recur(hp)

    y = _mm(u_scr[...], w_out_ref[...])
    out_ref[0] = _residual_ln(x, y, gate, lng_ref[0], lnb_ref[0])


def _hgrn_layer(x, mod, layer, ln_g, ln_b, w_in, lb_logits, norm_g, w_out, state, *, row_of_batch):
    bsz, n, d = x.shape
    hv = w_out.shape[0]
    has_s0 = state is not None
    emit_state = not has_s0
    pair = 2 * LANES
    state_block = (1, 1, 2, HG_HEADS, HG_KDIM, HG_VDIM)
    state_spec = pl.BlockSpec(state_block, lambda b: (b, 0, 0, 0, 0, 0))
    in_specs = [
        _seq_spec(n, d),
        pl.BlockSpec((1, 3, SUBLANES, d), lambda b: (layer, 0, 0, 0)),
        pl.BlockSpec((1, 1, d), lambda b: (layer, 0, 0)),
        pl.BlockSpec((1, 1, d), lambda b: (layer, 0, 0)),
        _const_spec(w_in.shape),
        _const_spec(lb_logits.shape),
        _const_spec(norm_g.shape),
        _const_spec(w_out.shape),
    ]
    args = [x, mod, ln_g, ln_b, w_in, lb_logits, norm_g, w_out]
    out_specs = [_seq_spec(n, d)]
    out_shape = [jax.ShapeDtypeStruct((bsz, n, d), F32)]
    if has_s0:
        in_specs.append(state_spec)
        args.append(state)
    if emit_state:
        out_specs.append(state_spec)
        out_shape.append(jax.ShapeDtypeStruct((bsz,) + state_block[1:], F32))
    kernel = functools.partial(_hgrn_kernel, row_of_batch=row_of_batch, layer=layer,
                               has_s0=has_s0, emit_state=emit_state)
    res = pl.pallas_call(
        kernel,
        grid=(bsz,),
        in_specs=in_specs,
        out_specs=out_specs,
        out_shape=out_shape,
        scratch_shapes=[
            pltpu.VMEM((n, d), BF16),
            pltpu.VMEM((2, 2, n, pair), BF16),
            pltpu.VMEM((2, 2, n, pair), BF16),
            pltpu.VMEM((2, 2, n, pair), BF16),
            pltpu.VMEM((2, 2, n, pair), BF16),
            pltpu.VMEM((2, n, pair), BF16),
            pltpu.VMEM((2, 2, n // HG_CHUNK, 1, pair), F32),
            pltpu.VMEM((2, n, pair), F32),
            pltpu.VMEM((2, n, pair), F32),
            pltpu.VMEM((n, hv), BF16),
        ],
        compiler_params=_compiler_params(1),
        name="hgrn2_layer",
    )(*args)
    return res[0] if has_s0 else tuple(res)


def _block_scan(a, x, reverse):
    n, c = a.shape
    nb = n // SUBLANES
    a = a.reshape(nb, SUBLANES, c)
    x = x.reshape(nb, SUBLANES, c)
    pos = lax.broadcasted_iota(jnp.int32, (1, SUBLANES, 1), 1)
    s = 1
    while s < SUBLANES:
        if reverse:
            keep = pos < SUBLANES - s
            shift = SUBLANES - s
        else:
            keep = pos >= s
            shift = s
        a_in = jnp.where(keep, pltpu.roll(a, shift, 1), 1.0)
        x_in = jnp.where(keep, pltpu.roll(x, shift, 1), 0.0)
        x = a * x_in + x
        a = a * a_in
        s *= 2
    return a.reshape(n, c), x.reshape(n, c)


def _rglru_kernel(*refs, row_of_batch, has_s0, emit_state):
    it = iter(refs)
    (x_ref, mod_ref, lng_ref, lnb_ref, w_in_ref, cw_ref, cb_ref, wg_ref, bg_ref, lam_ref,
     w_out_ref) = (next(it) for _ in range(11))
    s0_ref = next(it) if has_s0 else None
    out_ref = next(it)
    s_out_ref = next(it) if emit_state else None
    h_scr, a_scr, b_scr, gs_scr = it

    n, d = x_ref.shape[1], x_ref.shape[2]
    width = w_out_ref.shape[0]
    blk = width // RG_HEADS
    nb = n // SUBLANES

    row = pl.program_id(0) + 1 if row_of_batch else 0
    shift, scale, gate = _mod_rows(mod_ref, row)
    h_scr[...] = (x_ref[0] * (1.0 + scale) + shift).astype(BF16)

    z = -lam_ref[...]
    coef = -RG_C * (jnp.maximum(z, 0.0) + jnp.log1p(jnp.exp(-jnp.abs(z))))

    t = lax.broadcasted_iota(jnp.int32, (n, 1), 0)

    def project(hh):
        h = h_scr[...]
        return (_mm(h, w_in_ref[:, hh * blk:(hh + 1) * blk]),
                _mm(h, w_in_ref[:, width + hh * blk:width + (hh + 1) * blk]))

    def mix(hh, u_pre, g):
        cols = slice(hh * blk, (hh + 1) * blk)
        u = (cw_ref[0:1, cols] * jnp.where(t >= 2, pltpu.roll(u_pre, 2, 0), 0.0)
             + cw_ref[1:2, cols] * jnp.where(t >= 1, pltpu.roll(u_pre, 1, 0), 0.0)
             + cw_ref[2:3, cols] * u_pre
             + cw_ref[3:4, cols] * jnp.where(t < n - 1, pltpu.roll(u_pre, n - 1, 0), 0.0)
             + cb_ref[:, cols])
        gs_scr[:, cols] = _silu(g)
        gates = [jax.nn.sigmoid(_mm(u, wg_ref[dr, hh]) + bg_ref[dr, hh]) for dr in range(2)]
        for dr in range(2):
            a = jnp.exp(coef[dr:dr + 1, cols] * gates[dr][:, :blk])
            xin = jnp.exp(0.5 * jnp.log(1.0 - a * a)) * (gates[dr][:, blk:] * u)
            a_blk, h_blk = _block_scan(a, xin, reverse=(dr == 1))
            a_scr[dr, :, cols] = a_blk
            b_scr[dr, :, cols] = h_blk

    nxt = project(0)
    for hh in range(RG_HEADS):
        cur = nxt
        if hh + 1 < RG_HEADS:
            nxt = project(hh + 1)
        mix(hh, *cur)

    if has_s0:
        carry0 = (s0_ref[0, 0, 0:1, :], s0_ref[0, 0, 1:2, :])
    else:
        carry0 = (jnp.zeros((1, width), F32), jnp.zeros((1, width), F32))

    def carry_body(i, carry):
        cf, cb = carry
        rf = pl.ds(pl.multiple_of(i * SUBLANES, SUBLANES), SUBLANES)
        rb = pl.ds(pl.multiple_of((nb - 1 - i) * SUBLANES, SUBLANES), SUBLANES)
        hf = a_scr[0, rf, :] * cf + b_scr[0, rf, :]
        hb = a_scr[1, rb, :] * cb + b_scr[1, rb, :]
        b_scr[0, rf, :] = hf
        b_scr[1, rb, :] = hb
        return hf[SUBLANES - 1:SUBLANES, :], hb[0:1, :]

    cf, cb = lax.fori_loop(0, nb, carry_body, carry0)
    if emit_state:
        s_out_ref[0, 0, 0:1, :] = cf
        s_out_ref[0, 0, 1:2, :] = cb

    for r0 in range(0, n, ROW_TILE):
        rows = slice(r0, r0 + ROW_TILE)
        mix = ((b_scr[0, rows, :] + b_scr[1, rows, :]) * gs_scr[rows, :]).astype(BF16)
        y = _mm(mix, w_out_ref[...])
        out_ref[0, rows, :] = _residual_ln(x_ref[0, rows, :], y, gate, lng_ref[0], lnb_ref[0])


def _rglru_layer(x, mod, layer, ln_g, ln_b, w_in, conv_w, conv_b, w_gate, b_gate, lam, w_out, state,
                 *, row_of_batch):
    bsz, n, d = x.shape
    width = w_out.shape[0]
    has_s0 = state is not None
    emit_state = not has_s0
    b_gate = b_gate.reshape(2, RG_HEADS, 1, b_gate.shape[-1])
    state_block = (1, 1, 2, width)
    state_spec = pl.BlockSpec(state_block, lambda b: (b, 0, 0, 0))
    in_specs = [
        _seq_spec(n, d),
        pl.BlockSpec((1, 3, SUBLANES, d), lambda b: (layer, 0, 0, 0)),
        pl.BlockSpec((1, 1, d), lambda b: (layer, 0, 0)),
        pl.BlockSpec((1, 1, d), lambda b: (layer, 0, 0)),
        _const_spec(w_in.shape),
        _const_spec(conv_w.shape),
        _const_spec(conv_b.shape),
        _const_spec(w_gate.shape),
        _const_spec(b_gate.shape),
        _const_spec(lam.shape),
        _const_spec(w_out.shape),
    ]
    args = [x, mod, ln_g, ln_b, w_in, conv_w, conv_b, w_gate, b_gate, lam, w_out]
    out_specs = [_seq_spec(n, d)]
    out_shape = [jax.ShapeDtypeStruct((bsz, n, d), F32)]
    if has_s0:
        in_specs.append(state_spec)
        args.append(state)
    if emit_state:
        out_specs.append(state_spec)
        out_shape.append(jax.ShapeDtypeStruct((bsz,) + state_block[1:], F32))
    kernel = functools.partial(_rglru_kernel, row_of_batch=row_of_batch, has_s0=has_s0,
                               emit_state=emit_state)
    res = pl.pallas_call(
        kernel,
        grid=(bsz,),
        in_specs=in_specs,
        out_specs=out_specs,
        out_shape=out_shape,
        scratch_shapes=[
            pltpu.VMEM((n, d), BF16),
            pltpu.VMEM((2, n, width), F32),
            pltpu.VMEM((2, n, width), F32),
            pltpu.VMEM((n, width), F32),
        ],
        compiler_params=_compiler_params(1),
        name="rglru_layer",
    )(*args)
    return res[0] if has_s0 else tuple(res)


MLA_QK = MLA_NOPE + MLA_ROPE


def _mla_weights(w_in, w_qb, w_kvb, w_out):
    d = w_in.shape[0]
    a = MLA_Q_RANK + MLA_KV_RANK
    pad_l, pad_r = MLA_NOPE, MLA_HEAD_PAD - MLA_QK
    w_in_r = jnp.concatenate(
        [w_in[:, :a], jnp.zeros((d, pad_l), w_in.dtype), w_in[:, a:a + MLA_ROPE],
         jnp.zeros((d, pad_r), w_in.dtype), w_in[:, a + MLA_ROPE:]], axis=1)
    wq = w_qb.reshape(MLA_Q_RANK, MLA_HEADS, MLA_QK)
    wq = jnp.pad(wq, ((0, 0), (0, 0), (0, pad_r))).reshape(MLA_Q_RANK, MLA_HEADS * MLA_HEAD_PAD)
    wkv = w_kvb.reshape(MLA_KV_RANK, MLA_HEADS, MLA_NOPE + MLA_VDIM)
    wk = jnp.pad(wkv[:, :, :MLA_NOPE], ((0, 0), (0, 0), (0, MLA_HEAD_PAD - MLA_NOPE)))
    wk = wk.reshape(MLA_KV_RANK, MLA_HEADS * MLA_HEAD_PAD)
    wv = wkv[:, :, MLA_NOPE:].reshape(MLA_KV_RANK, MLA_HEADS * MLA_VDIM)
    return tuple(w.astype(BF16) for w in (w_in_r, wq, wk, wv, w_out))


def _rope_tables(n):
    half = MLA_ROPE // 2
    quarter = half // 2
    t = np.arange(n)
    inv_freq = ROPE_BASE ** (-np.arange(0, half, 2, dtype=np.float32) / half)
    cos = np.ones((n, MLA_HEAD_PAD), np.float32)
    sin = np.zeros((n, MLA_HEAD_PAD), np.float32)
    for k, pos in enumerate((t // GRID_W, t % GRID_W)):
        ang = pos[:, None].astype(np.float32) * inv_freq[None].astype(np.float32)
        c, s = np.cos(ang), np.sin(ang)
        base = MLA_NOPE + k * half
        cos[:, base:base + quarter] = c
        cos[:, base + quarter:base + half] = c
        sin[:, base:base + quarter] = -s
        sin[:, base + quarter:base + half] = s
    return jnp.asarray(cos), jnp.asarray(sin)


def _rope(x, cos, sin):
    quarter = MLA_ROPE // 4
    lane = lax.broadcasted_iota(jnp.int32, (1, MLA_HEAD_PAD), 1)
    first = ((lane - MLA_NOPE) % (2 * quarter)) < quarter
    partner = jnp.where(first, pltpu.roll(x, MLA_HEAD_PAD - quarter, 1), pltpu.roll(x, quarter, 1))
    return x * cos + partner * sin


def _rms(x, g):
    return x * lax.rsqrt(jnp.mean(x * x, axis=-1, keepdims=True) + RMS_EPS) * g


def _mla_kernel(*refs, row_of_batch, has_ctx):
    it = iter(refs)
    (x_ref, mod_ref, lng_ref, lnb_ref, w_in_ref, qn_ref, kvn_ref, wq_ref, wk_ref, wv_ref,
     w_out_ref) = (next(it) for _ in range(11))
    if has_ctx:
        ctx_ckv_ref, ctx_kpe_ref, cos_ref, sin_ref = (next(it) for _ in range(4))
    out_ref = next(it)
    if not has_ctx:
        ckv_out_ref, kpe_out_ref = next(it), next(it)
    h_scr, q_scr, k_scr, v_scr, o_scr = it

    n = x_ref.shape[1]
    n_ctx = k_scr.shape[0] - n
    hp = MLA_HEAD_PAD
    c_ckv = MLA_Q_RANK
    c_kpe = c_ckv + MLA_KV_RANK
    c_g = c_kpe + hp

    row = pl.program_id(0) + 1 if row_of_batch else 0
    shift, scale, gate = _mod_rows(mod_ref, row)
    sm_scale = MLA_QK ** -0.5
    n_tiles = n // ROW_TILE
    group = 2
    while group < MLA_HEADS and 2 * group * ROW_TILE * (n_ctx + n) <= MLA_GROUP_SCORES:
        group *= 2

    def store_keys(ckv, kpe, krows):
        k_all = _mm(ckv, wk_ref[...])
        v_scr[krows, :] = _mm(ckv, wv_ref[...]).astype(BF16)
        for hd in range(MLA_HEADS):
            cols = slice(hd * hp, (hd + 1) * hp)
            k_scr[krows, cols] = (k_all[:, cols] + kpe).astype(BF16)

    def proj_body(ti, carry):
        r0 = pl.multiple_of(ti * ROW_TILE, ROW_TILE)
        rows = pl.ds(r0, ROW_TILE)
        h = (x_ref[0, rows, :] * (1.0 + scale) + shift).astype(BF16)
        h_scr[rows, :] = h
        cq = _rms(_mm(h, w_in_ref[:, :c_ckv]), qn_ref[...])
        ckv = _rms(_mm(h, w_in_ref[:, c_ckv:c_kpe]), kvn_ref[...])
        kpe = _mm(h, w_in_ref[:, c_kpe:c_g])
        if has_ctx:
            cos, sin = cos_ref[rows, :], sin_ref[rows, :]
            kpe = _rope(kpe, cos, sin)
        else:
            ckv_out_ref[0, 0, rows, :] = ckv
            kpe_out_ref[0, 0, rows, :] = kpe[:, MLA_NOPE:MLA_QK]
        q_all = _mm(cq, wq_ref[...])
        for hd in range(MLA_HEADS):
            cols = slice(hd * hp, (hd + 1) * hp)
            qh = q_all[:, cols]
            if has_ctx:
                qh = _rope(qh, cos, sin)
            q_scr[rows, cols] = (qh * sm_scale).astype(BF16)
        store_keys(ckv, kpe, pl.ds(pl.multiple_of(n_ctx + r0, ROW_TILE), ROW_TILE))
        return carry

    def ctx_body(ti, carry):
        rows = pl.ds(pl.multiple_of(ti * ROW_TILE, ROW_TILE), ROW_TILE)
        store_keys(ctx_ckv_ref[0, 0, rows, :], ctx_kpe_ref[0, 0, rows, :], rows)
        return carry

    if n_tiles == 1:
        proj_body(0, 0)
    else:
        lax.fori_loop(0, n_tiles, proj_body, 0)
    if has_ctx:
        lax.fori_loop(0, n_ctx // ROW_TILE, ctx_body, 0)

    def tile_body(ti, carry):
        rows = pl.ds(pl.multiple_of(ti * ROW_TILE, ROW_TILE), ROW_TILE)
        for g0 in range(0, MLA_HEADS, group):
            heads = range(g0, g0 + group)
            ss = [_mm_nt(q_scr[rows, hd * hp:(hd + 1) * hp], k_scr[:, hd * hp:(hd + 1) * hp]) for hd in heads]
            if group == MLA_HEADS:
                g = _silu(_mm(h_scr[rows, :], w_in_ref[:, c_g:]))
            ms = [jnp.max(s, axis=-1, keepdims=True) for s in ss]
            es = [jnp.exp(s - m) for s, m in zip(ss, ms)]
            ls = [jnp.sum(e, axis=-1, keepdims=True) for e in es]
            os_ = [_mm(e, v_scr[:, hd * MLA_VDIM:(hd + 1) * MLA_VDIM]) / l for e, l, hd in zip(es, ls, heads)]
            if group == MLA_HEADS:
                o = jnp.concatenate(os_, axis=-1)
            else:
                for j in range(0, group, 2):
                    lo = (g0 + j) * MLA_VDIM
                    o_scr[:, lo:lo + 2 * MLA_VDIM] = jnp.concatenate(os_[j:j + 2], axis=-1)
        if group != MLA_HEADS:
            g = _silu(_mm(h_scr[rows, :], w_in_ref[:, c_g:]))
            o = o_scr[...]
        y = _mm(o * g, w_out_ref[...])
        out_ref[0, rows, :] = _residual_ln(x_ref[0, rows, :], y, gate, lng_ref[0], lnb_ref[0])
        return carry

    if n_tiles == 1:
        tile_body(0, 0)
    else:
        lax.fori_loop(0, n_tiles, tile_body, 0)


def _mla_layer(x, mod, layer, ln_g, ln_b, weights, q_norm, kv_norm, ctx_ckv, ctx_kpe, *, row_of_batch):
    bsz, n, d = x.shape
    w_in, wq, wk, wv, w_out = weights
    has_ctx = ctx_ckv is not None
    in_specs = [
        _seq_spec(n, d),
        pl.BlockSpec((1, 3, SUBLANES, d), lambda b: (layer, 0, 0, 0)),
        pl.BlockSpec((1, 1, d), lambda b: (layer, 0, 0)),
        pl.BlockSpec((1, 1, d), lambda b: (layer, 0, 0)),
        _const_spec(w_in.shape),
        _const_spec(q_norm.shape),
        _const_spec(kv_norm.shape),
        _const_spec(wq.shape),
        _const_spec(wk.shape),
        _const_spec(wv.shape),
        _const_spec(w_out.shape),
    ]
    args = [x, mod, ln_g, ln_b, w_in, q_norm, kv_norm, wq, wk, wv, w_out]
    out_specs = [_seq_spec(n, d)]
    out_shape = [jax.ShapeDtypeStruct((bsz, n, d), F32)]
    n_ctx = 0
    if has_ctx:
        n_ctx = ctx_ckv.shape[2]
        ctx_kpe = jnp.pad(ctx_kpe, ((0, 0), (0, 0), (0, 0), (MLA_NOPE, MLA_HEAD_PAD - MLA_QK)))
        cos, sin = _rope_tables(n)
        in_specs += [
            pl.BlockSpec((1, 1, n_ctx, MLA_KV_RANK), lambda b: (b, 0, 0, 0)),
            pl.BlockSpec((1, 1, n_ctx, MLA_HEAD_PAD), lambda b: (b, 0, 0, 0)),
            _const_spec(cos.shape),
            _const_spec(sin.shape),
        ]
        args += [ctx_ckv, ctx_kpe, cos, sin]
    else:
        out_specs += [pl.BlockSpec((1, 1, n, MLA_KV_RANK), lambda b: (b, 0, 0, 0)),
                      pl.BlockSpec((1, 1, n, MLA_ROPE), lambda b: (b, 0, 0, 0))]
        out_shape += [jax.ShapeDtypeStruct((bsz, 1, n, MLA_KV_RANK), F32),
                      jax.ShapeDtypeStruct((bsz, 1, n, MLA_ROPE), F32)]
    kernel = functools.partial(_mla_kernel, row_of_batch=row_of_batch, has_ctx=has_ctx)
    res = pl.pallas_call(
        kernel,
        grid=(bsz,),
        in_specs=in_specs,
        out_specs=out_specs,
        out_shape=out_shape,
        scratch_shapes=[
            pltpu.VMEM((n, d), BF16),
            pltpu.VMEM((n, MLA_HEADS * MLA_HEAD_PAD), BF16),
            pltpu.VMEM((n_ctx + n, MLA_HEADS * MLA_HEAD_PAD), BF16),
            pltpu.VMEM((n_ctx + n, MLA_HEADS * MLA_VDIM), BF16),
            pltpu.VMEM((ROW_TILE, MLA_HEADS * MLA_VDIM), F32),
        ],
        compiler_params=_compiler_params(1),
        name="mla_layer",
    )(*args)
    return res[0] if has_ctx else tuple(res)


def kernel(x_prompt, x_sample, state_hgrn, state_rglru, cache_mla_ckv, cache_mla_kpe, c, c_ctx, ada_w, ada_b, ln_g, ln_b, hg_w_in, hg_lb_logits, hg_norm_g, hg_w_out, sc_w_in, sc_conv_w, sc_conv_b, sc_w_out, rg_w_in, rg_conv_w, rg_conv_b, rg_w_gate, rg_b_gate, rg_lambda, rg_w_out, mla_w_in, mla_q_norm, mla_kv_norm, mla_w_qb, mla_w_kvb, mla_w_out):
    d = x_prompt.shape[-1]
    n_dec = c.shape[0]
    cond = jnp.zeros((SUBLANES, d), F32).at[0].set(c_ctx).at[1:1 + n_dec].set(c)
    mod = _modulation(cond, ada_w, ada_b)
    lng = ln_g.reshape(DEPTH, 1, d)
    lnb = ln_b.reshape(DEPTH, 1, d)
    hg_wi, hg_wo = hg_w_in[0].astype(BF16), hg_w_out[0].astype(BF16)
    sc_wi, sc_wo = sc_w_in[0].astype(BF16), sc_w_out[0].astype(BF16)
    rg_wi, rg_wg, rg_wo = rg_w_in[0].astype(BF16), rg_w_gate[0].astype(BF16), rg_w_out[0].astype(BF16)
    mla_w = _mla_weights(mla_w_in[0], mla_w_qb[0], mla_w_kvb[0], mla_w_out[0])

    def run(x, rob, st_hg, st_rg, ctx_ckv, ctx_kpe):
        r0 = _hgrn_layer(x, mod, 0, lng, lnb, hg_wi, hg_lb_logits, hg_norm_g, hg_wo, st_hg, row_of_batch=rob)
        x, new_hg = r0 if st_hg is None else (r0, None)
        x = _sconv_layer(x, mod, 1, lng, lnb, sc_wi, sc_conv_w[0], sc_conv_b, sc_wo, row_of_batch=rob)
        r2 = _rglru_layer(x, mod, 2, lng, lnb, rg_wi, rg_conv_w[0], rg_conv_b, rg_wg, rg_b_gate[0],
                          rg_lambda[0], rg_wo, st_rg, row_of_batch=rob)
        x, new_rg = r2 if st_rg is None else (r2, None)
        r3 = _mla_layer(x, mod, 3, lng, lnb, mla_w, mla_q_norm, mla_kv_norm, ctx_ckv, ctx_kpe,
                        row_of_batch=rob)
        if ctx_ckv is None:
            return (r3[0], new_hg, new_rg, r3[1], r3[2])
        return (r3,)

    y_prompt, new_hg, new_rg, new_ckv, new_kpe = run(x_prompt, False, None, None, None, None)
    (y_sample,) = run(x_sample, True, state_hgrn, state_rglru, cache_mla_ckv, cache_mla_kpe)
    return (y_prompt, y_sample, new_hg, new_rg, new_ckv, new_kpe)
```

```python
import functools

import jax
import jax.numpy as jnp
import numpy as np
from jax import lax
from jax.experimental import pallas as pl
from jax.experimental.pallas import tpu as pltpu

F32 = jnp.float32
BF16 = jnp.bfloat16

LANES = 128
SUBLANES = 8
VMEM_LIMIT_BYTES = 60 * 1024 * 1024

DEPTH = 4
DEEPNORM_ALPHA = (2 * DEPTH) ** 0.25
LN_EPS = 1e-5
RMS_EPS = 1e-6

ROW_TILE = 256

HG_HEADS = 8
HG_KDIM = 128
HG_VDIM = 128
HG_CHUNK = 32
SC_KERNEL = 3
RG_HEADS = 4
RG_KERNEL = 4
RG_C = 8.0
MLA_HEADS = 16
MLA_Q_RANK = 384
MLA_KV_RANK = 256
MLA_NOPE = 64
MLA_ROPE = 32
MLA_VDIM = 64
MLA_HEAD_PAD = 128
MLA_GROUP_SCORES = 1 << 21
ROPE_BASE = 10000.0
GRID_W = 64


def _mm(a, b):
    return jnp.dot(a.astype(BF16), b.astype(BF16), preferred_element_type=F32)


def _mm_nt(a, b):
    return lax.dot_general(a.astype(BF16), b.astype(BF16), (((1,), (1,)), ((), ())),
                           preferred_element_type=F32)


def _mm_tn(a, b):
    return lax.dot_general(a.astype(BF16), b.astype(BF16), (((0,), (0,)), ((), ())),
                           preferred_element_type=F32)


def _silu(x):
    return x * jax.nn.sigmoid(x)


def _mod_rows(mod_ref, row):
    if isinstance(row, int):
        return tuple(mod_ref[0, j, row:row + 1, :] for j in range(3))
    rows = lax.broadcasted_iota(jnp.int32, (SUBLANES, 1), 0)
    return tuple(jnp.sum(jnp.where(rows == row, mod_ref[0, j], 0.0), axis=0, keepdims=True)
                 for j in range(3))


def _residual_ln(x, y, gate, g, b):
    z = DEEPNORM_ALPHA * x + gate * y
    mu = jnp.mean(z, axis=-1, keepdims=True)
    zc = z - mu
    var = jnp.mean(zc * zc, axis=-1, keepdims=True)
    return zc * lax.rsqrt(var + LN_EPS) * g + b


def _const_spec(shape):
    nd = len(shape)
    return pl.BlockSpec(shape, lambda *_: (0,) * nd, pipeline_mode=pl.Buffered(1))


def _seq_spec(n, d, single_buffer=False):
    mode = dict(pipeline_mode=pl.Buffered(1)) if single_buffer else {}
    return pl.BlockSpec((1, n, d), lambda b: (b, 0, 0), **mode)


def _compiler_params(n_grid):
    return pltpu.CompilerParams(dimension_semantics=("arbitrary",) * n_grid,
                                vmem_limit_bytes=VMEM_LIMIT_BYTES)


def _mod_kernel(cond_ref, w_ref, b_ref, out_ref):
    s = _silu(cond_ref[...])
    out_ref[0, 0] = _mm(s, w_ref[0]) + b_ref[0, 0]


def _modulation(cond, ada_w, ada_b):
    n_layers, d, _ = ada_w.shape
    rows = cond.shape[0]
    return pl.pallas_call(
        _mod_kernel,
        grid=(n_layers, 3),
        in_specs=[
            pl.BlockSpec((rows, d), lambda l, j: (0, 0)),
            pl.BlockSpec((1, d, d), lambda l, j: (l, 0, j)),
            pl.BlockSpec((1, 1, 1, d), lambda l, j: (l, j, 0, 0)),
        ],
        out_specs=pl.BlockSpec((1, 1, rows, d), lambda l, j: (l, j, 0, 0)),
        out_shape=jax.ShapeDtypeStruct((n_layers, 3, rows, d), F32),
        compiler_params=_compiler_params(2),
        name="adaln_modulation",
    )(cond, ada_w, ada_b.reshape(n_layers, 3, 1, d))


def _sconv_kernel(x_ref, mod_ref, lng_ref, lnb_ref, w_in_ref, cw_ref, cb_ref, w_out_ref,
                  out_ref, u_scr, *, row_of_batch):
    n, d = x_ref.shape[1], x_ref.shape[2]
    width = w_out_ref.shape[0]
    row = pl.program_id(0) + 1 if row_of_batch else 0
    shift, scale, gate = _mod_rows(mod_ref, row)
    x = x_ref[0]
    h = (x * (1.0 + scale) + shift).astype(BF16)
    t = lax.broadcasted_iota(jnp.int32, (n, 1), 0)
    chunk = 2 * LANES
    def project(j):
        return tuple(_mm(h, w_in_ref[:, i * width + j * chunk:i * width + (j + 1) * chunk]) for i in range(4))

    nxt = project(0)
    for j in range(width // chunk):
        c0 = j * chunk
        bg, cg, v, g = nxt
        if j + 1 < width // chunk:
            nxt = project(j + 1)
        p = cg * v
        p_prev = jnp.where(t >= 1, pltpu.roll(p, 1, 0), 0.0)
        p_next = jnp.where(t < n - 1, pltpu.roll(p, n - 1, 0), 0.0)
        z = (cw_ref[0:1, c0:c0 + chunk] * p_prev + cw_ref[1:2, c0:c0 + chunk] * p
             + cw_ref[2:3, c0:c0 + chunk] * p_next + cb_ref[:, c0:c0 + chunk])
        u_scr[:, c0:c0 + chunk] = (_silu(g) * bg * z).astype(BF16)
    y = _mm(u_scr[...], w_out_ref[...])
    out_ref[0] = _residual_ln(x, y, gate, lng_ref[0], lnb_ref[0])


def _sconv_layer(x, mod, layer, ln_g, ln_b, w_in, conv_w, conv_b, w_out, *, row_of_batch):
    bsz, n, d = x.shape
    width = w_out.shape[0]
    kernel = functools.partial(_sconv_kernel, row_of_batch=row_of_batch)
    return pl.pallas_call(
        kernel,
        grid=(bsz,),
        in_specs=[
            _seq_spec(n, d),
            pl.BlockSpec((1, 3, SUBLANES, d), lambda b: (layer, 0, 0, 0)),
            pl.BlockSpec((1, 1, d), lambda b: (layer, 0, 0)),
            pl.BlockSpec((1, 1, d), lambda b: (layer, 0, 0)),
            _const_spec(w_in.shape),
            _const_spec(conv_w.shape),
            _const_spec(conv_b.shape),
            _const_spec(w_out.shape),
        ],
        out_specs=_seq_spec(n, d),
        out_shape=jax.ShapeDtypeStruct((bsz, n, d), F32),
        scratch_shapes=[pltpu.VMEM((n, width), BF16)],
        compiler_params=_compiler_params(1),
        name="sconv_layer",
    )(x, mod, ln_g, ln_b, w_in, conv_w, conv_b, w_out)


def _chunk_cumsum(x, reverse):
    n = x.shape[0]
    pos = lax.broadcasted_iota(jnp.int32, (n, 1), 0) % HG_CHUNK
    s = 1
    while s < HG_CHUNK:
        if reverse:
            x = x + jnp.where(pos < HG_CHUNK - s, pltpu.roll(x, n - s, 0), 0.0)
        else:
            x = x + jnp.where(pos >= s, pltpu.roll(x, s, 0), 0.0)
        s *= 2
    return x


def _hgrn_kernel(*refs, row_of_batch, layer, has_s0, emit_state):
    it = iter(refs)
    x_ref, mod_ref, lng_ref, lnb_ref, w_in_ref, lbl_ref, ng_ref, w_out_ref = (next(it) for _ in range(8))
    s0_ref = next(it) if has_s0 else None
    out_ref = next(it)
    s_out_ref = next(it) if emit_state else None
    h_scr, qrel_scr, krel_scr, qp_scr, kp_scr, v_scr, dec_scr, o_scr, g_scr, u_scr = it

    n, d = x_ref.shape[1], x_ref.shape[2]
    hk = HG_HEADS * HG_KDIM
    n_tiles = n // ROW_TILE
    chunks_per_tile = ROW_TILE // HG_CHUNK
    n_chunks = n // HG_CHUNK
    pair = 2 * LANES

    row = pl.program_id(0) + 1 if row_of_batch else 0
    shift, scale, gate = _mod_rows(mod_ref, row)
    x = x_ref[0]
    h_scr[...] = (x * (1.0 + scale) + shift).astype(BF16)

    lbs = []
    for dr in range(2):
        z = lbl_ref[dr]
        e = jnp.exp(z - jnp.max(z, axis=0, keepdims=True))
        lbs.append(jnp.sum(e[:layer + 1], axis=0, keepdims=True) / jnp.sum(e, axis=0, keepdims=True))

    ri = lax.broadcasted_iota(jnp.int32, (ROW_TILE, ROW_TILE), 0)
    ci = lax.broadcasted_iota(jnp.int32, (ROW_TILE, ROW_TILE), 1)
    same_chunk = (ri // HG_CHUNK) == (ci // HG_CHUNK)
    masks = (same_chunk & (ci <= ri), same_chunk & (ci >= ri))

    row_chunk = (lax.broadcasted_iota(jnp.int32, (ROW_TILE, LANES), 0) // HG_CHUNK).astype(BF16)

    def chunk_blocks(a):
        zero = jnp.zeros_like(a)
        return jnp.concatenate([jnp.where(row_chunk == c, a, zero) for c in range(chunks_per_tile)], axis=1)

    def project(hp):
        c0 = hp * pair
        slot = hp % 2
        h = h_scr[...]
        q = _silu(_mm(h, w_in_ref[:, c0:c0 + pair])) * (HG_KDIM ** -0.5)
        v_scr[slot] = _mm(h, w_in_ref[:, 3 * hk + c0:3 * hk + c0 + pair]).astype(BF16)
        q3 = q.reshape(n_chunks, HG_CHUNK, pair)
        for dr in range(2):
            fl = _mm(h, w_in_ref[:, (1 + dr) * hk + c0:(1 + dr) * hk + c0 + pair])
            lb = lbs[dr][:, c0:c0 + pair]
            f = lb + (1.0 - lb) * jax.nn.sigmoid(fl)
            k3 = (1.0 - f).reshape(n_chunks, HG_CHUNK, pair)
            cum = _chunk_cumsum(jnp.log(f), reverse=(dr == 1)).reshape(n_chunks, HG_CHUNK, pair)
            mid = HG_CHUNK // 2 - 1
            if dr == 0:
                ref, tot = cum[:, mid:mid + 1], cum[:, HG_CHUNK - 1:HG_CHUNK]
            else:
                ref, tot = cum[:, HG_CHUNK - 1 - mid:HG_CHUNK - mid], cum[:, 0:1]
            q_rel = q3 * jnp.exp(cum - ref)
            k_rel = k3 * jnp.exp(ref - cum)
            qrel_scr[slot, dr] = q_rel.reshape(n, pair).astype(BF16)
            krel_scr[slot, dr] = k_rel.reshape(n, pair).astype(BF16)
            qp_scr[slot, dr] = (q_rel * jnp.exp(ref)).reshape(n, pair).astype(BF16)
            kp_scr[slot, dr] = (k_rel * jnp.exp(tot - ref)).reshape(n, pair).astype(BF16)
            dec_scr[slot, dr] = jnp.exp(tot)
        g_scr[slot] = _silu(_mm(h, w_in_ref[:, 4 * hk + c0:4 * hk + c0 + pair]))

    def recur(hp):
        c0 = hp * pair
        slot = hp % 2
        for dr in range(2):
            for h2 in range(2):
                l0 = h2 * LANES
                head = hp * 2 + h2
                if has_s0:
                    s_t0 = s0_ref[0, 0, dr, head].T
                else:
                    s_t0 = jnp.zeros((HG_VDIM, HG_KDIM), F32)

                def tile_body(i, s_t, dr=dr, l0=l0, slot=slot):
                    ti = i if dr == 0 else n_tiles - 1 - i
                    rows = pl.ds(pl.multiple_of(ti * ROW_TILE, ROW_TILE), ROW_TILE)
                    lanes = slice(l0, l0 + LANES)
                    v = v_scr[slot, rows, lanes]
                    att = _mm_nt(qrel_scr[slot, dr, rows, lanes], krel_scr[slot, dr, rows, lanes])
                    o_tile = _mm(jnp.where(masks[dr], att, 0.0), v)
                    upd = _mm_tn(v, chunk_blocks(kp_scr[slot, dr, rows, lanes]))
                    s_prev = [None] * chunks_per_tile
                    order = range(chunks_per_tile) if dr == 0 else range(chunks_per_tile - 1, -1, -1)
                    for c in order:
                        s_prev[c] = s_t.astype(BF16)
                        dec = dec_scr[slot, dr, ti * chunks_per_tile + c][:, lanes]
                        s_t = s_t * dec + upd[:, c * LANES:(c + 1) * LANES]
                    o_tile += _mm_nt(chunk_blocks(qp_scr[slot, dr, rows, lanes]),
                                     jnp.concatenate(s_prev, axis=1))
                    if dr == 0:
                        o_scr[slot, rows, lanes] = o_tile
                    else:
                        o_scr[slot, rows, lanes] += o_tile
                    return s_t

                s_t = s_t0
                for i in range(n_tiles):
                    s_t = tile_body(i, s_t)
                if emit_state:
                    s_out_ref[0, 0, dr, head] = s_t.T

        for h2 in range(2):
            lanes = slice(h2 * LANES, (h2 + 1) * LANES)
            o = o_scr[slot, :, lanes]
            ms = jnp.mean(o * o, axis=-1, keepdims=True)
            on = o * lax.rsqrt(ms + RMS_EPS) * ng_ref[:, c0 + h2 * LANES:c0 + (h2 + 1) * LANES]
            u_scr[:, c0 + h2 * LANES:c0 + (h2 + 1) * LANES] = (on * g_scr[slot, :, lanes]).astype(BF16)

    n_pairs = hk // pair
    project(0)
    for hp in range(n_pairs):
        if hp + 1 < n_pairs:
            project(hp + 1)
        recur(hp)

    y = _mm(u_scr[...], w_out_ref[...])
    out_ref[0] = _residual_ln(x, y, gate, lng_ref[0], lnb_ref[0])


def _hgrn_layer(x, mod, layer, ln_g, ln_b, w_in, lb_logits, norm_g, w_out, state, *, row_of_batch):
    bsz, n, d = x.shape
    hv = w_out.shape[0]
    has_s0 = state is not None
    emit_state = not has_s0
    pair = 2 * LANES
    state_block = (1, 1, 2, HG_HEADS, HG_KDIM, HG_VDIM)
    state_spec = pl.BlockSpec(state_block, lambda b: (b, 0, 0, 0, 0, 0))
    in_specs = [
        _seq_spec(n, d),
        pl.BlockSpec((1, 3, SUBLANES, d), lambda b: (layer, 0, 0, 0)),
        pl.BlockSpec((1, 1, d), lambda b: (layer, 0, 0)),
        pl.BlockSpec((1, 1, d), lambda b: (layer, 0, 0)),
        _const_spec(w_in.shape),
        _const_spec(lb_logits.shape),
        _const_spec(norm_g.shape),
        _const_spec(w_out.shape),
    ]
    args = [x, mod, ln_g, ln_b, w_in, lb_logits, norm_g, w_out]
    out_specs = [_seq_spec(n, d)]
    out_shape = [jax.ShapeDtypeStruct((bsz, n, d), F32)]
    if has_s0:
        in_specs.append(state_spec)
        args.append(state)
    if emit_state:
        out_specs.append(state_spec)
        out_shape.append(jax.ShapeDtypeStruct((bsz,) + state_block[1:], F32))
    kernel = functools.partial(_hgrn_kernel, row_of_batch=row_of_batch, layer=layer,
                               has_s0=has_s0, emit_state=emit_state)
    res = pl.pallas_call(
        kernel,
        grid=(bsz,),
        in_specs=in_specs,
        out_specs=out_specs,
        out_shape=out_shape,
        scratch_shapes=[
            pltpu.VMEM((n, d), BF16),
            pltpu.VMEM((2, 2, n, pair), BF16),
            pltpu.VMEM((2, 2, n, pair), BF16),
            pltpu.VMEM((2, 2, n, pair), BF16),
            pltpu.VMEM((2, 2, n, pair), BF16),
            pltpu.VMEM((2, n, pair), BF16),
            pltpu.VMEM((2, 2, n // HG_CHUNK, 1, pair), F32),
            pltpu.VMEM((2, n, pair), F32),
            pltpu.VMEM((2, n, pair), F32),
            pltpu.VMEM((n, hv), BF16),
        ],
        compiler_params=_compiler_params(1),
        name="hgrn2_layer",
    )(*args)
    return res[0] if has_s0 else tuple(res)


def _block_scan(a, x, reverse):
    n, c = a.shape
    nb = n // SUBLANES
    a = a.reshape(nb, SUBLANES, c)
    x = x.reshape(nb, SUBLANES, c)
    pos = lax.broadcasted_iota(jnp.int32, (1, SUBLANES, 1), 1)
    s = 1
    while s < SUBLANES:
        if reverse:
            keep = pos < SUBLANES - s
            shift = SUBLANES - s
        else:
            keep = pos >= s
            shift = s
        a_in = jnp.where(keep, pltpu.roll(a, shift, 1), 1.0)
        x_in = jnp.where(keep, pltpu.roll(x, shift, 1), 0.0)
        x = a * x_in + x
        a = a * a_in
        s *= 2
    return a.reshape(n, c), x.reshape(n, c)


def _rglru_kernel(*refs, row_of_batch, has_s0, emit_state):
    it = iter(refs)
    (x_ref, mod_ref, lng_ref, lnb_ref, w_in_ref, cw_ref, cb_ref, wg_ref, bg_ref, lam_ref,
     w_out_ref) = (next(it) for _ in range(11))
    s0_ref = next(it) if has_s0 else None
    out_ref = next(it)
    s_out_ref = next(it) if emit_state else None
    h_scr, a_scr, b_scr, gs_scr = it

    n, d = x_ref.shape[1], x_ref.shape[2]
    width = w_out_ref.shape[0]
    blk = width // RG_HEADS
    nb = n // SUBLANES

    row = pl.program_id(0) + 1 if row_of_batch else 0
    shift, scale, gate = _mod_rows(mod_ref, row)
    h_scr[...] = (x_ref[0] * (1.0 + scale) + shift).astype(BF16)

    z = -lam_ref[...]
    coef = -RG_C * (jnp.maximum(z, 0.0) + jnp.log1p(jnp.exp(-jnp.abs(z))))

    t = lax.broadcasted_iota(jnp.int32, (n, 1), 0)

    def project(hh):
        h = h_scr[...]
        return (_mm(h, w_in_ref[:, hh * blk:(hh + 1) * blk]),
                _mm(h, w_in_ref[:, width + hh * blk:width + (hh + 1) * blk]))

    def mix(hh, u_pre, g):
        cols = slice(hh * blk, (hh + 1) * blk)
        u = (cw_ref[0:1, cols] * jnp.where(t >= 2, pltpu.roll(u_pre, 2, 0), 0.0)
             + cw_ref[1:2, cols] * jnp.where(t >= 1, pltpu.roll(u_pre, 1, 0), 0.0)
             + cw_ref[2:3, cols] * u_pre
             + cw_ref[3:4, cols] * jnp.where(t < n - 1, pltpu.roll(u_pre, n - 1, 0), 0.0)
             + cb_ref[:, cols])
        gs_scr[:, cols] = _silu(g)
        gates = [jax.nn.sigmoid(_mm(u, wg_ref[dr, hh]) + bg_ref[dr, hh]) for dr in range(2)]
        for dr in range(2):
            a = jnp.exp(coef[dr:dr + 1, cols] * gates[dr][:, :blk])
            xin = jnp.exp(0.5 * jnp.log(1.0 - a * a)) * (gates[dr][:, blk:] * u)
            a_blk, h_blk = _block_scan(a, xin, reverse=(dr == 1))
            a_scr[dr, :, cols] = a_blk
            b_scr[dr, :, cols] = h_blk

    nxt = project(0)
    for hh in range(RG_HEADS):
        cur = nxt
        if hh + 1 < RG_HEADS:
            nxt = project(hh + 1)
        mix(hh, *cur)

    if has_s0:
        carry0 = (s0_ref[0, 0, 0:1, :], s0_ref[0, 0, 1:2, :])
    else:
        carry0 = (jnp.zeros((1, width), F32), jnp.zeros((1, width), F32))

    def carry_body(i, carry):
        cf, cb = carry
        rf = pl.ds(pl.multiple_of(i * SUBLANES, SUBLANES), SUBLANES)
        rb = pl.ds(pl.multiple_of((nb - 1 - i) * SUBLANES, SUBLANES), SUBLANES)
        hf = a_scr[0, rf, :] * cf + b_scr[0, rf, :]
        hb = a_scr[1, rb, :] * cb + b_scr[1, rb, :]
        b_scr[0, rf, :] = hf
        b_scr[1, rb, :] = hb
        return hf[SUBLANES - 1:SUBLANES, :], hb[0:1, :]

    cf, cb = lax.fori_loop(0, nb, carry_body, carry0)
    if emit_state:
        s_out_ref[0, 0, 0:1, :] = cf
        s_out_ref[0, 0, 1:2, :] = cb

    for r0 in range(0, n, ROW_TILE):
        rows = slice(r0, r0 + ROW_TILE)
        mix = ((b_scr[0, rows, :] + b_scr[1, rows, :]) * gs_scr[rows, :]).astype(BF16)
        y = _mm(mix, w_out_ref[...])
        out_ref[0, rows, :] = _residual_ln(x_ref[0, rows, :], y, gate, lng_ref[0], lnb_ref[0])


def _rglru_layer(x, mod, layer, ln_g, ln_b, w_in, conv_w, conv_b, w_gate, b_gate, lam, w_out, state,
                 *, row_of_batch):
    bsz, n, d = x.shape
    width = w_out.shape[0]
    has_s0 = state is not None
    emit_state = not has_s0
    b_gate = b_gate.reshape(2, RG_HEADS, 1, b_gate.shape[-1])
    state_block = (1, 1, 2, width)
    state_spec = pl.BlockSpec(state_block, lambda b: (b, 0, 0, 0))
    in_specs = [
        _seq_spec(n, d),
        pl.BlockSpec((1, 3, SUBLANES, d), lambda b: (layer, 0, 0, 0)),
        pl.BlockSpec((1, 1, d), lambda b: (layer, 0, 0)),
        pl.BlockSpec((1, 1, d), lambda b: (layer, 0, 0)),
        _const_spec(w_in.shape),
        _const_spec(conv_w.shape),
        _const_spec(conv_b.shape),
        _const_spec(w_gate.shape),
        _const_spec(b_gate.shape),
        _const_spec(lam.shape),
        _const_spec(w_out.shape),
    ]
    args = [x, mod, ln_g, ln_b, w_in, conv_w, conv_b, w_gate, b_gate, lam, w_out]
    out_specs = [_seq_spec(n, d)]
    out_shape = [jax.ShapeDtypeStruct((bsz, n, d), F32)]
    if has_s0:
        in_specs.append(state_spec)
        args.append(state)
    if emit_state:
        out_specs.append(state_spec)
        out_shape.append(jax.ShapeDtypeStruct((bsz,) + state_block[1:], F32))
    kernel = functools.partial(_rglru_kernel, row_of_batch=row_of_batch, has_s0=has_s0,
                               emit_state=emit_state)
    res = pl.pallas_call(
        kernel,
        grid=(bsz,),
        in_specs=in_specs,
        out_specs=out_specs,
        out_shape=out_shape,
        scratch_shapes=[
            pltpu.VMEM((n, d), BF16),
            pltpu.VMEM((2, n, width), F32),
            pltpu.VMEM((2, n, width), F32),
            pltpu.VMEM((n, width), F32),
        ],
        compiler_params=_compiler_params(1),
        name="rglru_layer",
    )(*args)
    return res[0] if has_s0 else tuple(res)


MLA_QK = MLA_NOPE + MLA_ROPE


def _mla_weights(w_in, w_qb, w_kvb, w_out):
    w_in, w_qb, w_kvb, w_out = (w.astype(BF16) for w in (w_in, w_qb, w_kvb, w_out))
    d = w_in.shape[0]
    a = MLA_Q_RANK + MLA_KV_RANK
    pad_l, pad_r = MLA_NOPE, MLA_HEAD_PAD - MLA_QK
    w_in_r = jnp.concatenate(
        [w_in[:, :a], jnp.zeros((d, pad_l), w_in.dtype), w_in[:, a:a + MLA_ROPE],
         jnp.zeros((d, pad_r), w_in.dtype), w_in[:, a + MLA_ROPE:]], axis=1)
    wq = w_qb.reshape(MLA_Q_RANK, MLA_HEADS, MLA_QK)
    wq = jnp.pad(wq, ((0, 0), (0, 0), (0, pad_r))).reshape(MLA_Q_RANK, MLA_HEADS * MLA_HEAD_PAD)
    wkv = w_kvb.reshape(MLA_KV_RANK, MLA_HEADS, MLA_NOPE + MLA_VDIM)
    wk = jnp.pad(wkv[:, :, :MLA_NOPE], ((0, 0), (0, 0), (0, MLA_HEAD_PAD - MLA_NOPE)))
    wk = wk.reshape(MLA_KV_RANK, MLA_HEADS * MLA_HEAD_PAD)
    wv = wkv[:, :, MLA_NOPE:].reshape(MLA_KV_RANK, MLA_HEADS * MLA_VDIM)
    return w_in_r, wq, wk, wv, w_out


def _rope_tables(n):
    half = MLA_ROPE // 2
    quarter = half // 2
    t = np.arange(n)
    inv_freq = ROPE_BASE ** (-np.arange(0, half, 2, dtype=np.float32) / half)
    cos = np.ones((n, MLA_HEAD_PAD), np.float32)
    sin = np.zeros((n, MLA_HEAD_PAD), np.float32)
    for k, pos in enumerate((t // GRID_W, t % GRID_W)):
        ang = pos[:, None].astype(np.float32) * inv_freq[None].astype(np.float32)
        c, s = np.cos(ang), np.sin(ang)
        base = MLA_NOPE + k * half
        cos[:, base:base + quarter] = c
        cos[:, base + quarter:base + half] = c
        sin[:, base:base + quarter] = -s
        sin[:, base + quarter:base + half] = s
    return jnp.asarray(cos), jnp.asarray(sin)


def _rope(x, cos, sin):
    quarter = MLA_ROPE // 4
    lane = lax.broadcasted_iota(jnp.int32, (1, MLA_HEAD_PAD), 1)
    first = ((lane - MLA_NOPE) % (2 * quarter)) < quarter
    partner = jnp.where(first, pltpu.roll(x, MLA_HEAD_PAD - quarter, 1), pltpu.roll(x, quarter, 1))
    return x * cos + partner * sin


def _rms(x, g):
    return x * lax.rsqrt(jnp.mean(x * x, axis=-1, keepdims=True) + RMS_EPS) * g


def _mla_kernel(*refs, row_of_batch, has_ctx):
    it = iter(refs)
    (x_ref, mod_ref, lng_ref, lnb_ref, w_in_ref, qn_ref, kvn_ref, wq_ref, wk_ref, wv_ref,
     w_out_ref) = (next(it) for _ in range(11))
    if has_ctx:
        ctx_ckv_ref, ctx_kpe_ref, cos_ref, sin_ref = (next(it) for _ in range(4))
    out_ref = next(it)
    if not has_ctx:
        ckv_out_ref, kpe_out_ref = next(it), next(it)
    h_scr, q_scr, k_scr, v_scr, o_scr = it

    n = x_ref.shape[1]
    n_ctx = k_scr.shape[0] - n
    hp = MLA_HEAD_PAD
    c_ckv = MLA_Q_RANK
    c_kpe = c_ckv + MLA_KV_RANK
    c_g = c_kpe + hp

    row = pl.program_id(0) + 1 if row_of_batch else 0
    shift, scale, gate = _mod_rows(mod_ref, row)
    sm_scale = MLA_QK ** -0.5
    n_tiles = n // ROW_TILE
    group = 2
    while group < MLA_HEADS and 2 * group * ROW_TILE * (n_ctx + n) <= MLA_GROUP_SCORES:
        group *= 2

    def store_keys(ckv, kpe, krows):
        k_all = _mm(ckv, wk_ref[...])
        v_scr[krows, :] = _mm(ckv, wv_ref[...]).astype(BF16)
        for hd in range(MLA_HEADS):
            cols = slice(hd * hp, (hd + 1) * hp)
            k_scr[krows, cols] = (k_all[:, cols] + kpe).astype(BF16)

    def proj_body(ti, carry):
        r0 = pl.multiple_of(ti * ROW_TILE, ROW_TILE)
        rows = pl.ds(r0, ROW_TILE)
        h = (x_ref[0, rows, :] * (1.0 + scale) + shift).astype(BF16)
        h_scr[rows, :] = h
        cq = _rms(_mm(h, w_in_ref[:, :c_ckv]), qn_ref[...])
        ckv = _rms(_mm(h, w_in_ref[:, c_ckv:c_kpe]), kvn_ref[...])
        kpe = _mm(h, w_in_ref[:, c_kpe:c_g])
        if has_ctx:
            cos, sin = cos_ref[rows, :], sin_ref[rows, :]
            kpe = _rope(kpe, cos, sin)
        else:
            ckv_out_ref[0, 0, rows, :] = ckv
            kpe_out_ref[0, 0, rows, :] = kpe[:, MLA_NOPE:MLA_QK]
        q_all = _mm(cq, wq_ref[...])
        for hd in range(MLA_HEADS):
            cols = slice(hd * hp, (hd + 1) * hp)
            qh = q_all[:, cols]
            if has_ctx:
                qh = _rope(qh, cos, sin)
            q_scr[rows, cols] = (qh * sm_scale).astype(BF16)
        store_keys(ckv, kpe, pl.ds(pl.multiple_of(n_ctx + r0, ROW_TILE), ROW_TILE))
        return carry

    def ctx_body(ti, carry):
        rows = pl.ds(pl.multiple_of(ti * ROW_TILE, ROW_TILE), ROW_TILE)
        store_keys(ctx_ckv_ref[0, 0, rows, :], ctx_kpe_ref[0, 0, rows, :], rows)
        return carry

    if n_tiles == 1:
        proj_body(0, 0)
    else:
        lax.fori_loop(0, n_tiles, proj_body, 0)
    if has_ctx:
        lax.fori_loop(0, n_ctx // ROW_TILE, ctx_body, 0)

    def tile_body(ti, carry):
        rows = pl.ds(pl.multiple_of(ti * ROW_TILE, ROW_TILE), ROW_TILE)
        for g0 in range(0, MLA_HEADS, group):
            heads = range(g0, g0 + group)
            ss = [_mm_nt(q_scr[rows, hd * hp:(hd + 1) * hp], k_scr[:, hd * hp:(hd + 1) * hp]) for hd in heads]
            if group == MLA_HEADS:
                g = _silu(_mm(h_scr[rows, :], w_in_ref[:, c_g:]))
            ms = [jnp.max(s, axis=-1, keepdims=True) for s in ss]
            es = [jnp.exp(s - m) for s, m in zip(ss, ms)]
            ls = [jnp.sum(e, axis=-1, keepdims=True) for e in es]
            os_ = [_mm(e, v_scr[:, hd * MLA_VDIM:(hd + 1) * MLA_VDIM]) / l for e, l, hd in zip(es, ls, heads)]
            if group == MLA_HEADS:
                o = jnp.concatenate(os_, axis=-1)
            else:
                for j in range(0, group, 2):
                    lo = (g0 + j) * MLA_VDIM
                    o_scr[:, lo:lo + 2 * MLA_VDIM] = jnp.concatenate(os_[j:j + 2], axis=-1)
        if group != MLA_HEADS:
            g = _silu(_mm(h_scr[rows, :], w_in_ref[:, c_g:]))
            o = o_scr[...]
        y = _mm(o * g, w_out_ref[...])
        out_ref[0, rows, :] = _residual_ln(x_ref[0, rows, :], y, gate, lng_ref[0], lnb_ref[0])
        return carry

    if n_tiles == 1:
        tile_body(0, 0)
    else:
        lax.fori_loop(0, n_tiles, tile_body, 0)


def _mla_layer(x, mod, layer, ln_g, ln_b, weights, q_norm, kv_norm, ctx_ckv, ctx_kpe, *, row_of_batch):
    bsz, n, d = x.shape
    w_in, wq, wk, wv, w_out = weights
    has_ctx = ctx_ckv is not None
    in_specs = [
        _seq_spec(n, d),
        pl.BlockSpec((1, 3, SUBLANES, d), lambda b: (layer, 0, 0, 0)),
        pl.BlockSpec((1, 1, d), lambda b: (layer, 0, 0)),
        pl.BlockSpec((1, 1, d), lambda b: (layer, 0, 0)),
        _const_spec(w_in.shape),
        _const_spec(q_norm.shape),
        _const_spec(kv_norm.shape),
        _const_spec(wq.shape),
        _const_spec(wk.shape),
        _const_spec(wv.shape),
        _const_spec(w_out.shape),
    ]
    args = [x, mod, ln_g, ln_b, w_in, q_norm, kv_norm, wq, wk, wv, w_out]
    out_specs = [_seq_spec(n, d, single_buffer=has_ctx)]
    out_shape = [jax.ShapeDtypeStruct((bsz, n, d), F32)]
    n_ctx = 0
    if has_ctx:
        n_ctx = ctx_ckv.shape[2]
        ctx_kpe = jnp.pad(ctx_kpe, ((0, 0), (0, 0), (0, 0), (MLA_NOPE, MLA_HEAD_PAD - MLA_QK)))
        cos, sin = _rope_tables(n)
        in_specs += [
            pl.BlockSpec((1, 1, n_ctx, MLA_KV_RANK), lambda b: (b, 0, 0, 0)),
            pl.BlockSpec((1, 1, n_ctx, MLA_HEAD_PAD), lambda b: (b, 0, 0, 0)),
            _const_spec(cos.shape),
            _const_spec(sin.shape),
        ]
        args += [ctx_ckv, ctx_kpe, cos, sin]
    else:
        out_specs += [pl.BlockSpec((1, 1, n, MLA_KV_RANK), lambda b: (b, 0, 0, 0)),
                      pl.BlockSpec((1, 1, n, MLA_ROPE), lambda b: (b, 0, 0, 0))]
        out_shape += [jax.ShapeDtypeStruct((bsz, 1, n, MLA_KV_RANK), F32),
                      jax.ShapeDtypeStruct((bsz, 1, n, MLA_ROPE), F32)]
    kernel = functools.partial(_mla_kernel, row_of_batch=row_of_batch, has_ctx=has_ctx)
    res = pl.pallas_call(
        kernel,
        grid=(bsz,),
        in_specs=in_specs,
        out_specs=out_specs,
        out_shape=out_shape,
        scratch_shapes=[
            pltpu.VMEM((n, d), BF16),
            pltpu.VMEM((n, MLA_HEADS * MLA_HEAD_PAD), BF16),
            pltpu.VMEM((n_ctx + n, MLA_HEADS * MLA_HEAD_PAD), BF16),
            pltpu.VMEM((n_ctx + n, MLA_HEADS * MLA_VDIM), BF16),
            pltpu.VMEM((ROW_TILE, MLA_HEADS * MLA_VDIM), F32),
        ],
        compiler_params=_compiler_params(1),
        name="mla_layer",
    )(*args)
    return res[0] if has_ctx else tuple(res)


def kernel(x_prompt, x_sample, state_hgrn, state_rglru, cache_mla_ckv, cache_mla_kpe, c, c_ctx, ada_w, ada_b, ln_g, ln_b, hg_w_in, hg_lb_logits, hg_norm_g, hg_w_out, sc_w_in, sc_conv_w, sc_conv_b, sc_w_out, rg_w_in, rg_conv_w, rg_conv_b, rg_w_gate, rg_b_gate, rg_lambda, rg_w_out, mla_w_in, mla_q_norm, mla_kv_norm, mla_w_qb, mla_w_kvb, mla_w_out):
    d = x_prompt.shape[-1]
    n_dec = c.shape[0]
    cond = jnp.concatenate([c_ctx[None], c, jnp.zeros((SUBLANES - 1 - n_dec, d), F32)], axis=0)
    mod = _modulation(cond, ada_w, ada_b)
    lng = ln_g.reshape(DEPTH, 1, d)
    lnb = ln_b.reshape(DEPTH, 1, d)
    hg_wi, hg_wo = hg_w_in[0].astype(BF16), hg_w_out[0].astype(BF16)
    sc_wi, sc_wo = sc_w_in[0].astype(BF16), sc_w_out[0].astype(BF16)
    rg_wi, rg_wg, rg_wo = rg_w_in[0].astype(BF16), rg_w_gate[0].astype(BF16), rg_w_out[0].astype(BF16)
    mla_w = _mla_weights(mla_w_in[0], mla_w_qb[0], mla_w_kvb[0], mla_w_out[0])

    def run(x, rob, st_hg, st_rg, ctx_ckv, ctx_kpe):
        r0 = _hgrn_layer(x, mod, 0, lng, lnb, hg_wi, hg_lb_logits, hg_norm_g, hg_wo, st_hg, row_of_batch=rob)
        x, new_hg = r0 if st_hg is None else (r0, None)
        x = _sconv_layer(x, mod, 1, lng, lnb, sc_wi, sc_conv_w[0], sc_conv_b, sc_wo, row_of_batch=rob)
        r2 = _rglru_layer(x, mod, 2, lng, lnb, rg_wi, rg_conv_w[0], rg_conv_b, rg_wg, rg_b_gate[0],
                          rg_lambda[0], rg_wo, st_rg, row_of_batch=rob)
        x, new_rg = r2 if st_rg is None else (r2, None)
        r3 = _mla_layer(x, mod, 3, lng, lnb, mla_w, mla_q_norm, mla_kv_norm, ctx_ckv, ctx_kpe,
                        row_of_batch=rob)
        if ctx_ckv is None:
            return (r3[0], new_hg, new_rg, r3[1], r3[2])
        return (r3,)

    y_prompt, new_hg, new_rg, new_ckv, new_kpe = run(x_prompt, False, None, None, None, None)
    (y_sample,) = run(x_sample, True, state_hgrn, state_rglru, cache_mla_ckv, cache_mla_kpe)
    return (y_prompt, y_sample, new_hg, new_rg, new_ckv, new_kpe)
```

```python
import functools

import jax
import jax.numpy as jnp
import numpy as np
from jax import lax
from jax.experimental import pallas as pl
from jax.experimental.pallas import tpu as pltpu

F32 = jnp.float32
BF16 = jnp.bfloat16

LANES = 128
SUBLANES = 8
VMEM_LIMIT_BYTES = 60 * 1024 * 1024

DEPTH = 4
DEEPNORM_ALPHA = (2 * DEPTH) ** 0.25
LN_EPS = 1e-5
RMS_EPS = 1e-6

ROW_TILE = 256

HG_HEADS = 8
HG_KDIM = 128
HG_VDIM = 128
HG_CHUNK = 32
SC_KERNEL = 3
RG_HEADS = 4
RG_KERNEL = 4
RG_C = 8.0
MLA_HEADS = 16
MLA_Q_RANK = 384
MLA_KV_RANK = 256
MLA_NOPE = 64
MLA_ROPE = 32
MLA_VDIM = 64
MLA_HEAD_PAD = 128
MLA_GROUP_SCORES = 1 << 21
ROPE_BASE = 10000.0
GRID_W = 64


def _mm(a, b):
    return jnp.dot(a.astype(BF16), b.astype(BF16), preferred_element_type=F32)


def _mm_nt(a, b):
    return lax.dot_general(a.astype(BF16), b.astype(BF16), (((1,), (1,)), ((), ())),
                           preferred_element_type=F32)


def _mm_tn(a, b):
    return lax.dot_general(a.astype(BF16), b.astype(BF16), (((0,), (0,)), ((), ())),
                           preferred_element_type=F32)


def _silu(x):
    return x * jax.nn.sigmoid(x)


def _mod_rows(mod_ref, row):
    if isinstance(row, int):
        return tuple(mod_ref[0, j, row:row + 1, :] for j in range(3))
    rows = lax.broadcasted_iota(jnp.int32, (SUBLANES, 1), 0)
    return tuple(jnp.sum(jnp.where(rows == row, mod_ref[0, j], 0.0), axis=0, keepdims=True)
                 for j in range(3))


def _residual_ln(x, y, gate, g, b):
    z = DEEPNORM_ALPHA * x + gate * y
    mu = jnp.mean(z, axis=-1, keepdims=True)
    zc = z - mu
    var = jnp.mean(zc * zc, axis=-1, keepdims=True)
    return zc * lax.rsqrt(var + LN_EPS) * g + b


def _const_spec(shape):
    nd = len(shape)
    return pl.BlockSpec(shape, lambda *_: (0,) * nd, pipeline_mode=pl.Buffered(1))


def _seq_spec(n, d, single_buffer=False):
    mode = dict(pipeline_mode=pl.Buffered(1)) if single_buffer else {}
    return pl.BlockSpec((1, n, d), lambda b: (b, 0, 0), **mode)


def _compiler_params(n_grid):
    return pltpu.CompilerParams(dimension_semantics=("arbitrary",) * n_grid,
                                vmem_limit_bytes=VMEM_LIMIT_BYTES)


def _mod_kernel(cond_ref, w_ref, b_ref, out_ref):
    s = _silu(cond_ref[...])
    out_ref[0, 0] = _mm(s, w_ref[0]) + b_ref[0, 0]


def _modulation(cond, ada_w, ada_b):
    n_layers, d, _ = ada_w.shape
    rows = cond.shape[0]
    return pl.pallas_call(
        _mod_kernel,
        grid=(n_layers, 3),
        in_specs=[
            pl.BlockSpec((rows, d), lambda l, j: (0, 0)),
            pl.BlockSpec((1, d, d), lambda l, j: (l, 0, j)),
            pl.BlockSpec((1, 1, 1, d), lambda l, j: (l, j, 0, 0)),
        ],
        out_specs=pl.BlockSpec((1, 1, rows, d), lambda l, j: (l, j, 0, 0)),
        out_shape=jax.ShapeDtypeStruct((n_layers, 3, rows, d), F32),
        compiler_params=_compiler_params(2),
        name="adaln_modulation",
    )(cond, ada_w, ada_b.reshape(n_layers, 3, 1, d))


def _sconv_kernel(x_ref, mod_ref, lng_ref, lnb_ref, w_in_ref, cw_ref, cb_ref, w_out_ref,
                  out_ref, u_scr, *, row_of_batch):
    n, d = x_ref.shape[1], x_ref.shape[2]
    width = w_out_ref.shape[0]
    row = pl.program_id(0) + 1 if row_of_batch else 0
    shift, scale, gate = _mod_rows(mod_ref, row)
    x = x_ref[0]
    h = (x * (1.0 + scale) + shift).astype(BF16)
    t = lax.broadcasted_iota(jnp.int32, (n, 1), 0)
    chunk = 2 * LANES
    def project(j):
        return tuple(_mm(h, w_in_ref[:, i * width + j * chunk:i * width + (j + 1) * chunk]) for i in range(4))

    nxt = project(0)
    for j in range(width // chunk):
        c0 = j * chunk
        bg, cg, v, g = nxt
        if j + 1 < width // chunk:
            nxt = project(j + 1)
        p = cg * v
        p_prev = jnp.where(t >= 1, pltpu.roll(p, 1, 0), 0.0)
        p_next = jnp.where(t < n - 1, pltpu.roll(p, n - 1, 0), 0.0)
        z = (cw_ref[0:1, c0:c0 + chunk] * p_prev + cw_ref[1:2, c0:c0 + chunk] * p
             + cw_ref[2:3, c0:c0 + chunk] * p_next + cb_ref[:, c0:c0 + chunk])
        u_scr[:, c0:c0 + chunk] = (_silu(g) * bg * z).astype(BF16)
    y = _mm(u_scr[...], w_out_ref[...])
    out_ref[0] = _residual_ln(x, y, gate, lng_ref[0], lnb_ref[0])


def _sconv_layer(x, mod, layer, ln_g, ln_b, w_in, conv_w, conv_b, w_out, *, row_of_batch):
    bsz, n, d = x.shape
    width = w_out.shape[0]
    kernel = functools.partial(_sconv_kernel, row_of_batch=row_of_batch)
    return pl.pallas_call(
        kernel,
        grid=(bsz,),
        in_specs=[
            _seq_spec(n, d),
            pl.BlockSpec((1, 3, SUBLANES, d), lambda b: (layer, 0, 0, 0)),
            pl.BlockSpec((1, 1, d), lambda b: (layer, 0, 0)),
            pl.BlockSpec((1, 1, d), lambda b: (layer, 0, 0)),
            _const_spec(w_in.shape),
            _const_spec(conv_w.shape),
            _const_spec(conv_b.shape),
            _const_spec(w_out.shape),
        ],
        out_specs=_seq_spec(n, d),
        out_shape=jax.ShapeDtypeStruct((bsz, n, d), F32),
        scratch_shapes=[pltpu.VMEM((n, width), BF16)],
        compiler_params=_compiler_params(1),
        name="sconv_layer",
    )(x, mod, ln_g, ln_b, w_in, conv_w, conv_b, w_out)


def _chunk_cumsum(x, reverse):
    n = x.shape[0]
    pos = lax.broadcasted_iota(jnp.int32, (n, 1), 0) % HG_CHUNK
    s = 1
    while s < HG_CHUNK:
        if reverse:
            x = x + jnp.where(pos < HG_CHUNK - s, pltpu.roll(x, n - s, 0), 0.0)
        else:
            x = x + jnp.where(pos >= s, pltpu.roll(x, s, 0), 0.0)
        s *= 2
    return x


def _hgrn_kernel(*refs, row_of_batch, layer, has_s0, emit_state):
    it = iter(refs)
    x_ref, mod_ref, lng_ref, lnb_ref, w_in_ref, lbl_ref, ng_ref, w_out_ref = (next(it) for _ in range(8))
    s0_ref = next(it) if has_s0 else None
    out_ref = next(it)
    s_out_ref = next(it) if emit_state else None
    h_scr, qrel_scr, krel_scr, qp_scr, kp_scr, v_scr, dec_scr, o_scr, g_scr, u_scr = it

    n, d = x_ref.shape[1], x_ref.shape[2]
    hk = HG_HEADS * HG_KDIM
    n_tiles = n // ROW_TILE
    chunks_per_tile = ROW_TILE // HG_CHUNK
    n_chunks = n // HG_CHUNK
    pair = 2 * LANES

    row = pl.program_id(0) + 1 if row_of_batch else 0
    shift, scale, gate = _mod_rows(mod_ref, row)
    x = x_ref[0]
    h_scr[...] = (x * (1.0 + scale) + shift).astype(BF16)

    lbs = []
    for dr in range(2):
        z = lbl_ref[dr]
        e = jnp.exp(z - jnp.max(z, axis=0, keepdims=True))
        lbs.append(jnp.sum(e[:layer + 1], axis=0, keepdims=True) / jnp.sum(e, axis=0, keepdims=True))

    ri = lax.broadcasted_iota(jnp.int32, (ROW_TILE, ROW_TILE), 0)
    ci = lax.broadcasted_iota(jnp.int32, (ROW_TILE, ROW_TILE), 1)
    same_chunk = (ri // HG_CHUNK) == (ci // HG_CHUNK)
    masks = (same_chunk & (ci <= ri), same_chunk & (ci >= ri))

    row_chunk = (lax.broadcasted_iota(jnp.int32, (ROW_TILE, LANES), 0) // HG_CHUNK).astype(BF16)

    def chunk_blocks(a):
        zero = jnp.zeros_like(a)
        return jnp.concatenate([jnp.where(row_chunk == c, a, zero) for c in range(chunks_per_tile)], axis=1)

    def project(hp):
        c0 = hp * pair
        slot = hp % 2
        h = h_scr[...]
        q = _silu(_mm(h, w_in_ref[:, c0:c0 + pair])) * (HG_KDIM ** -0.5)
        v_scr[slot] = _mm(h, w_in_ref[:, 3 * hk + c0:3 * hk + c0 + pair]).astype(BF16)
        q3 = q.reshape(n_chunks, HG_CHUNK, pair)
        for dr in range(2):
            fl = _mm(h, w_in_ref[:, (1 + dr) * hk + c0:(1 + dr) * hk + c0 + pair])
            lb = lbs[dr][:, c0:c0 + pair]
            f = lb + (1.0 - lb) * jax.nn.sigmoid(fl)
            k3 = (1.0 - f).reshape(n_chunks, HG_CHUNK, pair)
            cum = _chunk_cumsum(jnp.log(f), reverse=(dr == 1)).reshape(n_chunks, HG_CHUNK, pair)
            mid = HG_CHUNK // 2 - 1
            if dr == 0:
                ref, tot = cum[:, mid:mid + 1], cum[:, HG_CHUNK - 1:HG_CHUNK]
            else:
                ref, tot = cum[:, HG_CHUNK - 1 - mid:HG_CHUNK - mid], cum[:, 0:1]
            q_rel = q3 * jnp.exp(cum - ref)
            k_rel = k3 * jnp.exp(ref - cum)
            qrel_scr[slot, dr] = q_rel.reshape(n, pair).astype(BF16)
            krel_scr[slot, dr] = k_rel.reshape(n, pair).astype(BF16)
            qp_scr[slot, dr] = (q_rel * jnp.exp(ref)).reshape(n, pair).astype(BF16)
            kp_scr[slot, dr] = (k_rel * jnp.exp(tot - ref)).reshape(n, pair).astype(BF16)
            dec_scr[slot, dr] = jnp.exp(tot)
        g_scr[slot] = _silu(_mm(h, w_in_ref[:, 4 * hk + c0:4 * hk + c0 + pair]))

    def recur(hp):
        c0 = hp * pair
        slot = hp % 2
        for dr in range(2):
            for h2 in range(2):
                l0 = h2 * LANES
                head = hp * 2 + h2
                if has_s0:
                    s_t0 = s0_ref[0, 0, dr, head].T
                else:
                    s_t0 = jnp.zeros((HG_VDIM, HG_KDIM), F32)

                def tile_body(i, s_t, dr=dr, l0=l0, slot=slot):
                    ti = i if dr == 0 else n_tiles - 1 - i
                    rows = pl.ds(pl.multiple_of(ti * ROW_TILE, ROW_TILE), ROW_TILE)
                    lanes = slice(l0, l0 + LANES)
                    v = v_scr[slot, rows, lanes]
                    att = _mm_nt(qrel_scr[slot, dr, rows, lanes], krel_scr[slot, dr, rows, lanes])
                    o_tile = _mm(jnp.where(masks[dr], att, 0.0), v)
                    upd = _mm_tn(v, chunk_blocks(kp_scr[slot, dr, rows, lanes]))
                    s_prev = [None] * chunks_per_tile
                    order = range(chunks_per_tile) if dr == 0 else range(chunks_per_tile - 1, -1, -1)
                    for c in order:
                        s_prev[c] = s_t.astype(BF16)
                        dec = dec_scr[slot, dr, ti * chunks_per_tile + c][:, lanes]
                        s_t = s_t * dec + upd[:, c * LANES:(c + 1) * LANES]
                    o_tile += _mm_nt(chunk_blocks(qp_scr[slot, dr, rows, lanes]),
                                     jnp.concatenate(s_prev, axis=1))
                    if dr == 0:
                        o_scr[slot, rows, lanes] = o_tile
                    else:
                        o_scr[slot, rows, lanes] += o_tile
                    return s_t

                s_t = s_t0
                for i in range(n_tiles):
                    s_t = tile_body(i, s_t)
                if emit_state:
                    s_out_ref[0, 0, dr, head] = s_t.T

        for h2 in range(2):
            lanes = slice(h2 * LANES, (h2 + 1) * LANES)
            o = o_scr[slot, :, lanes]
            ms = jnp.mean(o * o, axis=-1, keepdims=True)
            on = o * lax.rsqrt(ms + RMS_EPS) * ng_ref[:, c0 + h2 * LANES:c0 + (h2 + 1) * LANES]
            u_scr[:, c0 + h2 * LANES:c0 + (h2 + 1) * LANES] = (on * g_scr[slot, :, lanes]).astype(BF16)

    n_pairs = hk // pair
    project(0)
    for hp in range(n_pairs):
        if hp + 1 < n_pairs:
            project(hp + 1)
        recur(hp)

    y = _mm(u_scr[...], w_out_ref[...])
    out_ref[0] = _residual_ln(x, y, gate, lng_ref[0], lnb_ref[0])


def _hgrn_layer(x, mod, layer, ln_g, ln_b, w_in, lb_logits, norm_g, w_out, state, *, row_of_batch):
    bsz, n, d = x.shape
    hv = w_out.shape[0]
    has_s0 = state is not None
    emit_state = not has_s0
    pair = 2 * LANES
    state_block = (1, 1, 2, HG_HEADS, HG_KDIM, HG_VDIM)
    state_spec = pl.BlockSpec(state_block, lambda b: (b, 0, 0, 0, 0, 0))
    in_specs = [
        _seq_spec(n, d),
        pl.BlockSpec((1, 3, SUBLANES, d), lambda b: (layer, 0, 0, 0)),
        pl.BlockSpec((1, 1, d), lambda b: (layer, 0, 0)),
        pl.BlockSpec((1, 1, d), lambda b: (layer, 0, 0)),
        _const_spec(w_in.shape),
        _const_spec(lb_logits.shape),
        _const_spec(norm_g.shape),
        _const_spec(w_out.shape),
    ]
    args = [x, mod, ln_g, ln_b, w_in, lb_logits, norm_g, w_out]
    out_specs = [_seq_spec(n, d)]
    out_shape = [jax.ShapeDtypeStruct((bsz, n, d), F32)]
    if has_s0:
        in_specs.append(state_spec)
        args.append(state)
    if emit_state:
        out_specs.append(state_spec)
        out_shape.append(jax.ShapeDtypeStruct((bsz,) + state_block[1:], F32))
    kernel = functools.partial(_hgrn_kernel, row_of_batch=row_of_batch, layer=layer,
                               has_s0=has_s0, emit_state=emit_state)
    res = pl.pallas_call(
        kernel,
        grid=(bsz,),
        in_specs=in_specs,
        out_specs=out_specs,
        out_shape=out_shape,
        scratch_shapes=[
            pltpu.VMEM((n, d), BF16),
            pltpu.VMEM((2, 2, n, pair), BF16),
            pltpu.VMEM((2, 2, n, pair), BF16),
            pltpu.VMEM((2, 2, n, pair), BF16),
            pltpu.VMEM((2, 2, n, pair), BF16),
            pltpu.VMEM((2, n, pair), BF16),
            pltpu.VMEM((2, 2, n // HG_CHUNK, 1, pair), F32),
            pltpu.VMEM((2, n, pair), F32),
            pltpu.VMEM((2, n, pair), F32),
            pltpu.VMEM((n, hv), BF16),
        ],
        compiler_params=_compiler_params(1),
        name="hgrn2_layer",
    )(*args)
    return res[0] if has_s0 else tuple(res)


def _block_scan(a, x, reverse):
    n, c = a.shape
    nb = n // SUBLANES
    a = a.reshape(nb, SUBLANES, c)
    x = x.reshape(nb, SUBLANES, c)
    pos = lax.broadcasted_iota(jnp.int32, (1, SUBLANES, 1), 1)
    s = 1
    while s < SUBLANES:
        if reverse:
            keep = pos < SUBLANES - s
            shift = SUBLANES - s
        else:
            keep = pos >= s
            shift = s
        a_in = jnp.where(keep, pltpu.roll(a, shift, 1), 1.0)
        x_in = jnp.where(keep, pltpu.roll(x, shift, 1), 0.0)
        x = a * x_in + x
        a = a * a_in
        s *= 2
    return a.reshape(n, c), x.reshape(n, c)


def _rglru_kernel(*refs, row_of_batch, has_s0, emit_state):
    it = iter(refs)
    (x_ref, mod_ref, lng_ref, lnb_ref, w_in_ref, cw_ref, cb_ref, wg_ref, bg_ref, lam_ref,
     w_out_ref) = (next(it) for _ in range(11))
    s0_ref = next(it) if has_s0 else None
    out_ref = next(it)
    s_out_ref = next(it) if emit_state else None
    h_scr, a_scr, b_scr, gs_scr = it

    n, d = x_ref.shape[1], x_ref.shape[2]
    width = w_out_ref.shape[0]
    blk = width // RG_HEADS
    nb = n // SUBLANES

    row = pl.program_id(0) + 1 if row_of_batch else 0
    shift, scale, gate = _mod_rows(mod_ref, row)
    h_scr[...] = (x_ref[0] * (1.0 + scale) + shift).astype(BF16)

    z = -lam_ref[...]
    coef = -RG_C * (jnp.maximum(z, 0.0) + jnp.log1p(jnp.exp(-jnp.abs(z))))

    t = lax.broadcasted_iota(jnp.int32, (n, 1), 0)

    def project(hh):
        h = h_scr[...]
        return (_mm(h, w_in_ref[:, hh * blk:(hh + 1) * blk]),
                _mm(h, w_in_ref[:, width + hh * blk:width + (hh + 1) * blk]))

    def mix(hh, u_pre, g):
        cols = slice(hh * blk, (hh + 1) * blk)
        u = (cw_ref[0:1, cols] * jnp.where(t >= 2, pltpu.roll(u_pre, 2, 0), 0.0)
             + cw_ref[1:2, cols] * jnp.where(t >= 1, pltpu.roll(u_pre, 1, 0), 0.0)
             + cw_ref[2:3, cols] * u_pre
             + cw_ref[3:4, cols] * jnp.where(t < n - 1, pltpu.roll(u_pre, n - 1, 0), 0.0)
             + cb_ref[:, cols])
        gs_scr[:, cols] = _silu(g)
        gates = [jax.nn.sigmoid(_mm(u, wg_ref[dr, hh]) + bg_ref[dr, hh]) for dr in range(2)]
        for dr in range(2):
            a = jnp.exp(coef[dr:dr + 1, cols] * gates[dr][:, :blk])
            xin = jnp.exp(0.5 * jnp.log(1.0 - a * a)) * (gates[dr][:, blk:] * u)
            a_blk, h_blk = _block_scan(a, xin, reverse=(dr == 1))
            a_scr[dr, :, cols] = a_blk
            b_scr[dr, :, cols] = h_blk

    nxt = project(0)
    for hh in range(RG_HEADS):
        cur = nxt
        if hh + 1 < RG_HEADS:
            nxt = project(hh + 1)
        mix(hh, *cur)

    if has_s0:
        carry0 = (s0_ref[0, 0, 0:1, :], s0_ref[0, 0, 1:2, :])
    else:
        carry0 = (jnp.zeros((1, width), F32), jnp.zeros((1, width), F32))

    def carry_body(i, carry):
        cf, cb = carry
        rf = pl.ds(pl.multiple_of(i * SUBLANES, SUBLANES), SUBLANES)
        rb = pl.ds(pl.multiple_of((nb - 1 - i) * SUBLANES, SUBLANES), SUBLANES)
        hf = a_scr[0, rf, :] * cf + b_scr[0, rf, :]
        hb = a_scr[1, rb, :] * cb + b_scr[1, rb, :]
        b_scr[0, rf, :] = hf
        b_scr[1, rb, :] = hb
        return hf[SUBLANES - 1:SUBLANES, :], hb[0:1, :]

    cf, cb = carry0
    for i in range(nb):
        cf, cb = carry_body(i, (cf, cb))
    if emit_state:
        s_out_ref[0, 0, 0:1, :] = cf
        s_out_ref[0, 0, 1:2, :] = cb

    for r0 in range(0, n, ROW_TILE):
        rows = slice(r0, r0 + ROW_TILE)
        mix = ((b_scr[0, rows, :] + b_scr[1, rows, :]) * gs_scr[rows, :]).astype(BF16)
        y = _mm(mix, w_out_ref[...])
        out_ref[0, rows, :] = _residual_ln(x_ref[0, rows, :], y, gate, lng_ref[0], lnb_ref[0])


def _rglru_layer(x, mod, layer, ln_g, ln_b, w_in, conv_w, conv_b, w_gate, b_gate, lam, w_out, state,
                 *, row_of_batch):
    bsz, n, d = x.shape
    width = w_out.shape[0]
    has_s0 = state is not None
    emit_state = not has_s0
    b_gate = b_gate.reshape(2, RG_HEADS, 1, b_gate.shape[-1])
    state_block = (1, 1, 2, width)
    state_spec = pl.BlockSpec(state_block, lambda b: (b, 0, 0, 0))
    in_specs = [
        _seq_spec(n, d),
        pl.BlockSpec((1, 3, SUBLANES, d), lambda b: (layer, 0, 0, 0)),
        pl.BlockSpec((1, 1, d), lambda b: (layer, 0, 0)),
        pl.BlockSpec((1, 1, d), lambda b: (layer, 0, 0)),
        _const_spec(w_in.shape),
        _const_spec(conv_w.shape),
        _const_spec(conv_b.shape),
        _const_spec(w_gate.shape),
        _const_spec(b_gate.shape),
        _const_spec(lam.shape),
        _const_spec(w_out.shape),
    ]
    args = [x, mod, ln_g, ln_b, w_in, conv_w, conv_b, w_gate, b_gate, lam, w_out]
    out_specs = [_seq_spec(n, d)]
    out_shape = [jax.ShapeDtypeStruct((bsz, n, d), F32)]
    if has_s0:
        in_specs.append(state_spec)
        args.append(state)
    if emit_state:
        out_specs.append(state_spec)
        out_shape.append(jax.ShapeDtypeStruct((bsz,) + state_block[1:], F32))
    kernel = functools.partial(_rglru_kernel, row_of_batch=row_of_batch, has_s0=has_s0,
                               emit_state=emit_state)
    res = pl.pallas_call(
        kernel,
        grid=(bsz,),
        in_specs=in_specs,
        out_specs=out_specs,
        out_shape=out_shape,
        scratch_shapes=[
            pltpu.VMEM((n, d), BF16),
            pltpu.VMEM((2, n, width), F32),
            pltpu.VMEM((2, n, width), F32),
            pltpu.VMEM((n, width), F32),
        ],
        compiler_params=_compiler_params(1),
        name="rglru_layer",
    )(*args)
    return res[0] if has_s0 else tuple(res)


MLA_QK = MLA_NOPE + MLA_ROPE


def _mla_weights(w_in, w_qb, w_kvb, w_out):
    w_in, w_qb, w_kvb, w_out = (w.astype(BF16) for w in (w_in, w_qb, w_kvb, w_out))
    d = w_in.shape[0]
    a = MLA_Q_RANK + MLA_KV_RANK
    pad_l, pad_r = MLA_NOPE, MLA_HEAD_PAD - MLA_QK
    w_in_r = jnp.concatenate(
        [w_in[:, :a], jnp.zeros((d, pad_l), w_in.dtype), w_in[:, a:a + MLA_ROPE],
         jnp.zeros((d, pad_r), w_in.dtype), w_in[:, a + MLA_ROPE:]], axis=1)
    wq = w_qb.reshape(MLA_Q_RANK, MLA_HEADS, MLA_QK)
    wq = jnp.pad(wq, ((0, 0), (0, 0), (0, pad_r))).reshape(MLA_Q_RANK, MLA_HEADS * MLA_HEAD_PAD)
    wkv = w_kvb.reshape(MLA_KV_RANK, MLA_HEADS, MLA_NOPE + MLA_VDIM)
    wk = jnp.pad(wkv[:, :, :MLA_NOPE], ((0, 0), (0, 0), (0, MLA_HEAD_PAD - MLA_NOPE)))
    wk = wk.reshape(MLA_KV_RANK, MLA_HEADS * MLA_HEAD_PAD)
    wv_t = wkv[:, :, MLA_NOPE:].reshape(MLA_KV_RANK, MLA_HEADS * MLA_VDIM).T
    return w_in_r, wq, wk, wv_t, w_out


def _rope_tables(n):
    half = MLA_ROPE // 2
    quarter = half // 2
    t = np.arange(n)
    inv_freq = ROPE_BASE ** (-np.arange(0, half, 2, dtype=np.float32) / half)
    cos = np.ones((n, MLA_HEAD_PAD), np.float32)
    sin = np.zeros((n, MLA_HEAD_PAD), np.float32)
    for k, pos in enumerate((t // GRID_W, t % GRID_W)):
        ang = pos[:, None].astype(np.float32) * inv_freq[None].astype(np.float32)
        c, s = np.cos(ang), np.sin(ang)
        base = MLA_NOPE + k * half
        cos[:, base:base + quarter] = c
        cos[:, base + quarter:base + half] = c
        sin[:, base:base + quarter] = -s
        sin[:, base + quarter:base + half] = s
    return jnp.asarray(cos), jnp.asarray(sin)


def _rope(x, cos, sin):
    quarter = MLA_ROPE // 4
    lane = lax.broadcasted_iota(jnp.int32, (1, MLA_HEAD_PAD), 1)
    first = ((lane - MLA_NOPE) % (2 * quarter)) < quarter
    partner = jnp.where(first, pltpu.roll(x, MLA_HEAD_PAD - quarter, 1), pltpu.roll(x, quarter, 1))
    return x * cos + partner * sin


def _rms(x, g):
    return x * lax.rsqrt(jnp.mean(x * x, axis=-1, keepdims=True) + RMS_EPS) * g


def _mla_kernel(*refs, row_of_batch, has_ctx):
    it = iter(refs)
    (x_ref, mod_ref, lng_ref, lnb_ref, w_in_ref, qn_ref, kvn_ref, wq_ref, wk_ref, wv_ref,
     w_out_ref) = (next(it) for _ in range(11))
    if has_ctx:
        ctx_ckv_ref, ctx_kpe_ref, cos_ref, sin_ref = (next(it) for _ in range(4))
    out_ref = next(it)
    if not has_ctx:
        ckv_out_ref, kpe_out_ref = next(it), next(it)
    h_scr, q_scr, k_scr, v_scr = it

    n = x_ref.shape[1]
    n_ctx = k_scr.shape[0] - n
    hp = MLA_HEAD_PAD
    c_ckv = MLA_Q_RANK
    c_kpe = c_ckv + MLA_KV_RANK
    c_g = c_kpe + hp

    row = pl.program_id(0) + 1 if row_of_batch else 0
    shift, scale, gate = _mod_rows(mod_ref, row)
    sm_scale = MLA_QK ** -0.5
    n_tiles = n // ROW_TILE
    group = 2
    while group < MLA_HEADS and 2 * group * ROW_TILE * (n_ctx + n) <= MLA_GROUP_SCORES:
        group *= 2

    def store_keys(ckv, kpe, krows):
        k_all = _mm(ckv, wk_ref[...])
        v_scr[:, krows] = _mm_nt(wv_ref[...], ckv).astype(BF16)
        for hd in range(MLA_HEADS):
            cols = slice(hd * hp, (hd + 1) * hp)
            k_scr[krows, cols] = (k_all[:, cols] + kpe).astype(BF16)

    def proj_body(ti):
        r0 = ti * ROW_TILE
        rows = slice(r0, r0 + ROW_TILE)
        h = (x_ref[0, rows, :] * (1.0 + scale) + shift).astype(BF16)
        h_scr[rows, :] = h
        cq = _rms(_mm(h, w_in_ref[:, :c_ckv]), qn_ref[...])
        ckv = _rms(_mm(h, w_in_ref[:, c_ckv:c_kpe]), kvn_ref[...])
        kpe = _mm(h, w_in_ref[:, c_kpe:c_g])
        if has_ctx:
            cos, sin = cos_ref[rows, :], sin_ref[rows, :]
            kpe = _rope(kpe, cos, sin)
        else:
            ckv_out_ref[0, 0, rows, :] = ckv
            kpe_out_ref[0, 0, rows, :] = kpe[:, MLA_NOPE:MLA_QK]
        q_all = _mm(cq, wq_ref[...])
        for hd in range(MLA_HEADS):
            cols = slice(hd * hp, (hd + 1) * hp)
            qh = q_all[:, cols]
            if has_ctx:
                qh = _rope(qh, cos, sin)
            q_scr[rows, cols] = (qh * sm_scale).astype(BF16)
        store_keys(ckv, kpe, slice(n_ctx + r0, n_ctx + r0 + ROW_TILE))

    def ctx_body(ti):
        rows = slice(ti * ROW_TILE, (ti + 1) * ROW_TILE)
        store_keys(ctx_ckv_ref[0, 0, rows, :], ctx_kpe_ref[0, 0, rows, :], rows)

    for ti in range(n_tiles):
        proj_body(ti)
    for ti in range(n_ctx // ROW_TILE):
        ctx_body(ti)

    def tile_body(ti, carry):
        rows = pl.ds(pl.multiple_of(ti * ROW_TILE, ROW_TILE), ROW_TILE)
        o_t = []
        for g0 in range(0, MLA_HEADS, group):
            heads = range(g0, g0 + group)
            ss = [_mm_nt(k_scr[:, hd * hp:(hd + 1) * hp], q_scr[rows, hd * hp:(hd + 1) * hp]) for hd in heads]
            if g0 == 0:
                g = _silu(_mm(h_scr[rows, :], w_in_ref[:, c_g:]))
            ms = [jnp.max(s, axis=0, keepdims=True) for s in ss]
            es = [jnp.exp(s - m) for s, m in zip(ss, ms)]
            ls = [jnp.sum(e, axis=0, keepdims=True) for e in es]
            o_t += [_mm(v_scr[hd * MLA_VDIM:(hd + 1) * MLA_VDIM, :], e) / l for e, l, hd in zip(es, ls, heads)]
        o = jnp.concatenate(o_t, axis=0).T
        y = _mm(o * g, w_out_ref[...])
        out_ref[0, rows, :] = _residual_ln(x_ref[0, rows, :], y, gate, lng_ref[0], lnb_ref[0])
        return carry

    if n_tiles == 1:
        tile_body(0, 0)
    else:
        lax.fori_loop(0, n_tiles, tile_body, 0)


def _mla_layer(x, mod, layer, ln_g, ln_b, weights, q_norm, kv_norm, ctx_ckv, ctx_kpe, *, row_of_batch):
    bsz, n, d = x.shape
    w_in, wq, wk, wv, w_out = weights
    has_ctx = ctx_ckv is not None
    in_specs = [
        _seq_spec(n, d),
        pl.BlockSpec((1, 3, SUBLANES, d), lambda b: (layer, 0, 0, 0)),
        pl.BlockSpec((1, 1, d), lambda b: (layer, 0, 0)),
        pl.BlockSpec((1, 1, d), lambda b: (layer, 0, 0)),
        _const_spec(w_in.shape),
        _const_spec(q_norm.shape),
        _const_spec(kv_norm.shape),
        _const_spec(wq.shape),
        _const_spec(wk.shape),
        _const_spec(wv.shape),
        _const_spec(w_out.shape),
    ]
    args = [x, mod, ln_g, ln_b, w_in, q_norm, kv_norm, wq, wk, wv, w_out]
    out_specs = [_seq_spec(n, d, single_buffer=has_ctx)]
    out_shape = [jax.ShapeDtypeStruct((bsz, n, d), F32)]
    n_ctx = 0
    if has_ctx:
        n_ctx = ctx_ckv.shape[2]
        ctx_kpe = jnp.pad(ctx_kpe, ((0, 0), (0, 0), (0, 0), (MLA_NOPE, MLA_HEAD_PAD - MLA_QK)))
        cos, sin = _rope_tables(n)
        in_specs += [
            pl.BlockSpec((1, 1, n_ctx, MLA_KV_RANK), lambda b: (b, 0, 0, 0)),
            pl.BlockSpec((1, 1, n_ctx, MLA_HEAD_PAD), lambda b: (b, 0, 0, 0)),
            _const_spec(cos.shape),
            _const_spec(sin.shape),
        ]
        args += [ctx_ckv, ctx_kpe, cos, sin]
    else:
        out_specs += [pl.BlockSpec((1, 1, n, MLA_KV_RANK), lambda b: (b, 0, 0, 0)),
                      pl.BlockSpec((1, 1, n, MLA_ROPE), lambda b: (b, 0, 0, 0))]
        out_shape += [jax.ShapeDtypeStruct((bsz, 1, n, MLA_KV_RANK), F32),
                      jax.ShapeDtypeStruct((bsz, 1, n, MLA_ROPE), F32)]
    kernel = functools.partial(_mla_kernel, row_of_batch=row_of_batch, has_ctx=has_ctx)
    res = pl.pallas_call(
        kernel,
        grid=(bsz,),
        in_specs=in_specs,
        out_specs=out_specs,
        out_shape=out_shape,
        scratch_shapes=[
            pltpu.VMEM((n, d), BF16),
            pltpu.VMEM((n, MLA_HEADS * MLA_HEAD_PAD), BF16),
            pltpu.VMEM((n_ctx + n, MLA_HEADS * MLA_HEAD_PAD), BF16),
            pltpu.VMEM((MLA_HEADS * MLA_VDIM, n_ctx + n), BF16),
        ],
        compiler_params=_compiler_params(1),
        name="mla_layer",
    )(*args)
    return res[0] if has_ctx else tuple(res)


def kernel(x_prompt, x_sample, state_hgrn, state_rglru, cache_mla_ckv, cache_mla_kpe, c, c_ctx, ada_w, ada_b, ln_g, ln_b, hg_w_in, hg_lb_logits, hg_norm_g, hg_w_out, sc_w_in, sc_conv_w, sc_conv_b, sc_w_out, rg_w_in, rg_conv_w, rg_conv_b, rg_w_gate, rg_b_gate, rg_lambda, rg_w_out, mla_w_in, mla_q_norm, mla_kv_norm, mla_w_qb, mla_w_kvb, mla_w_out):
    d = x_prompt.shape[-1]
    n_dec = c.shape[0]
    cond = jnp.concatenate([c_ctx[None], c, jnp.zeros((SUBLANES - 1 - n_dec, d), F32)], axis=0)
    mod = _modulation(cond, ada_w, ada_b)
    lng = ln_g.reshape(DEPTH, 1, d)
    lnb = ln_b.reshape(DEPTH, 1, d)
    hg_wi, hg_wo = hg_w_in[0].astype(BF16), hg_w_out[0].astype(BF16)
    sc_wi, sc_wo = sc_w_in[0].astype(BF16), sc_w_out[0].astype(BF16)
    rg_wi, rg_wg, rg_wo = rg_w_in[0].astype(BF16), rg_w_gate[0].astype(BF16), rg_w_out[0].astype(BF16)
    mla_w = _mla_weights(mla_w_in[0], mla_w_qb[0], mla_w_kvb[0], mla_w_out[0])

    def run(x, rob, st_hg, st_rg, ctx_ckv, ctx_kpe):
        r0 = _hgrn_layer(x, mod, 0, lng, lnb, hg_wi, hg_lb_logits, hg_norm_g, hg_wo, st_hg, row_of_batch=rob)
        x, new_hg = r0 if st_hg is None else (r0, None)
        x = _sconv_layer(x, mod, 1, lng, lnb, sc_wi, sc_conv_w[0], sc_conv_b, sc_wo, row_of_batch=rob)
        r2 = _rglru_layer(x, mod, 2, lng, lnb, rg_wi, rg_conv_w[0], rg_conv_b, rg_wg, rg_b_gate[0],
                          rg_lambda[0], rg_wo, st_rg, row_of_batch=rob)
        x, new_rg = r2 if st_rg is None else (r2, None)
        r3 = _mla_layer(x, mod, 3, lng, lnb, mla_w, mla_q_norm, mla_kv_norm, ctx_ckv, ctx_kpe,
                        row_of_batch=rob)
        if ctx_ckv is None:
            return (r3[0], new_hg, new_rg, r3[1], r3[2])
        return (r3,)

    y_prompt, new_hg, new_rg, new_ckv, new_kpe = run(x_prompt, False, None, None, None, None)
    (y_sample,) = run(x_sample, True, state_hgrn, state_rglru, cache_mla_ckv, cache_mla_kpe)
    return (y_prompt, y_sample, new_hg, new_rg, new_ckv, new_kpe)
```

```python
import functools

import jax
import jax.numpy as jnp
import numpy as np
from jax import lax
from jax.experimental import pallas as pl
from jax.experimental.pallas import tpu as pltpu

F32 = jnp.float32
BF16 = jnp.bfloat16

LANES = 128
SUBLANES = 8
VMEM_LIMIT_BYTES = 60 * 1024 * 1024

DEPTH = 4
DEEPNORM_ALPHA = (2 * DEPTH) ** 0.25
LN_EPS = 1e-5
RMS_EPS = 1e-6

ROW_TILE = 256

HG_HEADS = 8
HG_KDIM = 128
HG_VDIM = 128
HG_CHUNK = 32
SC_KERNEL = 3
RG_HEADS = 4
RG_KERNEL = 4
RG_C = 8.0
MLA_HEADS = 16
MLA_Q_RANK = 384
MLA_KV_RANK = 256
MLA_NOPE = 64
MLA_ROPE = 32
MLA_VDIM = 64
MLA_HEAD_PAD = 128
MLA_GROUP_SCORES = 1 << 21
ROPE_BASE = 10000.0
GRID_W = 64


def _mm(a, b):
    return jnp.dot(a.astype(BF16), b.astype(BF16), preferred_element_type=F32)


def _mm_nt(a, b):
    return lax.dot_general(a.astype(BF16), b.astype(BF16), (((1,), (1,)), ((), ())),
                           preferred_element_type=F32)


def _mm_tn(a, b):
    return lax.dot_general(a.astype(BF16), b.astype(BF16), (((0,), (0,)), ((), ())),
                           preferred_element_type=F32)


def _silu(x):
    return x * jax.nn.sigmoid(x)


def _mod_rows(mod_ref, row):
    if isinstance(row, int):
        return tuple(mod_ref[0, j, row:row + 1, :] for j in range(3))
    rows = lax.broadcasted_iota(jnp.int32, (SUBLANES, 1), 0)
    return tuple(jnp.sum(jnp.where(rows == row, mod_ref[0, j], 0.0), axis=0, keepdims=True)
                 for j in range(3))


def _residual_ln(x, y, gate, g, b):
    z = DEEPNORM_ALPHA * x + gate * y
    mu = jnp.mean(z, axis=-1, keepdims=True)
    zc = z - mu
    var = jnp.mean(zc * zc, axis=-1, keepdims=True)
    return zc * lax.rsqrt(var + LN_EPS) * g + b


def _const_spec(shape):
    nd = len(shape)
    return pl.BlockSpec(shape, lambda *_: (0,) * nd, pipeline_mode=pl.Buffered(1))


def _seq_spec(n, d, single_buffer=False):
    mode = dict(pipeline_mode=pl.Buffered(1)) if single_buffer else {}
    return pl.BlockSpec((1, n, d), lambda b: (b, 0, 0), **mode)


def _compiler_params(n_grid):
    return pltpu.CompilerParams(dimension_semantics=("arbitrary",) * n_grid,
                                vmem_limit_bytes=VMEM_LIMIT_BYTES)


def _mod_kernel(cond_ref, w_ref, b_ref, out_ref):
    s = _silu(cond_ref[...])
    out_ref[0, 0] = _mm(s, w_ref[0]) + b_ref[0, 0]


def _modulation(cond, ada_w, ada_b):
    n_layers, d, _ = ada_w.shape
    rows = cond.shape[0]
    return pl.pallas_call(
        _mod_kernel,
        grid=(n_layers, 3),
        in_specs=[
            pl.BlockSpec((rows, d), lambda l, j: (0, 0)),
            pl.BlockSpec((1, d, d), lambda l, j: (l, 0, j)),
            pl.BlockSpec((1, 1, 1, d), lambda l, j: (l, j, 0, 0)),
        ],
        out_specs=pl.BlockSpec((1, 1, rows, d), lambda l, j: (l, j, 0, 0)),
        out_shape=jax.ShapeDtypeStruct((n_layers, 3, rows, d), F32),
        compiler_params=_compiler_params(2),
        name="adaln_modulation",
    )(cond, ada_w, ada_b.reshape(n_layers, 3, 1, d))


def _sconv_kernel(x_ref, mod_ref, lng_ref, lnb_ref, w_in_ref, cw_ref, cb_ref, w_out_ref,
                  out_ref, u_scr, *, row_of_batch):
    n, d = x_ref.shape[1], x_ref.shape[2]
    width = w_out_ref.shape[0]
    row = pl.program_id(0) + 1 if row_of_batch else 0
    shift, scale, gate = _mod_rows(mod_ref, row)
    x = x_ref[0]
    h = (x * (1.0 + scale) + shift).astype(BF16)
    t = lax.broadcasted_iota(jnp.int32, (n, 1), 0)
    chunk = 2 * LANES
    def project(j):
        return tuple(_mm(h, w_in_ref[:, i * width + j * chunk:i * width + (j + 1) * chunk]) for i in range(4))

    nxt = project(0)
    for j in range(width // chunk):
        c0 = j * chunk
        bg, cg, v, g = nxt
        if j + 1 < width // chunk:
            nxt = project(j + 1)
        p = cg * v
        p_prev = jnp.where(t >= 1, pltpu.roll(p, 1, 0), 0.0)
        p_next = jnp.where(t < n - 1, pltpu.roll(p, n - 1, 0), 0.0)
        z = (cw_ref[0:1, c0:c0 + chunk] * p_prev + cw_ref[1:2, c0:c0 + chunk] * p
             + cw_ref[2:3, c0:c0 + chunk] * p_next + cb_ref[:, c0:c0 + chunk])
        u_scr[:, c0:c0 + chunk] = (_silu(g) * bg * z).astype(BF16)
    y = _mm(u_scr[...], w_out_ref[...])
    out_ref[0] = _residual_ln(x, y, gate, lng_ref[0], lnb_ref[0])


def _sconv_layer(x, mod, layer, ln_g, ln_b, w_in, conv_w, conv_b, w_out, *, row_of_batch):
    bsz, n, d = x.shape
    width = w_out.shape[0]
    kernel = functools.partial(_sconv_kernel, row_of_batch=row_of_batch)
    return pl.pallas_call(
        kernel,
        grid=(bsz,),
        in_specs=[
            _seq_spec(n, d),
            pl.BlockSpec((1, 3, SUBLANES, d), lambda b: (layer, 0, 0, 0)),
            pl.BlockSpec((1, 1, d), lambda b: (layer, 0, 0)),
            pl.BlockSpec((1, 1, d), lambda b: (layer, 0, 0)),
            _const_spec(w_in.shape),
            _const_spec(conv_w.shape),
            _const_spec(conv_b.shape),
            _const_spec(w_out.shape),
        ],
        out_specs=_seq_spec(n, d),
        out_shape=jax.ShapeDtypeStruct((bsz, n, d), F32),
        scratch_shapes=[pltpu.VMEM((n, width), BF16)],
        compiler_params=_compiler_params(1),
        name="sconv_layer",
    )(x, mod, ln_g, ln_b, w_in, conv_w, conv_b, w_out)


def _chunk_cumsum(x, reverse):
    n = x.shape[0]
    pos = lax.broadcasted_iota(jnp.int32, (n, 1), 0) % HG_CHUNK
    s = 1
    while s < HG_CHUNK:
        if reverse:
            x = x + jnp.where(pos < HG_CHUNK - s, pltpu.roll(x, n - s, 0), 0.0)
        else:
            x = x + jnp.where(pos >= s, pltpu.roll(x, s, 0), 0.0)
        s *= 2
    return x


def _hgrn_kernel(*refs, row_of_batch, layer, has_s0, emit_state, n_groups):
    it = iter(refs)
    x_ref, mod_ref, lng_ref, lnb_ref, w_in_ref, lbl_ref, ng_ref, w_out_ref = (next(it) for _ in range(8))
    s0_ref = next(it) if has_s0 else None
    out_ref = next(it)
    s_out_ref = next(it) if emit_state else None
    h_scr, qrel_scr, krel_scr, qp_scr, kp_scr, v_scr, dec_scr, o_scr, g_scr, u_scr = it

    n, d = x_ref.shape[1], x_ref.shape[2]
    hk = w_out_ref.shape[0]
    n_tiles = n // ROW_TILE
    chunks_per_tile = ROW_TILE // HG_CHUNK
    n_chunks = n // HG_CHUNK
    pair = 2 * LANES

    row = pl.program_id(0) + 1 if row_of_batch else 0
    shift, scale, gate = _mod_rows(mod_ref, row)
    x = x_ref[0]
    h_scr[...] = (x * (1.0 + scale) + shift).astype(BF16)

    lbs = []
    for dr in range(2):
        z = lbl_ref[dr]
        e = jnp.exp(z - jnp.max(z, axis=0, keepdims=True))
        lbs.append(jnp.sum(e[:layer + 1], axis=0, keepdims=True) / jnp.sum(e, axis=0, keepdims=True))

    ri = lax.broadcasted_iota(jnp.int32, (ROW_TILE, ROW_TILE), 0)
    ci = lax.broadcasted_iota(jnp.int32, (ROW_TILE, ROW_TILE), 1)
    same_chunk = (ri // HG_CHUNK) == (ci // HG_CHUNK)
    masks = (same_chunk & (ci <= ri), same_chunk & (ci >= ri))

    row_chunk = (lax.broadcasted_iota(jnp.int32, (ROW_TILE, LANES), 0) // HG_CHUNK).astype(BF16)

    def chunk_blocks(a):
        zero = jnp.zeros_like(a)
        return jnp.concatenate([jnp.where(row_chunk == c, a, zero) for c in range(chunks_per_tile)], axis=1)

    def project(hp):
        c0 = hp * pair
        slot = hp % 2
        h = h_scr[...]
        q = _silu(_mm(h, w_in_ref[0, :, c0:c0 + pair])) * (HG_KDIM ** -0.5)
        v_scr[slot] = _mm(h, w_in_ref[0, :, 3 * hk + c0:3 * hk + c0 + pair]).astype(BF16)
        q3 = q.reshape(n_chunks, HG_CHUNK, pair)
        for dr in range(2):
            fl = _mm(h, w_in_ref[0, :, (1 + dr) * hk + c0:(1 + dr) * hk + c0 + pair])
            lb = lbs[dr][:, c0:c0 + pair]
            f = lb + (1.0 - lb) * jax.nn.sigmoid(fl)
            k3 = (1.0 - f).reshape(n_chunks, HG_CHUNK, pair)
            cum = _chunk_cumsum(jnp.log(f), reverse=(dr == 1)).reshape(n_chunks, HG_CHUNK, pair)
            mid = HG_CHUNK // 2 - 1
            if dr == 0:
                ref, tot = cum[:, mid:mid + 1], cum[:, HG_CHUNK - 1:HG_CHUNK]
            else:
                ref, tot = cum[:, HG_CHUNK - 1 - mid:HG_CHUNK - mid], cum[:, 0:1]
            q_rel = q3 * jnp.exp(cum - ref)
            k_rel = k3 * jnp.exp(ref - cum)
            qrel_scr[slot, dr] = q_rel.reshape(n, pair).astype(BF16)
            krel_scr[slot, dr] = k_rel.reshape(n, pair).astype(BF16)
            qp_scr[slot, dr] = (q_rel * jnp.exp(ref)).reshape(n, pair).astype(BF16)
            kp_scr[slot, dr] = (k_rel * jnp.exp(tot - ref)).reshape(n, pair).astype(BF16)
            dec_scr[slot, dr] = jnp.exp(tot)
        g_scr[slot] = _silu(_mm(h, w_in_ref[0, :, 4 * hk + c0:4 * hk + c0 + pair]))

    def recur(hp):
        c0 = hp * pair
        slot = hp % 2
        for dr in range(2):
            for h2 in range(2):
                l0 = h2 * LANES
                head = hp * 2 + h2
                if has_s0:
                    s_t0 = s0_ref[0, 0, dr, head].T
                else:
                    s_t0 = jnp.zeros((HG_VDIM, HG_KDIM), F32)

                def tile_body(i, s_t, dr=dr, l0=l0, slot=slot):
                    ti = i if dr == 0 else n_tiles - 1 - i
                    rows = pl.ds(pl.multiple_of(ti * ROW_TILE, ROW_TILE), ROW_TILE)
                    lanes = slice(l0, l0 + LANES)
                    v = v_scr[slot, rows, lanes]
                    att = _mm_nt(qrel_scr[slot, dr, rows, lanes], krel_scr[slot, dr, rows, lanes])
                    o_tile = _mm(jnp.where(masks[dr], att, 0.0), v)
                    upd = _mm_tn(v, chunk_blocks(kp_scr[slot, dr, rows, lanes]))
                    s_prev = [None] * chunks_per_tile
                    order = range(chunks_per_tile) if dr == 0 else range(chunks_per_tile - 1, -1, -1)
                    for c in order:
                        s_prev[c] = s_t.astype(BF16)
                        dec = dec_scr[slot, dr, ti * chunks_per_tile + c][:, lanes]
                        s_t = s_t * dec + upd[:, c * LANES:(c + 1) * LANES]
                    o_tile += _mm_nt(chunk_blocks(qp_scr[slot, dr, rows, lanes]),
                                     jnp.concatenate(s_prev, axis=1))
                    if dr == 0:
                        o_scr[slot, rows, lanes] = o_tile
                    else:
                        o_scr[slot, rows, lanes] += o_tile
                    return s_t

                s_t = s_t0
                for i in range(n_tiles):
                    s_t = tile_body(i, s_t)
                if emit_state:
                    s_out_ref[0, 0, dr, head] = s_t.T

        for h2 in range(2):
            lanes = slice(h2 * LANES, (h2 + 1) * LANES)
            o = o_scr[slot, :, lanes]
            ms = jnp.mean(o * o, axis=-1, keepdims=True)
            on = o * lax.rsqrt(ms + RMS_EPS) * ng_ref[:, c0 + h2 * LANES:c0 + (h2 + 1) * LANES]
            u_scr[:, c0 + h2 * LANES:c0 + (h2 + 1) * LANES] = (on * g_scr[slot, :, lanes]).astype(BF16)

    n_pairs = hk // pair
    project(0)
    for hp in range(n_pairs):
        if hp + 1 < n_pairs:
            project(hp + 1)
        recur(hp)

    y = _mm(u_scr[...], w_out_ref[...])
    if n_groups == 1:
        out_ref[0] = _residual_ln(x, y, gate, lng_ref[0], lnb_ref[0])
    else:
        grp = pl.program_id(1)

        @pl.when(grp == 0)
        def _():
            out_ref[0] = y

        @pl.when((grp > 0) & (grp < n_groups - 1))
        def _():
            out_ref[0] += y

        @pl.when(grp == n_groups - 1)
        def _():
            out_ref[0] = _residual_ln(x, out_ref[0] + y, gate, lng_ref[0], lnb_ref[0])


def _hgrn_layer(x, mod, layer, ln_g, ln_b, w_in, lb_logits, norm_g, w_out, state, *, row_of_batch):
    bsz, n, d = x.shape
    hv = w_out.shape[0]
    has_s0 = state is not None
    emit_state = not has_s0
    pair = 2 * LANES
    n_groups = 2 if n > ROW_TILE else 1
    gcols = hv // n_groups
    heads = HG_HEADS // n_groups
    n_proj = w_in.shape[1] // hv
    w_in = w_in.reshape(d, n_proj, n_groups, gcols).transpose(2, 0, 1, 3).reshape(n_groups, d, n_proj * gcols)
    state_block = (1, 1, 2, heads, HG_KDIM, HG_VDIM)
    state_spec = pl.BlockSpec(state_block, lambda b, g: (b, 0, 0, g, 0, 0))
    seq_spec = pl.BlockSpec((1, n, d), lambda b, g: (b, 0, 0))
    wmode = dict(pipeline_mode=pl.Buffered(1)) if n_groups == 1 else {}
    in_specs = [
        seq_spec,
        pl.BlockSpec((1, 3, SUBLANES, d), lambda b, g: (layer, 0, 0, 0)),
        pl.BlockSpec((1, 1, d), lambda b, g: (layer, 0, 0)),
        pl.BlockSpec((1, 1, d), lambda b, g: (layer, 0, 0)),
        pl.BlockSpec((1, d, n_proj * gcols), lambda b, g: (g, 0, 0), **wmode),
        pl.BlockSpec(lb_logits.shape[:2] + (gcols,), lambda b, g: (0, 0, g), **wmode),
        pl.BlockSpec((1, gcols), lambda b, g: (0, g), **wmode),
        pl.BlockSpec((gcols, d), lambda b, g: (g, 0), **wmode),
    ]
    args = [x, mod, ln_g, ln_b, w_in, lb_logits, norm_g, w_out]
    out_specs = [seq_spec]
    out_shape = [jax.ShapeDtypeStruct((bsz, n, d), F32)]
    if has_s0:
        in_specs.append(state_spec)
        args.append(state)
    if emit_state:
        out_specs.append(state_spec)
        out_shape.append(jax.ShapeDtypeStruct((bsz, 1, 2, HG_HEADS, HG_KDIM, HG_VDIM), F32))
    kernel = functools.partial(_hgrn_kernel, row_of_batch=row_of_batch, layer=layer,
                               has_s0=has_s0, emit_state=emit_state, n_groups=n_groups)
    res = pl.pallas_call(
        kernel,
        grid=(bsz, n_groups),
        in_specs=in_specs,
        out_specs=out_specs,
        out_shape=out_shape,
        scratch_shapes=[
            pltpu.VMEM((n, d), BF16),
            pltpu.VMEM((2, 2, n, pair), BF16),
            pltpu.VMEM((2, 2, n, pair), BF16),
            pltpu.VMEM((2, 2, n, pair), BF16),
            pltpu.VMEM((2, 2, n, pair), BF16),
            pltpu.VMEM((2, n, pair), BF16),
            pltpu.VMEM((2, 2, n // HG_CHUNK, 1, pair), F32),
            pltpu.VMEM((2, n, pair), F32),
            pltpu.VMEM((2, n, pair), F32),
            pltpu.VMEM((n, gcols), BF16),
        ],
        compiler_params=_compiler_params(2),
        name="hgrn2_layer",
    )(*args)
    return res[0] if has_s0 else tuple(res)


def _block_scan(a, x, reverse):
    n, c = a.shape
    nb = n // SUBLANES
    a = a.reshape(nb, SUBLANES, c)
    x = x.reshape(nb, SUBLANES, c)
    pos = lax.broadcasted_iota(jnp.int32, (1, SUBLANES, 1), 1)
    s = 1
    while s < SUBLANES:
        if reverse:
            keep = pos < SUBLANES - s
            shift = SUBLANES - s
        else:
            keep = pos >= s
            shift = s
        a_in = jnp.where(keep, pltpu.roll(a, shift, 1), 1.0)
        x_in = jnp.where(keep, pltpu.roll(x, shift, 1), 0.0)
        x = a * x_in + x
        a = a * a_in
        s *= 2
    return a.reshape(n, c), x.reshape(n, c)


def _rglru_kernel(*refs, row_of_batch, has_s0, emit_state):
    it = iter(refs)
    (x_ref, mod_ref, lng_ref, lnb_ref, w_in_ref, cw_ref, cb_ref, wg_ref, bg_ref, lam_ref,
     w_out_ref) = (next(it) for _ in range(11))
    s0_ref = next(it) if has_s0 else None
    out_ref = next(it)
    s_out_ref = next(it) if emit_state else None
    h_scr, a_scr, b_scr, gs_scr = it

    n, d = x_ref.shape[1], x_ref.shape[2]
    width = w_out_ref.shape[0]
    blk = width // RG_HEADS
    nb = n // SUBLANES

    row = pl.program_id(0) + 1 if row_of_batch else 0
    shift, scale, gate = _mod_rows(mod_ref, row)
    h_scr[...] = (x_ref[0] * (1.0 + scale) + shift).astype(BF16)

    z = -lam_ref[...]
    coef = -RG_C * (jnp.maximum(z, 0.0) + jnp.log1p(jnp.exp(-jnp.abs(z))))

    t = lax.broadcasted_iota(jnp.int32, (n, 1), 0)

    def project(hh):
        h = h_scr[...]
        return (_mm(h, w_in_ref[:, hh * blk:(hh + 1) * blk]),
                _mm(h, w_in_ref[:, width + hh * blk:width + (hh + 1) * blk]))

    def mix(hh, u_pre, g):
        cols = slice(hh * blk, (hh + 1) * blk)
        u = (cw_ref[0:1, cols] * jnp.where(t >= 2, pltpu.roll(u_pre, 2, 0), 0.0)
             + cw_ref[1:2, cols] * jnp.where(t >= 1, pltpu.roll(u_pre, 1, 0), 0.0)
             + cw_ref[2:3, cols] * u_pre
             + cw_ref[3:4, cols] * jnp.where(t < n - 1, pltpu.roll(u_pre, n - 1, 0), 0.0)
             + cb_ref[:, cols])
        gs_scr[:, cols] = _silu(g)
        gates = [jax.nn.sigmoid(_mm(u, wg_ref[dr, hh]) + bg_ref[dr, hh]) for dr in range(2)]
        for dr in range(2):
            a = jnp.exp(coef[dr:dr + 1, cols] * gates[dr][:, :blk])
            xin = jnp.exp(0.5 * jnp.log(1.0 - a * a)) * (gates[dr][:, blk:] * u)
            a_blk, h_blk = _block_scan(a, xin, reverse=(dr == 1))
            a_scr[dr, :, cols] = a_blk
            b_scr[dr, :, cols] = h_blk

    nxt = project(0)
    for hh in range(RG_HEADS):
        cur = nxt
        if hh + 1 < RG_HEADS:
            nxt = project(hh + 1)
        mix(hh, *cur)

    if has_s0:
        carry0 = (s0_ref[0, 0, 0:1, :], s0_ref[0, 0, 1:2, :])
    else:
        carry0 = (jnp.zeros((1, width), F32), jnp.zeros((1, width), F32))

    def carry_body(i, carry):
        cf, cb = carry
        rf = pl.ds(pl.multiple_of(i * SUBLANES, SUBLANES), SUBLANES)
        rb = pl.ds(pl.multiple_of((nb - 1 - i) * SUBLANES, SUBLANES), SUBLANES)
        hf = a_scr[0, rf, :] * cf + b_scr[0, rf, :]
        hb = a_scr[1, rb, :] * cb + b_scr[1, rb, :]
        b_scr[0, rf, :] = hf
        b_scr[1, rb, :] = hb
        return hf[SUBLANES - 1:SUBLANES, :], hb[0:1, :]

    cf, cb = carry0
    for i in range(nb):
        cf, cb = carry_body(i, (cf, cb))
    if emit_state:
        s_out_ref[0, 0, 0:1, :] = cf
        s_out_ref[0, 0, 1:2, :] = cb

    for r0 in range(0, n, ROW_TILE):
        rows = slice(r0, r0 + ROW_TILE)
        mix = ((b_scr[0, rows, :] + b_scr[1, rows, :]) * gs_scr[rows, :]).astype(BF16)
        y = _mm(mix, w_out_ref[...])
        out_ref[0, rows, :] = _residual_ln(x_ref[0, rows, :], y, gate, lng_ref[0], lnb_ref[0])


def _rglru_layer(x, mod, layer, ln_g, ln_b, w_in, conv_w, conv_b, w_gate, b_gate, lam, w_out, state,
                 *, row_of_batch):
    bsz, n, d = x.shape
    width = w_out.shape[0]
    has_s0 = state is not None
    emit_state = not has_s0
    b_gate = b_gate.reshape(2, RG_HEADS, 1, b_gate.shape[-1])
    state_block = (1, 1, 2, width)
    state_spec = pl.BlockSpec(state_block, lambda b: (b, 0, 0, 0))
    in_specs = [
        _seq_spec(n, d),
        pl.BlockSpec((1, 3, SUBLANES, d), lambda b: (layer, 0, 0, 0)),
        pl.BlockSpec((1, 1, d), lambda b: (layer, 0, 0)),
        pl.BlockSpec((1, 1, d), lambda b: (layer, 0, 0)),
        _const_spec(w_in.shape),
        _const_spec(conv_w.shape),
        _const_spec(conv_b.shape),
        _const_spec(w_gate.shape),
        _const_spec(b_gate.shape),
        _const_spec(lam.shape),
        _const_spec(w_out.shape),
    ]
    args = [x, mod, ln_g, ln_b, w_in, conv_w, conv_b, w_gate, b_gate, lam, w_out]
    out_specs = [_seq_spec(n, d)]
    out_shape = [jax.ShapeDtypeStruct((bsz, n, d), F32)]
    if has_s0:
        in_specs.append(state_spec)
        args.append(state)
    if emit_state:
        out_specs.append(state_spec)
        out_shape.append(jax.ShapeDtypeStruct((bsz,) + state_block[1:], F32))
    kernel = functools.partial(_rglru_kernel, row_of_batch=row_of_batch, has_s0=has_s0,
                               emit_state=emit_state)
    res = pl.pallas_call(
        kernel,
        grid=(bsz,),
        in_specs=in_specs,
        out_specs=out_specs,
        out_shape=out_shape,
        scratch_shapes=[
            pltpu.VMEM((n, d), BF16),
            pltpu.VMEM((2, n, width), F32),
            pltpu.VMEM((2, n, width), F32),
            pltpu.VMEM((n, width), F32),
        ],
        compiler_params=_compiler_params(1),
        name="rglru_layer",
    )(*args)
    return res[0] if has_s0 else tuple(res)


MLA_QK = MLA_NOPE + MLA_ROPE


def _mla_weights(w_in, w_qb, w_kvb, w_out):
    w_in, w_qb, w_kvb, w_out = (w.astype(BF16) for w in (w_in, w_qb, w_kvb, w_out))
    d = w_in.shape[0]
    a = MLA_Q_RANK + MLA_KV_RANK
    pad_l, pad_r = MLA_NOPE, MLA_HEAD_PAD - MLA_QK
    w_in_r = jnp.concatenate(
        [w_in[:, :a], jnp.zeros((d, pad_l), w_in.dtype), w_in[:, a:a + MLA_ROPE],
         jnp.zeros((d, pad_r), w_in.dtype), w_in[:, a + MLA_ROPE:]], axis=1)
    wq = w_qb.reshape(MLA_Q_RANK, MLA_HEADS, MLA_QK)
    wq = jnp.pad(wq, ((0, 0), (0, 0), (0, pad_r))).reshape(MLA_Q_RANK, MLA_HEADS * MLA_HEAD_PAD)
    wkv = w_kvb.reshape(MLA_KV_RANK, MLA_HEADS, MLA_NOPE + MLA_VDIM)
    wk = jnp.pad(wkv[:, :, :MLA_NOPE], ((0, 0), (0, 0), (0, MLA_HEAD_PAD - MLA_NOPE)))
    wk = wk.reshape(MLA_KV_RANK, MLA_HEADS * MLA_HEAD_PAD)
    wv_t = wkv[:, :, MLA_NOPE:].reshape(MLA_KV_RANK, MLA_HEADS * MLA_VDIM).T
    return w_in_r, wq, wk, wv_t, w_out


def _rope_tables(n):
    half = MLA_ROPE // 2
    quarter = half // 2
    t = np.arange(n)
    inv_freq = ROPE_BASE ** (-np.arange(0, half, 2, dtype=np.float32) / half)
    cos = np.ones((n, MLA_HEAD_PAD), np.float32)
    sin = np.zeros((n, MLA_HEAD_PAD), np.float32)
    for k, pos in enumerate((t // GRID_W, t % GRID_W)):
        ang = pos[:, None].astype(np.float32) * inv_freq[None].astype(np.float32)
        c, s = np.cos(ang), np.sin(ang)
        base = MLA_NOPE + k * half
        cos[:, base:base + quarter] = c
        cos[:, base + quarter:base + half] = c
        sin[:, base:base + quarter] = -s
        sin[:, base + quarter:base + half] = s
    return jnp.asarray(cos), jnp.asarray(sin)


def _rope(x, cos, sin):
    quarter = MLA_ROPE // 4
    lane = lax.broadcasted_iota(jnp.int32, (1, MLA_HEAD_PAD), 1)
    first = ((lane - MLA_NOPE) % (2 * quarter)) < quarter
    partner = jnp.where(first, pltpu.roll(x, MLA_HEAD_PAD - quarter, 1), pltpu.roll(x, quarter, 1))
    return x * cos + partner * sin


def _rms(x, g):
    return x * lax.rsqrt(jnp.mean(x * x, axis=-1, keepdims=True) + RMS_EPS) * g


def _mla_kernel(*refs, row_of_batch, has_ctx):
    it = iter(refs)
    (x_ref, mod_ref, lng_ref, lnb_ref, w_in_ref, qn_ref, kvn_ref, wq_ref, wk_ref, wv_ref,
     w_out_ref) = (next(it) for _ in range(11))
    if has_ctx:
        ctx_ckv_ref, ctx_kpe_ref, cos_ref, sin_ref = (next(it) for _ in range(4))
    out_ref = next(it)
    if not has_ctx:
        ckv_out_ref, kpe_out_ref = next(it), next(it)
    h_scr, q_scr, k_scr, v_scr = it

    n = x_ref.shape[1]
    n_ctx = k_scr.shape[0] - n
    hp = MLA_HEAD_PAD
    c_ckv = MLA_Q_RANK
    c_kpe = c_ckv + MLA_KV_RANK
    c_g = c_kpe + hp

    row = pl.program_id(0) + 1 if row_of_batch else 0
    shift, scale, gate = _mod_rows(mod_ref, row)
    sm_scale = MLA_QK ** -0.5
    n_tiles = n // ROW_TILE
    group = 2
    while group < MLA_HEADS and 2 * group * ROW_TILE * (n_ctx + n) <= MLA_GROUP_SCORES:
        group *= 2

    def store_keys(ckv, kpe, krows):
        k_all = _mm(ckv, wk_ref[...])
        v_scr[:, krows] = _mm_nt(wv_ref[...], ckv).astype(BF16)
        for hd in range(MLA_HEADS):
            cols = slice(hd * hp, (hd + 1) * hp)
            k_scr[krows, cols] = (k_all[:, cols] + kpe).astype(BF16)

    def proj_body(ti):
        r0 = ti * ROW_TILE
        rows = slice(r0, r0 + ROW_TILE)
        h = (x_ref[0, rows, :] * (1.0 + scale) + shift).astype(BF16)
        h_scr[rows, :] = h
        cq = _rms(_mm(h, w_in_ref[:, :c_ckv]), qn_ref[...])
        ckv = _rms(_mm(h, w_in_ref[:, c_ckv:c_kpe]), kvn_ref[...])
        kpe = _mm(h, w_in_ref[:, c_kpe:c_g])
        if has_ctx:
            cos, sin = cos_ref[rows, :], sin_ref[rows, :]
            kpe = _rope(kpe, cos, sin)
        else:
            ckv_out_ref[0, 0, rows, :] = ckv
            kpe_out_ref[0, 0, rows, :] = kpe[:, MLA_NOPE:MLA_QK]
        q_all = _mm(cq, wq_ref[...])
        for hd in range(MLA_HEADS):
            cols = slice(hd * hp, (hd + 1) * hp)
            qh = q_all[:, cols]
            if has_ctx:
                qh = _rope(qh, cos, sin)
            q_scr[rows, cols] = (qh * sm_scale).astype(BF16)
        store_keys(ckv, kpe, slice(n_ctx + r0, n_ctx + r0 + ROW_TILE))

    def ctx_body(ti):
        rows = slice(ti * ROW_TILE, (ti + 1) * ROW_TILE)
        store_keys(ctx_ckv_ref[0, 0, rows, :], ctx_kpe_ref[0, 0, rows, :], rows)

    for ti in range(n_tiles):
        proj_body(ti)
    for ti in range(n_ctx // ROW_TILE):
        ctx_body(ti)

    def tile_body(ti, carry):
        rows = pl.ds(pl.multiple_of(ti * ROW_TILE, ROW_TILE), ROW_TILE)
        o_t = []
        for g0 in range(0, MLA_HEADS, group):
            heads = range(g0, g0 + group)
            ss = [_mm_nt(k_scr[:, hd * hp:(hd + 1) * hp], q_scr[rows, hd * hp:(hd + 1) * hp]) for hd in heads]
            if g0 == 0:
                g = _silu(_mm(h_scr[rows, :], w_in_ref[:, c_g:]))
            ms = [jnp.max(s, axis=0, keepdims=True) for s in ss]
            es = [jnp.exp(s - m) for s, m in zip(ss, ms)]
            ls = [jnp.sum(e, axis=0, keepdims=True) for e in es]
            o_t += [_mm(v_scr[hd * MLA_VDIM:(hd + 1) * MLA_VDIM, :], e) / l for e, l, hd in zip(es, ls, heads)]
        o = jnp.concatenate(o_t, axis=0).T
        y = _mm(o * g, w_out_ref[...])
        out_ref[0, rows, :] = _residual_ln(x_ref[0, rows, :], y, gate, lng_ref[0], lnb_ref[0])
        return carry

    if n_tiles == 1:
        tile_body(0, 0)
    else:
        lax.fori_loop(0, n_tiles, tile_body, 0)


def _mla_layer(x, mod, layer, ln_g, ln_b, weights, q_norm, kv_norm, ctx_ckv, ctx_kpe, *, row_of_batch):
    bsz, n, d = x.shape
    w_in, wq, wk, wv, w_out = weights
    has_ctx = ctx_ckv is not None
    in_specs = [
        _seq_spec(n, d),
        pl.BlockSpec((1, 3, SUBLANES, d), lambda b: (layer, 0, 0, 0)),
        pl.BlockSpec((1, 1, d), lambda b: (layer, 0, 0)),
        pl.BlockSpec((1, 1, d), lambda b: (layer, 0, 0)),
        _const_spec(w_in.shape),
        _const_spec(q_norm.shape),
        _const_spec(kv_norm.shape),
        _const_spec(wq.shape),
        _const_spec(wk.shape),
        _const_spec(wv.shape),
        _const_spec(w_out.shape),
    ]
    args = [x, mod, ln_g, ln_b, w_in, q_norm, kv_norm, wq, wk, wv, w_out]
    out_specs = [_seq_spec(n, d, single_buffer=has_ctx)]
    out_shape = [jax.ShapeDtypeStruct((bsz, n, d), F32)]
    n_ctx = 0
    if has_ctx:
        n_ctx = ctx_ckv.shape[2]
        ctx_kpe = jnp.pad(ctx_kpe, ((0, 0), (0, 0), (0, 0), (MLA_NOPE, MLA_HEAD_PAD - MLA_QK)))
        cos, sin = _rope_tables(n)
        in_specs += [
            pl.BlockSpec((1, 1, n_ctx, MLA_KV_RANK), lambda b: (b, 0, 0, 0)),
            pl.BlockSpec((1, 1, n_ctx, MLA_HEAD_PAD), lambda b: (b, 0, 0, 0)),
            _const_spec(cos.shape),
            _const_spec(sin.shape),
        ]
        args += [ctx_ckv, ctx_kpe, cos, sin]
    else:
        out_specs += [pl.BlockSpec((1, 1, n, MLA_KV_RANK), lambda b: (b, 0, 0, 0)),
                      pl.BlockSpec((1, 1, n, MLA_ROPE), lambda b: (b, 0, 0, 0))]
        out_shape += [jax.ShapeDtypeStruct((bsz, 1, n, MLA_KV_RANK), F32),
                      jax.ShapeDtypeStruct((bsz, 1, n, MLA_ROPE), F32)]
    kernel = functools.partial(_mla_kernel, row_of_batch=row_of_batch, has_ctx=has_ctx)
    res = pl.pallas_call(
        kernel,
        grid=(bsz,),
        in_specs=in_specs,
        out_specs=out_specs,
        out_shape=out_shape,
        scratch_shapes=[
            pltpu.VMEM((n, d), BF16),
            pltpu.VMEM((n, MLA_HEADS * MLA_HEAD_PAD), BF16),
            pltpu.VMEM((n_ctx + n, MLA_HEADS * MLA_HEAD_PAD), BF16),
            pltpu.VMEM((MLA_HEADS * MLA_VDIM, n_ctx + n), BF16),
        ],
        compiler_params=_compiler_params(1),
        name="mla_layer",
    )(*args)
    return res[0] if has_ctx else tuple(res)


def kernel(x_prompt, x_sample, state_hgrn, state_rglru, cache_mla_ckv, cache_mla_kpe, c, c_ctx, ada_w, ada_b, ln_g, ln_b, hg_w_in, hg_lb_logits, hg_norm_g, hg_w_out, sc_w_in, sc_conv_w, sc_conv_b, sc_w_out, rg_w_in, rg_conv_w, rg_conv_b, rg_w_gate, rg_b_gate, rg_lambda, rg_w_out, mla_w_in, mla_q_norm, mla_kv_norm, mla_w_qb, mla_w_kvb, mla_w_out):
    d = x_prompt.shape[-1]
    n_dec = c.shape[0]
    cond = jnp.concatenate([c_ctx[None], c, jnp.zeros((SUBLANES - 1 - n_dec, d), F32)], axis=0)
    mod = _modulation(cond, ada_w, ada_b)
    lng = ln_g.reshape(DEPTH, 1, d)
    lnb = ln_b.reshape(DEPTH, 1, d)
    hg_wi, hg_wo = hg_w_in[0].astype(BF16), hg_w_out[0].astype(BF16)
    sc_wi, sc_wo = sc_w_in[0].astype(BF16), sc_w_out[0].astype(BF16)
    rg_wi, rg_wg, rg_wo = rg_w_in[0].astype(BF16), rg_w_gate[0].astype(BF16), rg_w_out[0].astype(BF16)
    mla_w = _mla_weights(mla_w_in[0], mla_w_qb[0], mla_w_kvb[0], mla_w_out[0])

    def run(x, rob, st_hg, st_rg, ctx_ckv, ctx_kpe):
        r0 = _hgrn_layer(x, mod, 0, lng, lnb, hg_wi, hg_lb_logits, hg_norm_g, hg_wo, st_hg, row_of_batch=rob)
        x, new_hg = r0 if st_hg is None else (r0, None)
        x = _sconv_layer(x, mod, 1, lng, lnb, sc_wi, sc_conv_w[0], sc_conv_b, sc_wo, row_of_batch=rob)
        r2 = _rglru_layer(x, mod, 2, lng, lnb, rg_wi, rg_conv_w[0], rg_conv_b, rg_wg, rg_b_gate[0],
                          rg_lambda[0], rg_wo, st_rg, row_of_batch=rob)
        x, new_rg = r2 if st_rg is None else (r2, None)
        r3 = _mla_layer(x, mod, 3, lng, lnb, mla_w, mla_q_norm, mla_kv_norm, ctx_ckv, ctx_kpe,
                        row_of_batch=rob)
        if ctx_ckv is None:
            return (r3[0], new_hg, new_rg, r3[1], r3[2])
        return (r3,)

    y_prompt, new_hg, new_rg, new_ckv, new_kpe = run(x_prompt, False, None, None, None, None)
    (y_sample,) = run(x_sample, True, state_hgrn, state_rglru, cache_mla_ckv, cache_mla_kpe)
    return (y_prompt, y_sample, new_hg, new_rg, new_ckv, new_kpe)
```

```python
import functools

import jax
import jax.numpy as jnp
import numpy as np
from jax import lax
from jax.experimental import pallas as pl
from jax.experimental.pallas import tpu as pltpu

F32 = jnp.float32
BF16 = jnp.bfloat16

LANES = 128
SUBLANES = 8
VMEM_LIMIT_BYTES = 60 * 1024 * 1024

DEPTH = 4
DEEPNORM_ALPHA = (2 * DEPTH) ** 0.25
LN_EPS = 1e-5
RMS_EPS = 1e-6

ROW_TILE = 256

HG_HEADS = 8
HG_KDIM = 128
HG_VDIM = 128
HG_CHUNK = 32
SC_KERNEL = 3
RG_HEADS = 4
RG_KERNEL = 4
RG_C = 8.0
MLA_HEADS = 16
MLA_Q_RANK = 384
MLA_KV_RANK = 256
MLA_NOPE = 64
MLA_ROPE = 32
MLA_VDIM = 64
MLA_HEAD_PAD = 128
MLA_GROUP_SCORES = 1 << 21
ROPE_BASE = 10000.0
GRID_W = 64


def _mm(a, b):
    return jnp.dot(a.astype(BF16), b.astype(BF16), preferred_element_type=F32)


def _mm_nt(a, b):
    return lax.dot_general(a.astype(BF16), b.astype(BF16), (((1,), (1,)), ((), ())),
                           preferred_element_type=F32)


def _mm_tn(a, b):
    return lax.dot_general(a.astype(BF16), b.astype(BF16), (((0,), (0,)), ((), ())),
                           preferred_element_type=F32)


def _silu(x):
    return x * jax.nn.sigmoid(x)


def _mod_rows(mod_ref, row):
    if isinstance(row, int):
        return tuple(mod_ref[0, j, row:row + 1, :] for j in range(3))
    rows = lax.broadcasted_iota(jnp.int32, (SUBLANES, 1), 0)
    return tuple(jnp.sum(jnp.where(rows == row, mod_ref[0, j], 0.0), axis=0, keepdims=True)
                 for j in range(3))


def _residual_ln(x, y, gate, g, b):
    z = DEEPNORM_ALPHA * x + gate * y
    mu = jnp.mean(z, axis=-1, keepdims=True)
    zc = z - mu
    var = jnp.mean(zc * zc, axis=-1, keepdims=True)
    return zc * lax.rsqrt(var + LN_EPS) * g + b


def _const_spec(shape):
    nd = len(shape)
    return pl.BlockSpec(shape, lambda *_: (0,) * nd, pipeline_mode=pl.Buffered(1))


def _seq_spec(n, d, single_buffer=False):
    mode = dict(pipeline_mode=pl.Buffered(1)) if single_buffer else {}
    return pl.BlockSpec((1, n, d), lambda b: (b, 0, 0), **mode)


def _compiler_params(n_grid):
    return pltpu.CompilerParams(dimension_semantics=("arbitrary",) * n_grid,
                                vmem_limit_bytes=VMEM_LIMIT_BYTES)


def _mod_kernel(cond_ref, w_ref, b_ref, out_ref):
    s = _silu(cond_ref[...])
    out_ref[0, 0] = _mm(s, w_ref[0]) + b_ref[0, 0]


def _modulation(cond, ada_w, ada_b):
    n_layers, d, _ = ada_w.shape
    rows = cond.shape[0]
    return pl.pallas_call(
        _mod_kernel,
        grid=(n_layers, 3),
        in_specs=[
            pl.BlockSpec((rows, d), lambda l, j: (0, 0)),
            pl.BlockSpec((1, d, d), lambda l, j: (l, 0, j)),
            pl.BlockSpec((1, 1, 1, d), lambda l, j: (l, j, 0, 0)),
        ],
        out_specs=pl.BlockSpec((1, 1, rows, d), lambda l, j: (l, j, 0, 0)),
        out_shape=jax.ShapeDtypeStruct((n_layers, 3, rows, d), F32),
        compiler_params=_compiler_params(2),
        name="adaln_modulation",
    )(cond, ada_w, ada_b.reshape(n_layers, 3, 1, d))


def _sconv_kernel(x_ref, mod_ref, lng_ref, lnb_ref, w_in_ref, cw_ref, cb_ref, w_out_ref,
                  out_ref, u_scr, *, row_of_batch):
    n, d = x_ref.shape[1], x_ref.shape[2]
    width = w_out_ref.shape[0]
    row = pl.program_id(0) + 1 if row_of_batch else 0
    shift, scale, gate = _mod_rows(mod_ref, row)
    x = x_ref[0]
    h = (x * (1.0 + scale) + shift).astype(BF16)
    t = lax.broadcasted_iota(jnp.int32, (n, 1), 0)
    chunk = 2 * LANES
    def project(j):
        return tuple(_mm(h, w_in_ref[:, i * width + j * chunk:i * width + (j + 1) * chunk]) for i in range(4))

    nxt = project(0)
    for j in range(width // chunk):
        c0 = j * chunk
        bg, cg, v, g = nxt
        if j + 1 < width // chunk:
            nxt = project(j + 1)
        p = cg * v
        p_prev = jnp.where(t >= 1, pltpu.roll(p, 1, 0), 0.0)
        p_next = jnp.where(t < n - 1, pltpu.roll(p, n - 1, 0), 0.0)
        z = (cw_ref[0:1, c0:c0 + chunk] * p_prev + cw_ref[1:2, c0:c0 + chunk] * p
             + cw_ref[2:3, c0:c0 + chunk] * p_next + cb_ref[:, c0:c0 + chunk])
        u_scr[:, c0:c0 + chunk] = (_silu(g) * bg * z).astype(BF16)
    y = _mm(u_scr[...], w_out_ref[...])
    out_ref[0] = _residual_ln(x, y, gate, lng_ref[0], lnb_ref[0])


def _sconv_layer(x, mod, layer, ln_g, ln_b, w_in, conv_w, conv_b, w_out, *, row_of_batch):
    bsz, n, d = x.shape
    width = w_out.shape[0]
    kernel = functools.partial(_sconv_kernel, row_of_batch=row_of_batch)
    return pl.pallas_call(
        kernel,
        grid=(bsz,),
        in_specs=[
            _seq_spec(n, d),
            pl.BlockSpec((1, 3, SUBLANES, d), lambda b: (layer, 0, 0, 0)),
            pl.BlockSpec((1, 1, d), lambda b: (layer, 0, 0)),
            pl.BlockSpec((1, 1, d), lambda b: (layer, 0, 0)),
            _const_spec(w_in.shape),
            _const_spec(conv_w.shape),
            _const_spec(conv_b.shape),
            _const_spec(w_out.shape),
        ],
        out_specs=_seq_spec(n, d),
        out_shape=jax.ShapeDtypeStruct((bsz, n, d), F32),
        scratch_shapes=[pltpu.VMEM((n, width), BF16)],
        compiler_params=_compiler_params(1),
        name="sconv_layer",
    )(x, mod, ln_g, ln_b, w_in, conv_w, conv_b, w_out)


def _chunk_cumsum(x, reverse):
    n = x.shape[0]
    pos = lax.broadcasted_iota(jnp.int32, (n, 1), 0) % HG_CHUNK
    s = 1
    while s < HG_CHUNK:
        if reverse:
            x = x + jnp.where(pos < HG_CHUNK - s, pltpu.roll(x, n - s, 0), 0.0)
        else:
            x = x + jnp.where(pos >= s, pltpu.roll(x, s, 0), 0.0)
        s *= 2
    return x


def _hgrn_kernel(*refs, row_of_batch, layer, has_s0, emit_state, n_groups):
    it = iter(refs)
    x_ref, mod_ref, lng_ref, lnb_ref = (next(it) for _ in range(4))
    wq_ref, wf_fw_ref, wf_bw_ref, wv_ref, wg_ref = (next(it) for _ in range(5))
    lbl_ref, ng_ref, w_out_ref = (next(it) for _ in range(3))
    s0_ref = next(it) if has_s0 else None
    out_ref = next(it)
    s_out_ref = next(it) if emit_state else None
    h_scr, qrel_scr, krel_scr, qp_scr, kp_scr, v_scr, dec_scr, o_scr, g_scr, u_scr = it

    n, d = x_ref.shape[1], x_ref.shape[2]
    hk = w_out_ref.shape[0]
    n_tiles = n // ROW_TILE
    chunks_per_tile = ROW_TILE // HG_CHUNK
    n_chunks = n // HG_CHUNK
    pair = 2 * LANES

    row = pl.program_id(0) + 1 if row_of_batch else 0
    shift, scale, gate = _mod_rows(mod_ref, row)
    x = x_ref[0]
    h_scr[...] = (x * (1.0 + scale) + shift).astype(BF16)

    lbs = []
    for dr in range(2):
        z = lbl_ref[dr]
        e = jnp.exp(z - jnp.max(z, axis=0, keepdims=True))
        lbs.append(jnp.sum(e[:layer + 1], axis=0, keepdims=True) / jnp.sum(e, axis=0, keepdims=True))

    ri = lax.broadcasted_iota(jnp.int32, (ROW_TILE, ROW_TILE), 0)
    ci = lax.broadcasted_iota(jnp.int32, (ROW_TILE, ROW_TILE), 1)
    same_chunk = (ri // HG_CHUNK) == (ci // HG_CHUNK)
    masks = (same_chunk & (ci <= ri), same_chunk & (ci >= ri))

    row_chunk = (lax.broadcasted_iota(jnp.int32, (ROW_TILE, LANES), 0) // HG_CHUNK).astype(BF16)

    def chunk_blocks(a):
        zero = jnp.zeros_like(a)
        return jnp.concatenate([jnp.where(row_chunk == c, a, zero) for c in range(chunks_per_tile)], axis=1)

    def project(hp):
        c0 = hp * pair
        slot = hp % 2
        h = h_scr[...]
        q = _silu(_mm(h, wq_ref[:, c0:c0 + pair])) * (HG_KDIM ** -0.5)
        v_scr[slot] = _mm(h, wv_ref[:, c0:c0 + pair]).astype(BF16)
        q3 = q.reshape(n_chunks, HG_CHUNK, pair)
        for dr in range(2):
            fl = _mm(h, (wf_fw_ref, wf_bw_ref)[dr][:, c0:c0 + pair])
            lb = lbs[dr][:, c0:c0 + pair]
            f = lb + (1.0 - lb) * jax.nn.sigmoid(fl)
            k3 = (1.0 - f).reshape(n_chunks, HG_CHUNK, pair)
            cum = _chunk_cumsum(jnp.log(f), reverse=(dr == 1)).reshape(n_chunks, HG_CHUNK, pair)
            mid = HG_CHUNK // 2 - 1
            if dr == 0:
                ref, tot = cum[:, mid:mid + 1], cum[:, HG_CHUNK - 1:HG_CHUNK]
            else:
                ref, tot = cum[:, HG_CHUNK - 1 - mid:HG_CHUNK - mid], cum[:, 0:1]
            q_rel = q3 * jnp.exp(cum - ref)
            k_rel = k3 * jnp.exp(ref - cum)
            qrel_scr[slot, dr] = q_rel.reshape(n, pair).astype(BF16)
            krel_scr[slot, dr] = k_rel.reshape(n, pair).astype(BF16)
            qp_scr[slot, dr] = (q_rel * jnp.exp(ref)).reshape(n, pair).astype(BF16)
            kp_scr[slot, dr] = (k_rel * jnp.exp(tot - ref)).reshape(n, pair).astype(BF16)
            dec_scr[slot, dr] = jnp.exp(tot)
        g_scr[slot] = _silu(_mm(h, wg_ref[:, c0:c0 + pair]))

    def recur(hp):
        c0 = hp * pair
        slot = hp % 2
        for dr in range(2):
            for h2 in range(2):
                l0 = h2 * LANES
                head = hp * 2 + h2
                if has_s0:
                    s_t0 = s0_ref[0, 0, dr, head].T
                else:
                    s_t0 = jnp.zeros((HG_VDIM, HG_KDIM), F32)

                def tile_body(i, s_t, dr=dr, l0=l0, slot=slot):
                    ti = i if dr == 0 else n_tiles - 1 - i
                    rows = pl.ds(pl.multiple_of(ti * ROW_TILE, ROW_TILE), ROW_TILE)
                    lanes = slice(l0, l0 + LANES)
                    v = v_scr[slot, rows, lanes]
                    att = _mm_nt(qrel_scr[slot, dr, rows, lanes], krel_scr[slot, dr, rows, lanes])
                    o_tile = _mm(jnp.where(masks[dr], att, 0.0), v)
                    upd = _mm_tn(v, chunk_blocks(kp_scr[slot, dr, rows, lanes]))
                    s_prev = [None] * chunks_per_tile
                    order = range(chunks_per_tile) if dr == 0 else range(chunks_per_tile - 1, -1, -1)
                    for c in order:
                        s_prev[c] = s_t.astype(BF16)
                        dec = dec_scr[slot, dr, ti * chunks_per_tile + c][:, lanes]
                        s_t = s_t * dec + upd[:, c * LANES:(c + 1) * LANES]
                    o_tile += _mm_nt(chunk_blocks(qp_scr[slot, dr, rows, lanes]),
                                     jnp.concatenate(s_prev, axis=1))
                    if dr == 0:
                        o_scr[slot, rows, lanes] = o_tile
                    else:
                        o_scr[slot, rows, lanes] += o_tile
                    return s_t

                s_t = s_t0
                for i in range(n_tiles):
                    s_t = tile_body(i, s_t)
                if emit_state:
                    s_out_ref[0, 0, dr, head] = s_t.T

        for h2 in range(2):
            lanes = slice(h2 * LANES, (h2 + 1) * LANES)
            o = o_scr[slot, :, lanes]
            ms = jnp.mean(o * o, axis=-1, keepdims=True)
            on = o * lax.rsqrt(ms + RMS_EPS) * ng_ref[:, c0 + h2 * LANES:c0 + (h2 + 1) * LANES]
            u_scr[:, c0 + h2 * LANES:c0 + (h2 + 1) * LANES] = (on * g_scr[slot, :, lanes]).astype(BF16)

    n_pairs = hk // pair
    project(0)
    for hp in range(n_pairs):
        if hp + 1 < n_pairs:
            project(hp + 1)
        recur(hp)

    y = _mm(u_scr[...], w_out_ref[...])
    if n_groups == 1:
        out_ref[0] = _residual_ln(x, y, gate, lng_ref[0], lnb_ref[0])
    else:
        grp = pl.program_id(1)

        @pl.when(grp == 0)
        def _():
            out_ref[0] = y

        @pl.when((grp > 0) & (grp < n_groups - 1))
        def _():
            out_ref[0] += y

        @pl.when(grp == n_groups - 1)
        def _():
            out_ref[0] = _residual_ln(x, out_ref[0] + y, gate, lng_ref[0], lnb_ref[0])


def _hgrn_layer(x, mod, layer, ln_g, ln_b, w_in, lb_logits, norm_g, w_out, state, *, row_of_batch):
    bsz, n, d = x.shape
    hv = w_out.shape[0]
    has_s0 = state is not None
    emit_state = not has_s0
    pair = 2 * LANES
    n_groups = 2 if n > ROW_TILE else 1
    gcols = hv // n_groups
    heads = HG_HEADS // n_groups
    n_proj = w_in.shape[1] // hv
    state_block = (1, 1, 2, heads, HG_KDIM, HG_VDIM)
    state_spec = pl.BlockSpec(state_block, lambda b, g: (b, 0, 0, g, 0, 0))
    seq_spec = pl.BlockSpec((1, n, d), lambda b, g: (b, 0, 0))
    wmode = dict(pipeline_mode=pl.Buffered(1)) if n_groups == 1 else {}
    in_specs = [
        seq_spec,
        pl.BlockSpec((1, 3, SUBLANES, d), lambda b, g: (layer, 0, 0, 0)),
        pl.BlockSpec((1, 1, d), lambda b, g: (layer, 0, 0)),
        pl.BlockSpec((1, 1, d), lambda b, g: (layer, 0, 0)),
        *[pl.BlockSpec((d, gcols), functools.partial(lambda b, g, i: (0, i * n_groups + g), i=i), **wmode)
          for i in range(n_proj)],
        pl.BlockSpec(lb_logits.shape[:2] + (gcols,), lambda b, g: (0, 0, g), **wmode),
        pl.BlockSpec((1, gcols), lambda b, g: (0, g), **wmode),
        pl.BlockSpec((gcols, d), lambda b, g: (g, 0), **wmode),
    ]
    args = [x, mod, ln_g, ln_b] + [w_in] * n_proj + [lb_logits, norm_g, w_out]
    out_specs = [seq_spec]
    out_shape = [jax.ShapeDtypeStruct((bsz, n, d), F32)]
    if has_s0:
        in_specs.append(state_spec)
        args.append(state)
    if emit_state:
        out_specs.append(state_spec)
        out_shape.append(jax.ShapeDtypeStruct((bsz, 1, 2, HG_HEADS, HG_KDIM, HG_VDIM), F32))
    kernel = functools.partial(_hgrn_kernel, row_of_batch=row_of_batch, layer=layer,
                               has_s0=has_s0, emit_state=emit_state, n_groups=n_groups)
    res = pl.pallas_call(
        kernel,
        grid=(bsz, n_groups),
        in_specs=in_specs,
        out_specs=out_specs,
        out_shape=out_shape,
        scratch_shapes=[
            pltpu.VMEM((n, d), BF16),
            pltpu.VMEM((2, 2, n, pair), BF16),
            pltpu.VMEM((2, 2, n, pair), BF16),
            pltpu.VMEM((2, 2, n, pair), BF16),
            pltpu.VMEM((2, 2, n, pair), BF16),
            pltpu.VMEM((2, n, pair), BF16),
            pltpu.VMEM((2, 2, n // HG_CHUNK, 1, pair), F32),
            pltpu.VMEM((2, n, pair), F32),
            pltpu.VMEM((2, n, pair), F32),
            pltpu.VMEM((n, gcols), BF16),
        ],
        compiler_params=_compiler_params(2),
        name="hgrn2_layer",
    )(*args)
    return res[0] if has_s0 else tuple(res)


def _block_scan(a, x, reverse):
    n, c = a.shape
    nb = n // SUBLANES
    a = a.reshape(nb, SUBLANES, c)
    x = x.reshape(nb, SUBLANES, c)
    pos = lax.broadcasted_iota(jnp.int32, (1, SUBLANES, 1), 1)
    s = 1
    while s < SUBLANES:
        if reverse:
            keep = pos < SUBLANES - s
            shift = SUBLANES - s
        else:
            keep = pos >= s
            shift = s
        a_in = jnp.where(keep, pltpu.roll(a, shift, 1), 1.0)
        x_in = jnp.where(keep, pltpu.roll(x, shift, 1), 0.0)
        x = a * x_in + x
        a = a * a_in
        s *= 2
    return a.reshape(n, c), x.reshape(n, c)


def _rglru_kernel(*refs, row_of_batch, has_s0, emit_state):
    it = iter(refs)
    (x_ref, mod_ref, lng_ref, lnb_ref, w_in_ref, cw_ref, cb_ref, wg_ref, bg_ref, lam_ref,
     w_out_ref) = (next(it) for _ in range(11))
    s0_ref = next(it) if has_s0 else None
    out_ref = next(it)
    s_out_ref = next(it) if emit_state else None
    h_scr, a_scr, b_scr, gs_scr = it

    n, d = x_ref.shape[1], x_ref.shape[2]
    width = w_out_ref.shape[0]
    blk = width // RG_HEADS
    nb = n // SUBLANES

    row = pl.program_id(0) + 1 if row_of_batch else 0
    shift, scale, gate = _mod_rows(mod_ref, row)
    h_scr[...] = (x_ref[0] * (1.0 + scale) + shift).astype(BF16)

    z = -lam_ref[...]
    coef = -RG_C * (jnp.maximum(z, 0.0) + jnp.log1p(jnp.exp(-jnp.abs(z))))

    t = lax.broadcasted_iota(jnp.int32, (n, 1), 0)

    def project(hh):
        h = h_scr[...]
        return (_mm(h, w_in_ref[:, hh * blk:(hh + 1) * blk]),
                _mm(h, w_in_ref[:, width + hh * blk:width + (hh + 1) * blk]))

    def mix(hh, u_pre, g):
        cols = slice(hh * blk, (hh + 1) * blk)
        u = (cw_ref[0:1, cols] * jnp.where(t >= 2, pltpu.roll(u_pre, 2, 0), 0.0)
             + cw_ref[1:2, cols] * jnp.where(t >= 1, pltpu.roll(u_pre, 1, 0), 0.0)
             + cw_ref[2:3, cols] * u_pre
             + cw_ref[3:4, cols] * jnp.where(t < n - 1, pltpu.roll(u_pre, n - 1, 0), 0.0)
             + cb_ref[:, cols])
        gs_scr[:, cols] = _silu(g)
        gates = [jax.nn.sigmoid(_mm(u, wg_ref[dr, hh]) + bg_ref[dr, hh]) for dr in range(2)]
        for dr in range(2):
            a = jnp.exp(coef[dr:dr + 1, cols] * gates[dr][:, :blk])
            xin = jnp.exp(0.5 * jnp.log(1.0 - a * a)) * (gates[dr][:, blk:] * u)
            a_blk, h_blk = _block_scan(a, xin, reverse=(dr == 1))
            a_scr[dr, :, cols] = a_blk
            b_scr[dr, :, cols] = h_blk

    nxt = project(0)
    for hh in range(RG_HEADS):
        cur = nxt
        if hh + 1 < RG_HEADS:
            nxt = project(hh + 1)
        mix(hh, *cur)

    if has_s0:
        carry0 = (s0_ref[0, 0, 0:1, :], s0_ref[0, 0, 1:2, :])
    else:
        carry0 = (jnp.zeros((1, width), F32), jnp.zeros((1, width), F32))

    def carry_body(i, carry):
        cf, cb = carry
        rf = pl.ds(pl.multiple_of(i * SUBLANES, SUBLANES), SUBLANES)
        rb = pl.ds(pl.multiple_of((nb - 1 - i) * SUBLANES, SUBLANES), SUBLANES)
        hf = a_scr[0, rf, :] * cf + b_scr[0, rf, :]
        hb = a_scr[1, rb, :] * cb + b_scr[1, rb, :]
        b_scr[0, rf, :] = hf
        b_scr[1, rb, :] = hb
        return hf[SUBLANES - 1:SUBLANES, :], hb[0:1, :]

    cf, cb = carry0
    for i in range(nb):
        cf, cb = carry_body(i, (cf, cb))
    if emit_state:
        s_out_ref[0, 0, 0:1, :] = cf
        s_out_ref[0, 0, 1:2, :] = cb

    for r0 in range(0, n, ROW_TILE):
        rows = slice(r0, r0 + ROW_TILE)
        mix = ((b_scr[0, rows, :] + b_scr[1, rows, :]) * gs_scr[rows, :]).astype(BF16)
        y = _mm(mix, w_out_ref[...])
        out_ref[0, rows, :] = _residual_ln(x_ref[0, rows, :], y, gate, lng_ref[0], lnb_ref[0])


def _rglru_layer(x, mod, layer, ln_g, ln_b, w_in, conv_w, conv_b, w_gate, b_gate, lam, w_out, state,
                 *, row_of_batch):
    bsz, n, d = x.shape
    width = w_out.shape[0]
    has_s0 = state is not None
    emit_state = not has_s0
    b_gate = b_gate.reshape(2, RG_HEADS, 1, b_gate.shape[-1])
    state_block = (1, 1, 2, width)
    state_spec = pl.BlockSpec(state_block, lambda b: (b, 0, 0, 0))
    in_specs = [
        _seq_spec(n, d),
        pl.BlockSpec((1, 3, SUBLANES, d), lambda b: (layer, 0, 0, 0)),
        pl.BlockSpec((1, 1, d), lambda b: (layer, 0, 0)),
        pl.BlockSpec((1, 1, d), lambda b: (layer, 0, 0)),
        _const_spec(w_in.shape),
        _const_spec(conv_w.shape),
        _const_spec(conv_b.shape),
        _const_spec(w_gate.shape),
        _const_spec(b_gate.shape),
        _const_spec(lam.shape),
        _const_spec(w_out.shape),
    ]
    args = [x, mod, ln_g, ln_b, w_in, conv_w, conv_b, w_gate, b_gate, lam, w_out]
    out_specs = [_seq_spec(n, d)]
    out_shape = [jax.ShapeDtypeStruct((bsz, n, d), F32)]
    if has_s0:
        in_specs.append(state_spec)
        args.append(state)
    if emit_state:
        out_specs.append(state_spec)
        out_shape.append(jax.ShapeDtypeStruct((bsz,) + state_block[1:], F32))
    kernel = functools.partial(_rglru_kernel, row_of_batch=row_of_batch, has_s0=has_s0,
                               emit_state=emit_state)
    res = pl.pallas_call(
        kernel,
        grid=(bsz,),
        in_specs=in_specs,
        out_specs=out_specs,
        out_shape=out_shape,
        scratch_shapes=[
            pltpu.VMEM((n, d), BF16),
            pltpu.VMEM((2, n, width), F32),
            pltpu.VMEM((2, n, width), F32),
            pltpu.VMEM((n, width), F32),
        ],
        compiler_params=_compiler_params(1),
        name="rglru_layer",
    )(*args)
    return res[0] if has_s0 else tuple(res)


MLA_QK = MLA_NOPE + MLA_ROPE


def _mla_weights(w_in, w_qb, w_kvb, w_out):
    w_in, w_qb, w_kvb, w_out = (w.astype(BF16) for w in (w_in, w_qb, w_kvb, w_out))
    d = w_in.shape[0]
    a = MLA_Q_RANK + MLA_KV_RANK
    pad_l, pad_r = MLA_NOPE, MLA_HEAD_PAD - MLA_QK
    w_in_r = jnp.concatenate(
        [w_in[:, :a], jnp.zeros((d, pad_l), w_in.dtype), w_in[:, a:a + MLA_ROPE],
         jnp.zeros((d, pad_r), w_in.dtype), w_in[:, a + MLA_ROPE:]], axis=1)
    wq = w_qb.reshape(MLA_Q_RANK, MLA_HEADS, MLA_QK)
    wq = jnp.pad(wq, ((0, 0), (0, 0), (0, pad_r))).reshape(MLA_Q_RANK, MLA_HEADS * MLA_HEAD_PAD)
    wkv = w_kvb.reshape(MLA_KV_RANK, MLA_HEADS, MLA_NOPE + MLA_VDIM)
    wk = jnp.pad(wkv[:, :, :MLA_NOPE], ((0, 0), (0, 0), (0, MLA_HEAD_PAD - MLA_NOPE)))
    wk = wk.reshape(MLA_KV_RANK, MLA_HEADS * MLA_HEAD_PAD)
    wv_t = wkv[:, :, MLA_NOPE:].reshape(MLA_KV_RANK, MLA_HEADS * MLA_VDIM).T
    return w_in_r, wq, wk, wv_t, w_out


def _rope_tables(n):
    half = MLA_ROPE // 2
    quarter = half // 2
    t = np.arange(n)
    inv_freq = ROPE_BASE ** (-np.arange(0, half, 2, dtype=np.float32) / half)
    cos = np.ones((n, MLA_HEAD_PAD), np.float32)
    sin = np.zeros((n, MLA_HEAD_PAD), np.float32)
    for k, pos in enumerate((t // GRID_W, t % GRID_W)):
        ang = pos[:, None].astype(np.float32) * inv_freq[None].astype(np.float32)
        c, s = np.cos(ang), np.sin(ang)
        base = MLA_NOPE + k * half
        cos[:, base:base + quarter] = c
        cos[:, base + quarter:base + half] = c
        sin[:, base:base + quarter] = -s
        sin[:, base + quarter:base + half] = s
    return jnp.asarray(cos), jnp.asarray(sin)


def _rope(x, cos, sin):
    quarter = MLA_ROPE // 4
    lane = lax.broadcasted_iota(jnp.int32, (1, MLA_HEAD_PAD), 1)
    first = ((lane - MLA_NOPE) % (2 * quarter)) < quarter
    partner = jnp.where(first, pltpu.roll(x, MLA_HEAD_PAD - quarter, 1), pltpu.roll(x, quarter, 1))
    return x * cos + partner * sin


def _rms(x, g):
    return x * lax.rsqrt(jnp.mean(x * x, axis=-1, keepdims=True) + RMS_EPS) * g


def _mla_kernel(*refs, row_of_batch, has_ctx):
    it = iter(refs)
    (x_ref, mod_ref, lng_ref, lnb_ref, w_in_ref, qn_ref, kvn_ref, wq_ref, wk_ref, wv_ref,
     w_out_ref) = (next(it) for _ in range(11))
    if has_ctx:
        ctx_ckv_ref, ctx_kpe_ref, cos_ref, sin_ref = (next(it) for _ in range(4))
    out_ref = next(it)
    if not has_ctx:
        ckv_out_ref, kpe_out_ref = next(it), next(it)
    h_scr, q_scr, k_scr, v_scr = it

    n = x_ref.shape[1]
    n_ctx = k_scr.shape[0] - n
    hp = MLA_HEAD_PAD
    c_ckv = MLA_Q_RANK
    c_kpe = c_ckv + MLA_KV_RANK
    c_g = c_kpe + hp

    row = pl.program_id(0) + 1 if row_of_batch else 0
    shift, scale, gate = _mod_rows(mod_ref, row)
    sm_scale = MLA_QK ** -0.5
    n_tiles = n // ROW_TILE
    group = 2
    while group < MLA_HEADS and 2 * group * ROW_TILE * (n_ctx + n) <= MLA_GROUP_SCORES:
        group *= 2

    def store_keys(ckv, kpe, krows):
        k_all = _mm(ckv, wk_ref[...])
        v_scr[:, krows] = _mm_nt(wv_ref[...], ckv).astype(BF16)
        for hd in range(MLA_HEADS):
            cols = slice(hd * hp, (hd + 1) * hp)
            k_scr[krows, cols] = (k_all[:, cols] + kpe).astype(BF16)

    def proj_body(ti):
        r0 = ti * ROW_TILE
        rows = slice(r0, r0 + ROW_TILE)
        h = (x_ref[0, rows, :] * (1.0 + scale) + shift).astype(BF16)
        h_scr[rows, :] = h
        cq = _rms(_mm(h, w_in_ref[:, :c_ckv]), qn_ref[...])
        ckv = _rms(_mm(h, w_in_ref[:, c_ckv:c_kpe]), kvn_ref[...])
        kpe = _mm(h, w_in_ref[:, c_kpe:c_g])
        if has_ctx:
            cos, sin = cos_ref[rows, :], sin_ref[rows, :]
            kpe = _rope(kpe, cos, sin)
        else:
            ckv_out_ref[0, 0, rows, :] = ckv
            kpe_out_ref[0, 0, rows, :] = kpe[:, MLA_NOPE:MLA_QK]
        q_all = _mm(cq, wq_ref[...])
        for hd in range(MLA_HEADS):
            cols = slice(hd * hp, (hd + 1) * hp)
            qh = q_all[:, cols]
            if has_ctx:
                qh = _rope(qh, cos, sin)
            q_scr[rows, cols] = (qh * sm_scale).astype(BF16)
        store_keys(ckv, kpe, slice(n_ctx + r0, n_ctx + r0 + ROW_TILE))

    def ctx_body(ti):
        rows = slice(ti * ROW_TILE, (ti + 1) * ROW_TILE)
        store_keys(ctx_ckv_ref[0, 0, rows, :], ctx_kpe_ref[0, 0, rows, :], rows)

    for ti in range(n_tiles):
        proj_body(ti)
    for ti in range(n_ctx // ROW_TILE):
        ctx_body(ti)

    def tile_body(ti, carry):
        rows = pl.ds(pl.multiple_of(ti * ROW_TILE, ROW_TILE), ROW_TILE)
        o_t = []
        for g0 in range(0, MLA_HEADS, group):
            heads = range(g0, g0 + group)
            ss = [_mm_nt(k_scr[:, hd * hp:(hd + 1) * hp], q_scr[rows, hd * hp:(hd + 1) * hp]) for hd in heads]
            if g0 == 0:
                g = _silu(_mm(h_scr[rows, :], w_in_ref[:, c_g:]))
            ms = [jnp.max(s, axis=0, keepdims=True) for s in ss]
            es = [jnp.exp(s - m) for s, m in zip(ss, ms)]
            ls = [jnp.sum(e, axis=0, keepdims=True) for e in es]
            o_t += [_mm(v_scr[hd * MLA_VDIM:(hd + 1) * MLA_VDIM, :], e) / l for e, l, hd in zip(es, ls, heads)]
        o = jnp.concatenate(o_t, axis=0).T
        y = _mm(o * g, w_out_ref[...])
        out_ref[0, rows, :] = _residual_ln(x_ref[0, rows, :], y, gate, lng_ref[0], lnb_ref[0])
        return carry

    if n_tiles == 1:
        tile_body(0, 0)
    else:
        lax.fori_loop(0, n_tiles, tile_body, 0)


def _mla_layer(x, mod, layer, ln_g, ln_b, weights, q_norm, kv_norm, ctx_ckv, ctx_kpe, *, row_of_batch):
    bsz, n, d = x.shape
    w_in, wq, wk, wv, w_out = weights
    has_ctx = ctx_ckv is not None
    in_specs = [
        _seq_spec(n, d),
        pl.BlockSpec((1, 3, SUBLANES, d), lambda b: (layer, 0, 0, 0)),
        pl.BlockSpec((1, 1, d), lambda b: (layer, 0, 0)),
        pl.BlockSpec((1, 1, d), lambda b: (layer, 0, 0)),
        _const_spec(w_in.shape),
        _const_spec(q_norm.shape),
        _const_spec(kv_norm.shape),
        _const_spec(wq.shape),
        _const_spec(wk.shape),
        _const_spec(wv.shape),
        _const_spec(w_out.shape),
    ]
    args = [x, mod, ln_g, ln_b, w_in, q_norm, kv_norm, wq, wk, wv, w_out]
    out_specs = [_seq_spec(n, d, single_buffer=has_ctx)]
    out_shape = [jax.ShapeDtypeStruct((bsz, n, d), F32)]
    n_ctx = 0
    if has_ctx:
        n_ctx = ctx_ckv.shape[2]
        ctx_kpe = jnp.pad(ctx_kpe, ((0, 0), (0, 0), (0, 0), (MLA_NOPE, MLA_HEAD_PAD - MLA_QK)))
        cos, sin = _rope_tables(n)
        in_specs += [
            pl.BlockSpec((1, 1, n_ctx, MLA_KV_RANK), lambda b: (b, 0, 0, 0)),
            pl.BlockSpec((1, 1, n_ctx, MLA_HEAD_PAD), lambda b: (b, 0, 0, 0)),
            _const_spec(cos.shape),
            _const_spec(sin.shape),
        ]
        args += [ctx_ckv, ctx_kpe, cos, sin]
    else:
        out_specs += [pl.BlockSpec((1, 1, n, MLA_KV_RANK), lambda b: (b, 0, 0, 0)),
                      pl.BlockSpec((1, 1, n, MLA_ROPE), lambda b: (b, 0, 0, 0))]
        out_shape += [jax.ShapeDtypeStruct((bsz, 1, n, MLA_KV_RANK), F32),
                      jax.ShapeDtypeStruct((bsz, 1, n, MLA_ROPE), F32)]
    kernel = functools.partial(_mla_kernel, row_of_batch=row_of_batch, has_ctx=has_ctx)
    res = pl.pallas_call(
        kernel,
        grid=(bsz,),
        in_specs=in_specs,
        out_specs=out_specs,
        out_shape=out_shape,
        scratch_shapes=[
            pltpu.VMEM((n, d), BF16),
            pltpu.VMEM((n, MLA_HEADS * MLA_HEAD_PAD), BF16),
            pltpu.VMEM((n_ctx + n, MLA_HEADS * MLA_HEAD_PAD), BF16),
            pltpu.VMEM((MLA_HEADS * MLA_VDIM, n_ctx + n), BF16),
        ],
        compiler_params=_compiler_params(1),
        name="mla_layer",
    )(*args)
    return res[0] if has_ctx else tuple(res)


def kernel(x_prompt, x_sample, state_hgrn, state_rglru, cache_mla_ckv, cache_mla_kpe, c, c_ctx, ada_w, ada_b, ln_g, ln_b, hg_w_in, hg_lb_logits, hg_norm_g, hg_w_out, sc_w_in, sc_conv_w, sc_conv_b, sc_w_out, rg_w_in, rg_conv_w, rg_conv_b, rg_w_gate, rg_b_gate, rg_lambda, rg_w_out, mla_w_in, mla_q_norm, mla_kv_norm, mla_w_qb, mla_w_kvb, mla_w_out):
    d = x_prompt.shape[-1]
    n_dec = c.shape[0]
    cond = jnp.concatenate([c_ctx[None], c, jnp.zeros((SUBLANES - 1 - n_dec, d), F32)], axis=0)
    mod = _modulation(cond, ada_w, ada_b)
    lng = ln_g.reshape(DEPTH, 1, d)
    lnb = ln_b.reshape(DEPTH, 1, d)
    hg_wi, hg_wo = hg_w_in[0].astype(BF16), hg_w_out[0].astype(BF16)
    sc_wi, sc_wo = sc_w_in[0].astype(BF16), sc_w_out[0].astype(BF16)
    rg_wi, rg_wg, rg_wo = rg_w_in[0].astype(BF16), rg_w_gate[0].astype(BF16), rg_w_out[0].astype(BF16)
    mla_w = _mla_weights(mla_w_in[0], mla_w_qb[0], mla_w_kvb[0], mla_w_out[0])

    def run(x, rob, st_hg, st_rg, ctx_ckv, ctx_kpe):
        r0 = _hgrn_layer(x, mod, 0, lng, lnb, hg_wi, hg_lb_logits, hg_norm_g, hg_wo, st_hg, row_of_batch=rob)
        x, new_hg = r0 if st_hg is None else (r0, None)
        x = _sconv_layer(x, mod, 1, lng, lnb, sc_wi, sc_conv_w[0], sc_conv_b, sc_wo, row_of_batch=rob)
        r2 = _rglru_layer(x, mod, 2, lng, lnb, rg_wi, rg_conv_w[0], rg_conv_b, rg_wg, rg_b_gate[0],
                          rg_lambda[0], rg_wo, st_rg, row_of_batch=rob)
        x, new_rg = r2 if st_rg is None else (r2, None)
        r3 = _mla_layer(x, mod, 3, lng, lnb, mla_w, mla_q_norm, mla_kv_norm, ctx_ckv, ctx_kpe,
                        row_of_batch=rob)
        if ctx_ckv is None:
            return (r3[0], new_hg, new_rg, r3[1], r3[2])
        return (r3,)

    y_prompt, new_hg, new_rg, new_ckv, new_kpe = run(x_prompt, False, None, None, None, None)
    (y_sample,) = run(x_sample, True, state_hgrn, state_rglru, cache_mla_ckv, cache_mla_kpe)
    return (y_prompt, y_sample, new_hg, new_rg, new_ckv, new_kpe)
```

```python
import functools

import jax
import jax.numpy as jnp
import numpy as np
from jax import lax
from jax.experimental import pallas as pl
from jax.experimental.pallas import tpu as pltpu

F32 = jnp.float32
BF16 = jnp.bfloat16

LANES = 128
SUBLANES = 8
VMEM_LIMIT_BYTES = 60 * 1024 * 1024

DEPTH = 4
DEEPNORM_ALPHA = (2 * DEPTH) ** 0.25
LN_EPS = 1e-5
RMS_EPS = 1e-6

ROW_TILE = 256

HG_HEADS = 8
HG_KDIM = 128
HG_VDIM = 128
HG_CHUNK = 32
SC_KERNEL = 3
RG_HEADS = 4
RG_KERNEL = 4
RG_C = 8.0
MLA_HEADS = 16
MLA_Q_RANK = 384
MLA_KV_RANK = 256
MLA_NOPE = 64
MLA_ROPE = 32
MLA_VDIM = 64
MLA_HEAD_PAD = 128
MLA_GROUP_SCORES = 1 << 21
ROPE_BASE = 10000.0
GRID_W = 64


def _mm(a, b):
    return jnp.dot(a.astype(BF16), b.astype(BF16), preferred_element_type=F32)


def _mm_nt(a, b):
    return lax.dot_general(a.astype(BF16), b.astype(BF16), (((1,), (1,)), ((), ())),
                           preferred_element_type=F32)


def _mm_tn(a, b):
    return lax.dot_general(a.astype(BF16), b.astype(BF16), (((0,), (0,)), ((), ())),
                           preferred_element_type=F32)


def _silu(x):
    return x * jax.nn.sigmoid(x)


def _mod_rows(mod_ref, row):
    if isinstance(row, int):
        return tuple(mod_ref[0, j, row:row + 1, :] for j in range(3))
    rows = lax.broadcasted_iota(jnp.int32, (SUBLANES, 1), 0)
    return tuple(jnp.sum(jnp.where(rows == row, mod_ref[0, j], 0.0), axis=0, keepdims=True)
                 for j in range(3))


def _residual_ln(x, y, gate, g, b):
    z = DEEPNORM_ALPHA * x + gate * y
    mu = jnp.mean(z, axis=-1, keepdims=True)
    zc = z - mu
    var = jnp.mean(zc * zc, axis=-1, keepdims=True)
    return zc * lax.rsqrt(var + LN_EPS) * g + b


def _const_spec(shape):
    nd = len(shape)
    return pl.BlockSpec(shape, lambda *_: (0,) * nd, pipeline_mode=pl.Buffered(1))


def _seq_spec(n, d, single_buffer=False):
    mode = dict(pipeline_mode=pl.Buffered(1)) if single_buffer else {}
    return pl.BlockSpec((1, n, d), lambda b: (b, 0, 0), **mode)


def _compiler_params(n_grid):
    return pltpu.CompilerParams(dimension_semantics=("arbitrary",) * n_grid,
                                vmem_limit_bytes=VMEM_LIMIT_BYTES)


def _mod_kernel(cond_ref, w_ref, b_ref, out_ref):
    s = _silu(cond_ref[...]).astype(BF16)
    d = cond_ref.shape[1]
    for j in range(3):
        out_ref[0, j] = _mm(s, w_ref[0, :, j * d:(j + 1) * d]) + b_ref[0, j]


def _modulation(cond, ada_w, ada_b):
    n_layers, d, _ = ada_w.shape
    rows = cond.shape[0]
    return pl.pallas_call(
        _mod_kernel,
        grid=(n_layers,),
        in_specs=[
            pl.BlockSpec((rows, d), lambda l: (0, 0)),
            pl.BlockSpec((1, d, 3 * d), lambda l: (l, 0, 0)),
            pl.BlockSpec((1, 3, 1, d), lambda l: (l, 0, 0, 0)),
        ],
        out_specs=pl.BlockSpec((1, 3, rows, d), lambda l: (l, 0, 0, 0)),
        out_shape=jax.ShapeDtypeStruct((n_layers, 3, rows, d), F32),
        compiler_params=_compiler_params(1),
        name="adaln_modulation",
    )(cond, ada_w, ada_b.reshape(n_layers, 3, 1, d))


def _sconv_kernel(x_ref, mod_ref, lng_ref, lnb_ref, w_in_ref, cw_ref, cb_ref, w_out_ref,
                  out_ref, u_scr, *, row_of_batch):
    n, d = x_ref.shape[1], x_ref.shape[2]
    width = w_out_ref.shape[0]
    row = pl.program_id(0) + 1 if row_of_batch else 0
    shift, scale, gate = _mod_rows(mod_ref, row)
    x = x_ref[0]
    h = (x * (1.0 + scale) + shift).astype(BF16)
    t = lax.broadcasted_iota(jnp.int32, (n, 1), 0)
    chunk = width if n <= ROW_TILE else 2 * LANES

    def project(j):
        return tuple(_mm(h, w_in_ref[:, i * width + j * chunk:i * width + (j + 1) * chunk]) for i in range(4))

    nxt = project(0)
    for j in range(width // chunk):
        c0 = j * chunk
        bg, cg, v, g = nxt
        if j + 1 < width // chunk:
            nxt = project(j + 1)
        p = cg * v
        p_prev = jnp.where(t >= 1, pltpu.roll(p, 1, 0), 0.0)
        p_next = jnp.where(t < n - 1, pltpu.roll(p, n - 1, 0), 0.0)
        z = (cw_ref[0:1, c0:c0 + chunk] * p_prev + cw_ref[1:2, c0:c0 + chunk] * p
             + cw_ref[2:3, c0:c0 + chunk] * p_next + cb_ref[:, c0:c0 + chunk])
        u_scr[:, c0:c0 + chunk] = (_silu(g) * bg * z).astype(BF16)
    y = _mm(u_scr[...], w_out_ref[...])
    out_ref[0] = _residual_ln(x, y, gate, lng_ref[0], lnb_ref[0])


def _sconv_layer(x, mod, layer, ln_g, ln_b, w_in, conv_w, conv_b, w_out, *, row_of_batch):
    bsz, n, d = x.shape
    width = w_out.shape[0]
    kernel = functools.partial(_sconv_kernel, row_of_batch=row_of_batch)
    return pl.pallas_call(
        kernel,
        grid=(bsz,),
        in_specs=[
            _seq_spec(n, d),
            pl.BlockSpec((1, 3, SUBLANES, d), lambda b: (layer, 0, 0, 0)),
            pl.BlockSpec((1, 1, d), lambda b: (layer, 0, 0)),
            pl.BlockSpec((1, 1, d), lambda b: (layer, 0, 0)),
            _const_spec(w_in.shape),
            _const_spec(conv_w.shape),
            _const_spec(conv_b.shape),
            _const_spec(w_out.shape),
        ],
        out_specs=_seq_spec(n, d),
        out_shape=jax.ShapeDtypeStruct((bsz, n, d), F32),
        scratch_shapes=[pltpu.VMEM((n, width), BF16)],
        compiler_params=_compiler_params(1),
        name="sconv_layer",
    )(x, mod, ln_g, ln_b, w_in, conv_w, conv_b, w_out)


def _chunk_cumsum(x, reverse):
    n = x.shape[0]
    pos = lax.broadcasted_iota(jnp.int32, (n, 1), 0) % HG_CHUNK
    s = 1
    while s < HG_CHUNK:
        if reverse:
            x = x + jnp.where(pos < HG_CHUNK - s, pltpu.roll(x, n - s, 0), 0.0)
        else:
            x = x + jnp.where(pos >= s, pltpu.roll(x, s, 0), 0.0)
        s *= 2
    return x


def _hgrn_kernel(*refs, row_of_batch, layer, has_s0, emit_state, n_groups):
    it = iter(refs)
    x_ref, mod_ref, lng_ref, lnb_ref = (next(it) for _ in range(4))
    wq_ref, wf_fw_ref, wf_bw_ref, wv_ref, wg_ref = (next(it) for _ in range(5))
    lbl_ref, ng_ref, w_out_ref = (next(it) for _ in range(3))
    s0_ref = next(it) if has_s0 else None
    out_ref = next(it)
    s_out_ref = next(it) if emit_state else None
    h_scr, qrel_scr, krel_scr, qp_scr, kp_scr, v_scr, dec_scr, o_scr, g_scr, u_scr = it

    n, d = x_ref.shape[1], x_ref.shape[2]
    hk = w_out_ref.shape[0]
    n_tiles = n // ROW_TILE
    chunks_per_tile = ROW_TILE // HG_CHUNK
    n_chunks = n // HG_CHUNK
    pair = 2 * LANES

    row = pl.program_id(0) + 1 if row_of_batch else 0
    shift, scale, gate = _mod_rows(mod_ref, row)
    x = x_ref[0]
    h_scr[...] = (x * (1.0 + scale) + shift).astype(BF16)

    lbs = []
    for dr in range(2):
        z = lbl_ref[dr]
        e = jnp.exp(z - jnp.max(z, axis=0, keepdims=True))
        lbs.append(jnp.sum(e[:layer + 1], axis=0, keepdims=True) / jnp.sum(e, axis=0, keepdims=True))

    ri = lax.broadcasted_iota(jnp.int32, (ROW_TILE, ROW_TILE), 0)
    ci = lax.broadcasted_iota(jnp.int32, (ROW_TILE, ROW_TILE), 1)
    same_chunk = (ri // HG_CHUNK) == (ci // HG_CHUNK)
    masks = (same_chunk & (ci <= ri), same_chunk & (ci >= ri))

    row_chunk = (lax.broadcasted_iota(jnp.int32, (ROW_TILE, LANES), 0) // HG_CHUNK).astype(BF16)

    def chunk_blocks(a):
        zero = jnp.zeros_like(a)
        return jnp.concatenate([jnp.where(row_chunk == c, a, zero) for c in range(chunks_per_tile)], axis=1)

    def project(hp):
        c0 = hp * pair
        slot = hp % 2
        h = h_scr[...]
        q = _silu(_mm(h, wq_ref[:, c0:c0 + pair])) * (HG_KDIM ** -0.5)
        v_scr[slot] = _mm(h, wv_ref[:, c0:c0 + pair]).astype(BF16)
        q3 = q.reshape(n_chunks, HG_CHUNK, pair)
        for dr in range(2):
            fl = _mm(h, (wf_fw_ref, wf_bw_ref)[dr][:, c0:c0 + pair])
            lb = lbs[dr][:, c0:c0 + pair]
            f = lb + (1.0 - lb) * jax.nn.sigmoid(fl)
            k3 = (1.0 - f).reshape(n_chunks, HG_CHUNK, pair)
            cum = _chunk_cumsum(jnp.log(f), reverse=(dr == 1)).reshape(n_chunks, HG_CHUNK, pair)
            mid = HG_CHUNK // 2 - 1
            if dr == 0:
                ref, tot = cum[:, mid:mid + 1], cum[:, HG_CHUNK - 1:HG_CHUNK]
            else:
                ref, tot = cum[:, HG_CHUNK - 1 - mid:HG_CHUNK - mid], cum[:, 0:1]
            q_rel = q3 * jnp.exp(cum - ref)
            k_rel = k3 * jnp.exp(ref - cum)
            qrel_scr[slot, dr] = q_rel.reshape(n, pair).astype(BF16)
            krel_scr[slot, dr] = k_rel.reshape(n, pair).astype(BF16)
            qp_scr[slot, dr] = (q_rel * jnp.exp(ref)).reshape(n, pair).astype(BF16)
            kp_scr[slot, dr] = (k_rel * jnp.exp(tot - ref)).reshape(n, pair).astype(BF16)
            dec_scr[slot, dr] = jnp.exp(tot)
        g_scr[slot] = _silu(_mm(h, wg_ref[:, c0:c0 + pair]))

    def recur(hp):
        c0 = hp * pair
        slot = hp % 2
        for dr in range(2):
            for h2 in range(2):
                l0 = h2 * LANES
                head = hp * 2 + h2
                if has_s0:
                    s_t0 = s0_ref[0, 0, dr, head].T
                else:
                    s_t0 = jnp.zeros((HG_VDIM, HG_KDIM), F32)

                def tile_body(i, s_t, dr=dr, l0=l0, slot=slot):
                    ti = i if dr == 0 else n_tiles - 1 - i
                    rows = pl.ds(pl.multiple_of(ti * ROW_TILE, ROW_TILE), ROW_TILE)
                    lanes = slice(l0, l0 + LANES)
                    v = v_scr[slot, rows, lanes]
                    att = _mm_nt(qrel_scr[slot, dr, rows, lanes], krel_scr[slot, dr, rows, lanes])
                    o_tile = _mm(jnp.where(masks[dr], att, 0.0), v)
                    upd = _mm_tn(v, chunk_blocks(kp_scr[slot, dr, rows, lanes]))
                    s_prev = [None] * chunks_per_tile
                    order = range(chunks_per_tile) if dr == 0 else range(chunks_per_tile - 1, -1, -1)
                    for c in order:
                        s_prev[c] = s_t.astype(BF16)
                        dec = dec_scr[slot, dr, ti * chunks_per_tile + c][:, lanes]
                        s_t = s_t * dec + upd[:, c * LANES:(c + 1) * LANES]
                    o_tile += _mm_nt(chunk_blocks(qp_scr[slot, dr, rows, lanes]),
                                     jnp.concatenate(s_prev, axis=1))
                    if dr == 0:
                        o_scr[slot, rows, lanes] = o_tile
                    else:
                        o_scr[slot, rows, lanes] += o_tile
                    return s_t

                s_t = s_t0
                for i in range(n_tiles):
                    s_t = tile_body(i, s_t)
                if emit_state:
                    s_out_ref[0, 0, dr, head] = s_t.T

        for h2 in range(2):
            lanes = slice(h2 * LANES, (h2 + 1) * LANES)
            o = o_scr[slot, :, lanes]
            ms = jnp.mean(o * o, axis=-1, keepdims=True)
            on = o * lax.rsqrt(ms + RMS_EPS) * ng_ref[:, c0 + h2 * LANES:c0 + (h2 + 1) * LANES]
            u_scr[:, c0 + h2 * LANES:c0 + (h2 + 1) * LANES] = (on * g_scr[slot, :, lanes]).astype(BF16)

    n_pairs = hk // pair
    project(0)
    for hp in range(n_pairs):
        if hp + 1 < n_pairs:
            project(hp + 1)
        recur(hp)

    y = _mm(u_scr[...], w_out_ref[...])
    if n_groups == 1:
        out_ref[0] = _residual_ln(x, y, gate, lng_ref[0], lnb_ref[0])
    else:
        grp = pl.program_id(1)

        @pl.when(grp == 0)
        def _():
            out_ref[0] = y

        @pl.when((grp > 0) & (grp < n_groups - 1))
        def _():
            out_ref[0] += y

        @pl.when(grp == n_groups - 1)
        def _():
            out_ref[0] = _residual_ln(x, out_ref[0] + y, gate, lng_ref[0], lnb_ref[0])


def _hgrn_layer(x, mod, layer, ln_g, ln_b, w_in, lb_logits, norm_g, w_out, state, *, row_of_batch):
    bsz, n, d = x.shape
    hv = w_out.shape[0]
    has_s0 = state is not None
    emit_state = not has_s0
    pair = 2 * LANES
    n_groups = 2 if n > ROW_TILE else 1
    gcols = hv // n_groups
    heads = HG_HEADS // n_groups
    n_proj = w_in.shape[1] // hv
    state_block = (1, 1, 2, heads, HG_KDIM, HG_VDIM)
    state_spec = pl.BlockSpec(state_block, lambda b, g: (b, 0, 0, g, 0, 0))
    seq_spec = pl.BlockSpec((1, n, d), lambda b, g: (b, 0, 0))
    wmode = dict(pipeline_mode=pl.Buffered(1)) if n_groups == 1 else {}
    in_specs = [
        seq_spec,
        pl.BlockSpec((1, 3, SUBLANES, d), lambda b, g: (layer, 0, 0, 0)),
        pl.BlockSpec((1, 1, d), lambda b, g: (layer, 0, 0)),
        pl.BlockSpec((1, 1, d), lambda b, g: (layer, 0, 0)),
        *[pl.BlockSpec((d, gcols), functools.partial(lambda b, g, i: (0, i * n_groups + g), i=i), **wmode)
          for i in range(n_proj)],
        pl.BlockSpec(lb_logits.shape[:2] + (gcols,), lambda b, g: (0, 0, g), **wmode),
        pl.BlockSpec((1, gcols), lambda b, g: (0, g), **wmode),
        pl.BlockSpec((gcols, d), lambda b, g: (g, 0), **wmode),
    ]
    args = [x, mod, ln_g, ln_b] + [w_in] * n_proj + [lb_logits, norm_g, w_out]
    out_specs = [seq_spec]
    out_shape = [jax.ShapeDtypeStruct((bsz, n, d), F32)]
    if has_s0:
        in_specs.append(state_spec)
        args.append(state)
    if emit_state:
        out_specs.append(state_spec)
        out_shape.append(jax.ShapeDtypeStruct((bsz, 1, 2, HG_HEADS, HG_KDIM, HG_VDIM), F32))
    kernel = functools.partial(_hgrn_kernel, row_of_batch=row_of_batch, layer=layer,
                               has_s0=has_s0, emit_state=emit_state, n_groups=n_groups)
    res = pl.pallas_call(
        kernel,
        grid=(bsz, n_groups),
        in_specs=in_specs,
        out_specs=out_specs,
        out_shape=out_shape,
        scratch_shapes=[
            pltpu.VMEM((n, d), BF16),
            pltpu.VMEM((2, 2, n, pair), BF16),
            pltpu.VMEM((2, 2, n, pair), BF16),
            pltpu.VMEM((2, 2, n, pair), BF16),
            pltpu.VMEM((2, 2, n, pair), BF16),
            pltpu.VMEM((2, n, pair), BF16),
            pltpu.VMEM((2, 2, n // HG_CHUNK, 1, pair), F32),
            pltpu.VMEM((2, n, pair), F32),
            pltpu.VMEM((2, n, pair), F32),
            pltpu.VMEM((n, gcols), BF16),
        ],
        compiler_params=_compiler_params(2),
        name="hgrn2_layer",
    )(*args)
    return res[0] if has_s0 else tuple(res)


def _block_scan(a, x, reverse):
    n, c = a.shape
    nb = n // SUBLANES
    a = a.reshape(nb, SUBLANES, c)
    x = x.reshape(nb, SUBLANES, c)
    pos = lax.broadcasted_iota(jnp.int32, (1, SUBLANES, 1), 1)
    s = 1
    while s < SUBLANES:
        if reverse:
            keep = pos < SUBLANES - s
            shift = SUBLANES - s
        else:
            keep = pos >= s
            shift = s
        a_in = jnp.where(keep, pltpu.roll(a, shift, 1), 1.0)
        x_in = jnp.where(keep, pltpu.roll(x, shift, 1), 0.0)
        x = a * x_in + x
        a = a * a_in
        s *= 2
    return a.reshape(n, c), x.reshape(n, c)


def _rglru_kernel(*refs, row_of_batch, has_s0, emit_state):
    it = iter(refs)
    (x_ref, mod_ref, lng_ref, lnb_ref, w_in_ref, cw_ref, cb_ref, wg_ref, bg_ref, lam_ref,
     w_out_ref) = (next(it) for _ in range(11))
    s0_ref = next(it) if has_s0 else None
    out_ref = next(it)
    s_out_ref = next(it) if emit_state else None
    h_scr, a_scr, b_scr, gs_scr = it

    n, d = x_ref.shape[1], x_ref.shape[2]
    width = w_out_ref.shape[0]
    blk = width // RG_HEADS
    nb = n // SUBLANES

    row = pl.program_id(0) + 1 if row_of_batch else 0
    shift, scale, gate = _mod_rows(mod_ref, row)
    h_scr[...] = (x_ref[0] * (1.0 + scale) + shift).astype(BF16)

    z = -lam_ref[...]
    coef = -RG_C * (jnp.maximum(z, 0.0) + jnp.log1p(jnp.exp(-jnp.abs(z))))

    t = lax.broadcasted_iota(jnp.int32, (n, 1), 0)

    def project(hh):
        h = h_scr[...]
        return (_mm(h, w_in_ref[:, hh * blk:(hh + 1) * blk]),
                _mm(h, w_in_ref[:, width + hh * blk:width + (hh + 1) * blk]))

    def mix(hh, u_pre, g):
        cols = slice(hh * blk, (hh + 1) * blk)
        u = (cw_ref[0:1, cols] * jnp.where(t >= 2, pltpu.roll(u_pre, 2, 0), 0.0)
             + cw_ref[1:2, cols] * jnp.where(t >= 1, pltpu.roll(u_pre, 1, 0), 0.0)
             + cw_ref[2:3, cols] * u_pre
             + cw_ref[3:4, cols] * jnp.where(t < n - 1, pltpu.roll(u_pre, n - 1, 0), 0.0)
             + cb_ref[:, cols])
        gs_scr[:, cols] = _silu(g)
        gates = [jax.nn.sigmoid(_mm(u, wg_ref[dr, hh]) + bg_ref[dr, hh]) for dr in range(2)]
        for dr in range(2):
            a = jnp.exp(coef[dr:dr + 1, cols] * gates[dr][:, :blk])
            xin = jnp.exp(0.5 * jnp.log(1.0 - a * a)) * (gates[dr][:, blk:] * u)
            a_blk, h_blk = _block_scan(a, xin, reverse=(dr == 1))
            a_scr[dr, :, cols] = a_blk
            b_scr[dr, :, cols] = h_blk

    nxt = project(0)
    for hh in range(RG_HEADS):
        cur = nxt
        if hh + 1 < RG_HEADS:
            nxt = project(hh + 1)
        mix(hh, *cur)

    if has_s0:
        carry0 = (s0_ref[0, 0, 0:1, :], s0_ref[0, 0, 1:2, :])
    else:
        carry0 = (jnp.zeros((1, width), F32), jnp.zeros((1, width), F32))

    def carry_body(i, carry):
        cf, cb = carry
        rf = pl.ds(pl.multiple_of(i * SUBLANES, SUBLANES), SUBLANES)
        rb = pl.ds(pl.multiple_of((nb - 1 - i) * SUBLANES, SUBLANES), SUBLANES)
        hf = a_scr[0, rf, :] * cf + b_scr[0, rf, :]
        hb = a_scr[1, rb, :] * cb + b_scr[1, rb, :]
        b_scr[0, rf, :] = hf
        b_scr[1, rb, :] = hb
        return hf[SUBLANES - 1:SUBLANES, :], hb[0:1, :]

    cf, cb = carry0
    for i in range(nb):
        cf, cb = carry_body(i, (cf, cb))
    if emit_state:
        s_out_ref[0, 0, 0:1, :] = cf
        s_out_ref[0, 0, 1:2, :] = cb

    for r0 in range(0, n, ROW_TILE):
        rows = slice(r0, r0 + ROW_TILE)
        mix = ((b_scr[0, rows, :] + b_scr[1, rows, :]) * gs_scr[rows, :]).astype(BF16)
        y = _mm(mix, w_out_ref[...])
        out_ref[0, rows, :] = _residual_ln(x_ref[0, rows, :], y, gate, lng_ref[0], lnb_ref[0])


def _rglru_layer(x, mod, layer, ln_g, ln_b, w_in, conv_w, conv_b, w_gate, b_gate, lam, w_out, state,
                 *, row_of_batch):
    bsz, n, d = x.shape
    width = w_out.shape[0]
    has_s0 = state is not None
    emit_state = not has_s0
    b_gate = b_gate.reshape(2, RG_HEADS, 1, b_gate.shape[-1])
    state_block = (1, 1, 2, width)
    state_spec = pl.BlockSpec(state_block, lambda b: (b, 0, 0, 0))
    in_specs = [
        _seq_spec(n, d),
        pl.BlockSpec((1, 3, SUBLANES, d), lambda b: (layer, 0, 0, 0)),
        pl.BlockSpec((1, 1, d), lambda b: (layer, 0, 0)),
        pl.BlockSpec((1, 1, d), lambda b: (layer, 0, 0)),
        _const_spec(w_in.shape),
        _const_spec(conv_w.shape),
        _const_spec(conv_b.shape),
        _const_spec(w_gate.shape),
        _const_spec(b_gate.shape),
        _const_spec(lam.shape),
        _const_spec(w_out.shape),
    ]
    args = [x, mod, ln_g, ln_b, w_in, conv_w, conv_b, w_gate, b_gate, lam, w_out]
    out_specs = [_seq_spec(n, d)]
    out_shape = [jax.ShapeDtypeStruct((bsz, n, d), F32)]
    if has_s0:
        in_specs.append(state_spec)
        args.append(state)
    if emit_state:
        out_specs.append(state_spec)
        out_shape.append(jax.ShapeDtypeStruct((bsz,) + state_block[1:], F32))
    kernel = functools.partial(_rglru_kernel, row_of_batch=row_of_batch, has_s0=has_s0,
                               emit_state=emit_state)
    res = pl.pallas_call(
        kernel,
        grid=(bsz,),
        in_specs=in_specs,
        out_specs=out_specs,
        out_shape=out_shape,
        scratch_shapes=[
            pltpu.VMEM((n, d), BF16),
            pltpu.VMEM((2, n, width), F32),
            pltpu.VMEM((2, n, width), F32),
            pltpu.VMEM((n, width), F32),
        ],
        compiler_params=_compiler_params(1),
        name="rglru_layer",
    )(*args)
    return res[0] if has_s0 else tuple(res)


MLA_QK = MLA_NOPE + MLA_ROPE


def _mla_weights(w_in, w_qb, w_kvb, w_out):
    w_in, w_qb, w_kvb, w_out = (w.astype(BF16) for w in (w_in, w_qb, w_kvb, w_out))
    d = w_in.shape[0]
    a = MLA_Q_RANK + MLA_KV_RANK
    pad_l, pad_r = MLA_NOPE, MLA_HEAD_PAD - MLA_QK
    w_in_r = jnp.concatenate(
        [w_in[:, :a], jnp.zeros((d, pad_l), w_in.dtype), w_in[:, a:a + MLA_ROPE],
         jnp.zeros((d, pad_r), w_in.dtype), w_in[:, a + MLA_ROPE:]], axis=1)
    wq = w_qb.reshape(MLA_Q_RANK, MLA_HEADS, MLA_QK)
    wq = jnp.pad(wq, ((0, 0), (0, 0), (0, pad_r))).reshape(MLA_Q_RANK, MLA_HEADS * MLA_HEAD_PAD)
    wkv = w_kvb.reshape(MLA_KV_RANK, MLA_HEADS, MLA_NOPE + MLA_VDIM)
    wk = jnp.pad(wkv[:, :, :MLA_NOPE], ((0, 0), (0, 0), (0, MLA_HEAD_PAD - MLA_NOPE)))
    wk = wk.reshape(MLA_KV_RANK, MLA_HEADS * MLA_HEAD_PAD)
    wv_t = wkv[:, :, MLA_NOPE:].reshape(MLA_KV_RANK, MLA_HEADS * MLA_VDIM).T
    return w_in_r, wq, wk, wv_t, w_out


def _rope_tables(n):
    half = MLA_ROPE // 2
    quarter = half // 2
    t = np.arange(n)
    inv_freq = ROPE_BASE ** (-np.arange(0, half, 2, dtype=np.float32) / half)
    cos = np.ones((n, MLA_HEAD_PAD), np.float32)
    sin = np.zeros((n, MLA_HEAD_PAD), np.float32)
    for k, pos in enumerate((t // GRID_W, t % GRID_W)):
        ang = pos[:, None].astype(np.float32) * inv_freq[None].astype(np.float32)
        c, s = np.cos(ang), np.sin(ang)
        base = MLA_NOPE + k * half
        cos[:, base:base + quarter] = c
        cos[:, base + quarter:base + half] = c
        sin[:, base:base + quarter] = -s
        sin[:, base + quarter:base + half] = s
    return jnp.asarray(cos), jnp.asarray(sin)


def _rope(x, cos, sin):
    quarter = MLA_ROPE // 4
    lane = lax.broadcasted_iota(jnp.int32, (1, MLA_HEAD_PAD), 1)
    first = ((lane - MLA_NOPE) % (2 * quarter)) < quarter
    partner = jnp.where(first, pltpu.roll(x, MLA_HEAD_PAD - quarter, 1), pltpu.roll(x, quarter, 1))
    return x * cos + partner * sin


def _rms(x, g):
    return x * lax.rsqrt(jnp.mean(x * x, axis=-1, keepdims=True) + RMS_EPS) * g


def _mla_kernel(*refs, row_of_batch, has_ctx):
    it = iter(refs)
    (x_ref, mod_ref, lng_ref, lnb_ref, w_in_ref, qn_ref, kvn_ref, wq_ref, wk_ref, wv_ref,
     w_out_ref) = (next(it) for _ in range(11))
    if has_ctx:
        ctx_ckv_ref, ctx_kpe_ref, cos_ref, sin_ref = (next(it) for _ in range(4))
    out_ref = next(it)
    if not has_ctx:
        ckv_out_ref, kpe_out_ref = next(it), next(it)
    h_scr, q_scr, k_scr, v_scr = it

    n = x_ref.shape[1]
    n_ctx = k_scr.shape[0] - n
    hp = MLA_HEAD_PAD
    c_ckv = MLA_Q_RANK
    c_kpe = c_ckv + MLA_KV_RANK
    c_g = c_kpe + hp

    row = pl.program_id(0) + 1 if row_of_batch else 0
    shift, scale, gate = _mod_rows(mod_ref, row)
    sm_scale = MLA_QK ** -0.5
    n_tiles = n // ROW_TILE
    group = 2
    while group < MLA_HEADS and 2 * group * ROW_TILE * (n_ctx + n) <= MLA_GROUP_SCORES:
        group *= 2

    def store_keys(ckv, kpe, krows):
        k_all = _mm(ckv, wk_ref[...])
        v_scr[:, krows] = _mm_nt(wv_ref[...], ckv).astype(BF16)
        for hd in range(MLA_HEADS):
            cols = slice(hd * hp, (hd + 1) * hp)
            k_scr[krows, cols] = (k_all[:, cols] + kpe).astype(BF16)

    def proj_body(ti):
        r0 = ti * ROW_TILE
        rows = slice(r0, r0 + ROW_TILE)
        h = (x_ref[0, rows, :] * (1.0 + scale) + shift).astype(BF16)
        h_scr[rows, :] = h
        cq = _rms(_mm(h, w_in_ref[:, :c_ckv]), qn_ref[...])
        ckv = _rms(_mm(h, w_in_ref[:, c_ckv:c_kpe]), kvn_ref[...])
        kpe = _mm(h, w_in_ref[:, c_kpe:c_g])
        if has_ctx:
            cos, sin = cos_ref[rows, :], sin_ref[rows, :]
            kpe = _rope(kpe, cos, sin)
        else:
            ckv_out_ref[0, 0, rows, :] = ckv
            kpe_out_ref[0, 0, rows, :] = kpe[:, MLA_NOPE:MLA_QK]
        q_all = _mm(cq, wq_ref[...])
        for hd in range(MLA_HEADS):
            cols = slice(hd * hp, (hd + 1) * hp)
            qh = q_all[:, cols]
            if has_ctx:
                qh = _rope(qh, cos, sin)
            q_scr[rows, cols] = (qh * sm_scale).astype(BF16)
        store_keys(ckv, kpe, slice(n_ctx + r0, n_ctx + r0 + ROW_TILE))

    def ctx_body(ti):
        rows = slice(ti * ROW_TILE, (ti + 1) * ROW_TILE)
        store_keys(ctx_ckv_ref[0, 0, rows, :], ctx_kpe_ref[0, 0, rows, :], rows)

    for ti in range(n_tiles):
        proj_body(ti)
    for ti in range(n_ctx // ROW_TILE):
        ctx_body(ti)

    def tile_body(ti, carry):
        rows = pl.ds(pl.multiple_of(ti * ROW_TILE, ROW_TILE), ROW_TILE)
        o_t = []
        for g0 in range(0, MLA_HEADS, group):
            heads = range(g0, g0 + group)
            ss = [_mm_nt(k_scr[:, hd * hp:(hd + 1) * hp], q_scr[rows, hd * hp:(hd + 1) * hp]) for hd in heads]
            if g0 == 0:
                g = _silu(_mm(h_scr[rows, :], w_in_ref[:, c_g:]))
            ms = [jnp.max(s, axis=0, keepdims=True) for s in ss]
            es = [jnp.exp(s - m) for s, m in zip(ss, ms)]
            ls = [jnp.sum(e, axis=0, keepdims=True) for e in es]
            o_t += [_mm(v_scr[hd * MLA_VDIM:(hd + 1) * MLA_VDIM, :], e) / l for e, l, hd in zip(es, ls, heads)]
        o = jnp.concatenate(o_t, axis=0).T
        y = _mm(o * g, w_out_ref[...])
        out_ref[0, rows, :] = _residual_ln(x_ref[0, rows, :], y, gate, lng_ref[0], lnb_ref[0])
        return carry

    if n_tiles == 1:
        tile_body(0, 0)
    else:
        lax.fori_loop(0, n_tiles, tile_body, 0)


def _mla_layer(x, mod, layer, ln_g, ln_b, weights, q_norm, kv_norm, ctx_ckv, ctx_kpe, *, row_of_batch):
    bsz, n, d = x.shape
    w_in, wq, wk, wv, w_out = weights
    has_ctx = ctx_ckv is not None
    in_specs = [
        _seq_spec(n, d),
        pl.BlockSpec((1, 3, SUBLANES, d), lambda b: (layer, 0, 0, 0)),
        pl.BlockSpec((1, 1, d), lambda b: (layer, 0, 0)),
        pl.BlockSpec((1, 1, d), lambda b: (layer, 0, 0)),
        _const_spec(w_in.shape),
        _const_spec(q_norm.shape),
        _const_spec(kv_norm.shape),
        _const_spec(wq.shape),
        _const_spec(wk.shape),
        _const_spec(wv.shape),
        _const_spec(w_out.shape),
    ]
    args = [x, mod, ln_g, ln_b, w_in, q_norm, kv_norm, wq, wk, wv, w_out]
    out_specs = [_seq_spec(n, d, single_buffer=has_ctx)]
    out_shape = [jax.ShapeDtypeStruct((bsz, n, d), F32)]
    n_ctx = 0
    if has_ctx:
        n_ctx = ctx_ckv.shape[2]
        ctx_kpe = jnp.pad(ctx_kpe, ((0, 0), (0, 0), (0, 0), (MLA_NOPE, MLA_HEAD_PAD - MLA_QK)))
        cos, sin = _rope_tables(n)
        in_specs += [
            pl.BlockSpec((1, 1, n_ctx, MLA_KV_RANK), lambda b: (b, 0, 0, 0)),
            pl.BlockSpec((1, 1, n_ctx, MLA_HEAD_PAD), lambda b: (b, 0, 0, 0)),
            _const_spec(cos.shape),
            _const_spec(sin.shape),
        ]
        args += [ctx_ckv, ctx_kpe, cos, sin]
    else:
        out_specs += [pl.BlockSpec((1, 1, n, MLA_KV_RANK), lambda b: (b, 0, 0, 0)),
                      pl.BlockSpec((1, 1, n, MLA_ROPE), lambda b: (b, 0, 0, 0))]
        out_shape += [jax.ShapeDtypeStruct((bsz, 1, n, MLA_KV_RANK), F32),
                      jax.ShapeDtypeStruct((bsz, 1, n, MLA_ROPE), F32)]
    kernel = functools.partial(_mla_kernel, row_of_batch=row_of_batch, has_ctx=has_ctx)
    res = pl.pallas_call(
        kernel,
        grid=(bsz,),
        in_specs=in_specs,
        out_specs=out_specs,
        out_shape=out_shape,
        scratch_shapes=[
            pltpu.VMEM((n, d), BF16),
            pltpu.VMEM((n, MLA_HEADS * MLA_HEAD_PAD), BF16),
            pltpu.VMEM((n_ctx + n, MLA_HEADS * MLA_HEAD_PAD), BF16),
            pltpu.VMEM((MLA_HEADS * MLA_VDIM, n_ctx + n), BF16),
        ],
        compiler_params=_compiler_params(1),
        name="mla_layer",
    )(*args)
    return res[0] if has_ctx else tuple(res)


def kernel(x_prompt, x_sample, state_hgrn, state_rglru, cache_mla_ckv, cache_mla_kpe, c, c_ctx, ada_w, ada_b, ln_g, ln_b, hg_w_in, hg_lb_logits, hg_norm_g, hg_w_out, sc_w_in, sc_conv_w, sc_conv_b, sc_w_out, rg_w_in, rg_conv_w, rg_conv_b, rg_w_gate, rg_b_gate, rg_lambda, rg_w_out, mla_w_in, mla_q_norm, mla_kv_norm, mla_w_qb, mla_w_kvb, mla_w_out):
    d = x_prompt.shape[-1]
    n_dec = c.shape[0]
    cond = jnp.concatenate([c_ctx[None], c, jnp.zeros((SUBLANES - 1 - n_dec, d), F32)], axis=0)
    mod = _modulation(cond, ada_w, ada_b)
    lng = ln_g.reshape(DEPTH, 1, d)
    lnb = ln_b.reshape(DEPTH, 1, d)
    hg_wi, hg_wo = hg_w_in[0].astype(BF16), hg_w_out[0].astype(BF16)
    sc_wi, sc_wo = sc_w_in[0].astype(BF16), sc_w_out[0].astype(BF16)
    rg_wi, rg_wg, rg_wo = rg_w_in[0].astype(BF16), rg_w_gate[0].astype(BF16), rg_w_out[0].astype(BF16)
    mla_w = _mla_weights(mla_w_in[0], mla_w_qb[0], mla_w_kvb[0], mla_w_out[0])

    def run(x, rob, st_hg, st_rg, ctx_ckv, ctx_kpe):
        r0 = _hgrn_layer(x, mod, 0, lng, lnb, hg_wi, hg_lb_logits, hg_norm_g, hg_wo, st_hg, row_of_batch=rob)
        x, new_hg = r0 if st_hg is None else (r0, None)
        x = _sconv_layer(x, mod, 1, lng, lnb, sc_wi, sc_conv_w[0], sc_conv_b, sc_wo, row_of_batch=rob)
        r2 = _rglru_layer(x, mod, 2, lng, lnb, rg_wi, rg_conv_w[0], rg_conv_b, rg_wg, rg_b_gate[0],
                          rg_lambda[0], rg_wo, st_rg, row_of_batch=rob)
        x, new_rg = r2 if st_rg is None else (r2, None)
        r3 = _mla_layer(x, mod, 3, lng, lnb, mla_w, mla_q_norm, mla_kv_norm, ctx_ckv, ctx_kpe,
                        row_of_batch=rob)
        if ctx_ckv is None:
            return (r3[0], new_hg, new_rg, r3[1], r3[2])
        return (r3,)

    y_prompt, new_hg, new_rg, new_ckv, new_kpe = run(x_prompt, False, None, None, None, None)
    (y_sample,) = run(x_sample, True, state_hgrn, state_rglru, cache_mla_ckv, cache_mla_kpe)
    return (y_prompt, y_sample, new_hg, new_rg, new_ckv, new_kpe)
```

```python
import functools

import jax
import jax.numpy as jnp
import numpy as np
from jax import lax
from jax.experimental import pallas as pl
from jax.experimental.pallas import tpu as pltpu

F32 = jnp.float32
BF16 = jnp.bfloat16

LANES = 128
SUBLANES = 8
VMEM_LIMIT_BYTES = 60 * 1024 * 1024

DEPTH = 4
DEEPNORM_ALPHA = (2 * DEPTH) ** 0.25
LN_EPS = 1e-5
RMS_EPS = 1e-6

ROW_TILE = 256

HG_HEADS = 8
HG_KDIM = 128
HG_VDIM = 128
HG_CHUNK = 32
SC_KERNEL = 3
RG_HEADS = 4
RG_KERNEL = 4
RG_C = 8.0
MLA_HEADS = 16
MLA_Q_RANK = 384
MLA_KV_RANK = 256
MLA_NOPE = 64
MLA_ROPE = 32
MLA_VDIM = 64
MLA_HEAD_PAD = 128
MLA_GROUP_SCORES = 1 << 21
ROPE_BASE = 10000.0
GRID_W = 64


def _mm(a, b):
    return jnp.dot(a.astype(BF16), b.astype(BF16), preferred_element_type=F32)


def _mm_nt(a, b):
    return lax.dot_general(a.astype(BF16), b.astype(BF16), (((1,), (1,)), ((), ())),
                           preferred_element_type=F32)


def _mm_tn(a, b):
    return lax.dot_general(a.astype(BF16), b.astype(BF16), (((0,), (0,)), ((), ())),
                           preferred_element_type=F32)


def _silu(x):
    return x * jax.nn.sigmoid(x)


def _mod_rows(mod_ref, row):
    if isinstance(row, int):
        return tuple(mod_ref[0, j, row:row + 1, :] for j in range(3))
    rows = lax.broadcasted_iota(jnp.int32, (SUBLANES, 1), 0)
    return tuple(jnp.sum(jnp.where(rows == row, mod_ref[0, j], 0.0), axis=0, keepdims=True)
                 for j in range(3))


def _residual_ln(x, y, gate, g, b):
    z = DEEPNORM_ALPHA * x + gate * y
    mu = jnp.mean(z, axis=-1, keepdims=True)
    zc = z - mu
    var = jnp.mean(zc * zc, axis=-1, keepdims=True)
    return zc * lax.rsqrt(var + LN_EPS) * g + b


def _const_spec(shape):
    nd = len(shape)
    return pl.BlockSpec(shape, lambda *_: (0,) * nd, pipeline_mode=pl.Buffered(1))


def _seq_spec(n, d, single_buffer=False):
    mode = dict(pipeline_mode=pl.Buffered(1)) if single_buffer else {}
    return pl.BlockSpec((1, n, d), lambda b: (b, 0, 0), **mode)


def _compiler_params(n_grid):
    return pltpu.CompilerParams(dimension_semantics=("arbitrary",) * n_grid,
                                vmem_limit_bytes=VMEM_LIMIT_BYTES)


def _mod_kernel(cond_ref, w_ref, b_ref, out_ref):
    s = _silu(cond_ref[...]).astype(BF16)
    d = cond_ref.shape[1]
    for j in range(3):
        out_ref[0, j] = _mm(s, w_ref[0, :, j * d:(j + 1) * d]) + b_ref[0, j]


def _modulation(cond, ada_w, ada_b):
    n_layers, d, _ = ada_w.shape
    rows = cond.shape[0]
    return pl.pallas_call(
        _mod_kernel,
        grid=(n_layers,),
        in_specs=[
            pl.BlockSpec((rows, d), lambda l: (0, 0)),
            pl.BlockSpec((1, d, 3 * d), lambda l: (l, 0, 0)),
            pl.BlockSpec((1, 3, 1, d), lambda l: (l, 0, 0, 0)),
        ],
        out_specs=pl.BlockSpec((1, 3, rows, d), lambda l: (l, 0, 0, 0)),
        out_shape=jax.ShapeDtypeStruct((n_layers, 3, rows, d), F32),
        compiler_params=_compiler_params(1),
        name="adaln_modulation",
    )(cond, ada_w, ada_b.reshape(n_layers, 3, 1, d))


def _sconv_kernel(x_ref, mod_ref, lng_ref, lnb_ref, w_in_ref, cw_ref, cb_ref, w_out_ref,
                  out_ref, u_scr, *, row_of_batch):
    n, d = x_ref.shape[1], x_ref.shape[2]
    width = w_out_ref.shape[0]
    row = pl.program_id(0) + 1 if row_of_batch else 0
    shift, scale, gate = _mod_rows(mod_ref, row)
    x = x_ref[0]
    h = (x * (1.0 + scale) + shift).astype(BF16)
    t = lax.broadcasted_iota(jnp.int32, (n, 1), 0)
    chunk = width if n <= ROW_TILE else 2 * LANES

    def project(j):
        return tuple(_mm(h, w_in_ref[:, i * width + j * chunk:i * width + (j + 1) * chunk]) for i in range(4))

    nxt = project(0)
    for j in range(width // chunk):
        c0 = j * chunk
        bg, cg, v, g = nxt
        if j + 1 < width // chunk:
            nxt = project(j + 1)
        p = cg * v
        p_prev = jnp.where(t >= 1, pltpu.roll(p, 1, 0), 0.0)
        p_next = jnp.where(t < n - 1, pltpu.roll(p, n - 1, 0), 0.0)
        z = (cw_ref[0:1, c0:c0 + chunk] * p_prev + cw_ref[1:2, c0:c0 + chunk] * p
             + cw_ref[2:3, c0:c0 + chunk] * p_next + cb_ref[:, c0:c0 + chunk])
        u_scr[:, c0:c0 + chunk] = (_silu(g) * bg * z).astype(BF16)
    y = _mm(u_scr[...], w_out_ref[...])
    out_ref[0] = _residual_ln(x, y, gate, lng_ref[0], lnb_ref[0])


def _sconv_layer(x, mod, layer, ln_g, ln_b, w_in, conv_w, conv_b, w_out, *, row_of_batch):
    bsz, n, d = x.shape
    width = w_out.shape[0]
    kernel = functools.partial(_sconv_kernel, row_of_batch=row_of_batch)
    return pl.pallas_call(
        kernel,
        grid=(bsz,),
        in_specs=[
            _seq_spec(n, d),
            pl.BlockSpec((1, 3, SUBLANES, d), lambda b: (layer, 0, 0, 0)),
            pl.BlockSpec((1, 1, d), lambda b: (layer, 0, 0)),
            pl.BlockSpec((1, 1, d), lambda b: (layer, 0, 0)),
            _const_spec(w_in.shape),
            _const_spec(conv_w.shape),
            _const_spec(conv_b.shape),
            _const_spec(w_out.shape),
        ],
        out_specs=_seq_spec(n, d),
        out_shape=jax.ShapeDtypeStruct((bsz, n, d), F32),
        scratch_shapes=[pltpu.VMEM((n, width), BF16)],
        compiler_params=pltpu.CompilerParams(
            dimension_semantics=("arbitrary",), vmem_limit_bytes=VMEM_LIMIT_BYTES,
            allow_input_fusion=[False, False, False, False, True, False, False, True]),
        name="sconv_layer",
    )(x, mod, ln_g, ln_b, w_in, conv_w, conv_b, w_out)


def _chunk_cumsum(x, reverse):
    n = x.shape[0]
    pos = lax.broadcasted_iota(jnp.int32, (n, 1), 0) % HG_CHUNK
    s = 1
    while s < HG_CHUNK:
        if reverse:
            x = x + jnp.where(pos < HG_CHUNK - s, pltpu.roll(x, n - s, 0), 0.0)
        else:
            x = x + jnp.where(pos >= s, pltpu.roll(x, s, 0), 0.0)
        s *= 2
    return x


def _hgrn_kernel(*refs, row_of_batch, layer, has_s0, emit_state, n_groups):
    it = iter(refs)
    x_ref, mod_ref, lng_ref, lnb_ref = (next(it) for _ in range(4))
    wq_ref, wf_fw_ref, wf_bw_ref, wv_ref, wg_ref = (next(it) for _ in range(5))
    lbl_ref, ng_ref, w_out_ref = (next(it) for _ in range(3))
    s0_ref = next(it) if has_s0 else None
    out_ref = next(it)
    s_out_ref = next(it) if emit_state else None
    h_scr, qrel_scr, krel_scr, qp_scr, kp_scr, v_scr, dec_scr, o_scr, g_scr, u_scr = it

    n, d = x_ref.shape[1], x_ref.shape[2]
    hk = w_out_ref.shape[0]
    n_tiles = n // ROW_TILE
    chunks_per_tile = ROW_TILE // HG_CHUNK
    n_chunks = n // HG_CHUNK
    pair = 2 * LANES

    row = pl.program_id(0) + 1 if row_of_batch else 0
    shift, scale, gate = _mod_rows(mod_ref, row)
    x = x_ref[0]
    h_scr[...] = (x * (1.0 + scale) + shift).astype(BF16)

    lbs = []
    for dr in range(2):
        z = lbl_ref[dr]
        e = jnp.exp(z - jnp.max(z, axis=0, keepdims=True))
        lbs.append(jnp.sum(e[:layer + 1], axis=0, keepdims=True) / jnp.sum(e, axis=0, keepdims=True))

    ri = lax.broadcasted_iota(jnp.int32, (ROW_TILE, ROW_TILE), 0)
    ci = lax.broadcasted_iota(jnp.int32, (ROW_TILE, ROW_TILE), 1)
    same_chunk = (ri // HG_CHUNK) == (ci // HG_CHUNK)
    masks = (same_chunk & (ci <= ri), same_chunk & (ci >= ri))

    row_chunk = (lax.broadcasted_iota(jnp.int32, (ROW_TILE, LANES), 0) // HG_CHUNK).astype(BF16)

    def chunk_blocks(a):
        zero = jnp.zeros_like(a)
        return jnp.concatenate([jnp.where(row_chunk == c, a, zero) for c in range(chunks_per_tile)], axis=1)

    def project(hp):
        c0 = hp * pair
        slot = hp % 2
        h = h_scr[...]
        q = _silu(_mm(h, wq_ref[:, c0:c0 + pair])) * (HG_KDIM ** -0.5)
        v_scr[slot] = _mm(h, wv_ref[:, c0:c0 + pair]).astype(BF16)
        q3 = q.reshape(n_chunks, HG_CHUNK, pair)
        for dr in range(2):
            fl = _mm(h, (wf_fw_ref, wf_bw_ref)[dr][:, c0:c0 + pair])
            lb = lbs[dr][:, c0:c0 + pair]
            f = lb + (1.0 - lb) * jax.nn.sigmoid(fl)
            k3 = (1.0 - f).reshape(n_chunks, HG_CHUNK, pair)
            cum = _chunk_cumsum(jnp.log(f), reverse=(dr == 1)).reshape(n_chunks, HG_CHUNK, pair)
            mid = HG_CHUNK // 2 - 1
            if dr == 0:
                ref, tot = cum[:, mid:mid + 1], cum[:, HG_CHUNK - 1:HG_CHUNK]
            else:
                ref, tot = cum[:, HG_CHUNK - 1 - mid:HG_CHUNK - mid], cum[:, 0:1]
            q_rel = q3 * jnp.exp(cum - ref)
            k_rel = k3 * jnp.exp(ref - cum)
            qrel_scr[slot, dr] = q_rel.reshape(n, pair).astype(BF16)
            krel_scr[slot, dr] = k_rel.reshape(n, pair).astype(BF16)
            qp_scr[slot, dr] = (q_rel * jnp.exp(ref)).reshape(n, pair).astype(BF16)
            kp_scr[slot, dr] = (k_rel * jnp.exp(tot - ref)).reshape(n, pair).astype(BF16)
            dec_scr[slot, dr] = jnp.exp(tot)
        g_scr[slot] = _silu(_mm(h, wg_ref[:, c0:c0 + pair]))

    def recur(hp):
        c0 = hp * pair
        slot = hp % 2
        for dr in range(2):
            for h2 in range(2):
                l0 = h2 * LANES
                head = hp * 2 + h2
                if has_s0:
                    s_t0 = s0_ref[0, 0, dr, head].T
                else:
                    s_t0 = jnp.zeros((HG_VDIM, HG_KDIM), F32)

                def tile_body(i, s_t, dr=dr, l0=l0, slot=slot):
                    ti = i if dr == 0 else n_tiles - 1 - i
                    rows = pl.ds(pl.multiple_of(ti * ROW_TILE, ROW_TILE), ROW_TILE)
                    lanes = slice(l0, l0 + LANES)
                    v = v_scr[slot, rows, lanes]
                    att = _mm_nt(qrel_scr[slot, dr, rows, lanes], krel_scr[slot, dr, rows, lanes])
                    o_tile = _mm(jnp.where(masks[dr], att, 0.0), v)
                    upd = _mm_tn(v, chunk_blocks(kp_scr[slot, dr, rows, lanes]))
                    s_prev = [None] * chunks_per_tile
                    order = range(chunks_per_tile) if dr == 0 else range(chunks_per_tile - 1, -1, -1)
                    for c in order:
                        s_prev[c] = s_t.astype(BF16)
                        dec = dec_scr[slot, dr, ti * chunks_per_tile + c][:, lanes]
                        s_t = s_t * dec + upd[:, c * LANES:(c + 1) * LANES]
                    o_tile += _mm_nt(chunk_blocks(qp_scr[slot, dr, rows, lanes]),
                                     jnp.concatenate(s_prev, axis=1))
                    if dr == 0:
                        o_scr[slot, rows, lanes] = o_tile
                    else:
                        o_scr[slot, rows, lanes] += o_tile
                    return s_t

                s_t = s_t0
                for i in range(n_tiles):
                    s_t = tile_body(i, s_t)
                if emit_state:
                    s_out_ref[0, 0, dr, head] = s_t.T

        for h2 in range(2):
            lanes = slice(h2 * LANES, (h2 + 1) * LANES)
            o = o_scr[slot, :, lanes]
            ms = jnp.mean(o * o, axis=-1, keepdims=True)
            on = o * lax.rsqrt(ms + RMS_EPS) * ng_ref[:, c0 + h2 * LANES:c0 + (h2 + 1) * LANES]
            u_scr[:, c0 + h2 * LANES:c0 + (h2 + 1) * LANES] = (on * g_scr[slot, :, lanes]).astype(BF16)

    n_pairs = hk // pair
    project(0)
    for hp in range(n_pairs):
        if hp + 1 < n_pairs:
            project(hp + 1)
        recur(hp)

    y = _mm(u_scr[...], w_out_ref[...])
    if n_groups == 1:
        out_ref[0] = _residual_ln(x, y, gate, lng_ref[0], lnb_ref[0])
    else:
        grp = pl.program_id(1)

        @pl.when(grp == 0)
        def _():
            out_ref[0] = y

        @pl.when((grp > 0) & (grp < n_groups - 1))
        def _():
            out_ref[0] += y

        @pl.when(grp == n_groups - 1)
        def _():
            out_ref[0] = _residual_ln(x, out_ref[0] + y, gate, lng_ref[0], lnb_ref[0])


def _hgrn_layer(x, mod, layer, ln_g, ln_b, w_in, lb_logits, norm_g, w_out, state, *, row_of_batch):
    bsz, n, d = x.shape
    hv = w_out.shape[0]
    has_s0 = state is not None
    emit_state = not has_s0
    pair = 2 * LANES
    n_groups = 2 if n > ROW_TILE else 1
    gcols = hv // n_groups
    heads = HG_HEADS // n_groups
    n_proj = w_in.shape[1] // hv
    state_block = (1, 1, 2, heads, HG_KDIM, HG_VDIM)
    state_spec = pl.BlockSpec(state_block, lambda b, g: (b, 0, 0, g, 0, 0))
    seq_spec = pl.BlockSpec((1, n, d), lambda b, g: (b, 0, 0))
    wmode = dict(pipeline_mode=pl.Buffered(1)) if n_groups == 1 else {}
    in_specs = [
        seq_spec,
        pl.BlockSpec((1, 3, SUBLANES, d), lambda b, g: (layer, 0, 0, 0)),
        pl.BlockSpec((1, 1, d), lambda b, g: (layer, 0, 0)),
        pl.BlockSpec((1, 1, d), lambda b, g: (layer, 0, 0)),
        *[pl.BlockSpec((d, gcols), functools.partial(lambda b, g, i: (0, i * n_groups + g), i=i), **wmode)
          for i in range(n_proj)],
        pl.BlockSpec(lb_logits.shape[:2] + (gcols,), lambda b, g: (0, 0, g), **wmode),
        pl.BlockSpec((1, gcols), lambda b, g: (0, g), **wmode),
        pl.BlockSpec((gcols, d), lambda b, g: (g, 0), **wmode),
    ]
    args = [x, mod, ln_g, ln_b] + [w_in] * n_proj + [lb_logits, norm_g, w_out]
    out_specs = [seq_spec]
    out_shape = [jax.ShapeDtypeStruct((bsz, n, d), F32)]
    if has_s0:
        in_specs.append(state_spec)
        args.append(state)
    if emit_state:
        out_specs.append(state_spec)
        out_shape.append(jax.ShapeDtypeStruct((bsz, 1, 2, HG_HEADS, HG_KDIM, HG_VDIM), F32))
    kernel = functools.partial(_hgrn_kernel, row_of_batch=row_of_batch, layer=layer,
                               has_s0=has_s0, emit_state=emit_state, n_groups=n_groups)
    res = pl.pallas_call(
        kernel,
        grid=(bsz, n_groups),
        in_specs=in_specs,
        out_specs=out_specs,
        out_shape=out_shape,
        scratch_shapes=[
            pltpu.VMEM((n, d), BF16),
            pltpu.VMEM((2, 2, n, pair), BF16),
            pltpu.VMEM((2, 2, n, pair), BF16),
            pltpu.VMEM((2, 2, n, pair), BF16),
            pltpu.VMEM((2, 2, n, pair), BF16),
            pltpu.VMEM((2, n, pair), BF16),
            pltpu.VMEM((2, 2, n // HG_CHUNK, 1, pair), F32),
            pltpu.VMEM((2, n, pair), F32),
            pltpu.VMEM((2, n, pair), F32),
            pltpu.VMEM((n, gcols), BF16),
        ],
        compiler_params=_compiler_params(2),
        name="hgrn2_layer",
    )(*args)
    return res[0] if has_s0 else tuple(res)


def _block_scan(a, x, reverse):
    n, c = a.shape
    nb = n // SUBLANES
    a = a.reshape(nb, SUBLANES, c)
    x = x.reshape(nb, SUBLANES, c)
    pos = lax.broadcasted_iota(jnp.int32, (1, SUBLANES, 1), 1)
    s = 1
    while s < SUBLANES:
        if reverse:
            keep = pos < SUBLANES - s
            shift = SUBLANES - s
        else:
            keep = pos >= s
            shift = s
        a_in = jnp.where(keep, pltpu.roll(a, shift, 1), 1.0)
        x_in = jnp.where(keep, pltpu.roll(x, shift, 1), 0.0)
        x = a * x_in + x
        a = a * a_in
        s *= 2
    return a.reshape(n, c), x.reshape(n, c)


def _rglru_kernel(*refs, row_of_batch, has_s0, emit_state):
    it = iter(refs)
    (x_ref, mod_ref, lng_ref, lnb_ref, w_in_ref, cw_ref, cb_ref, wg_ref, bg_ref, lam_ref,
     w_out_ref) = (next(it) for _ in range(11))
    s0_ref = next(it) if has_s0 else None
    out_ref = next(it)
    s_out_ref = next(it) if emit_state else None
    h_scr, a_scr, b_scr, gs_scr = it

    n, d = x_ref.shape[1], x_ref.shape[2]
    width = w_out_ref.shape[0]
    blk = width // RG_HEADS
    nb = n // SUBLANES

    row = pl.program_id(0) + 1 if row_of_batch else 0
    shift, scale, gate = _mod_rows(mod_ref, row)
    h_scr[...] = (x_ref[0] * (1.0 + scale) + shift).astype(BF16)

    z = -lam_ref[...]
    coef = -RG_C * (jnp.maximum(z, 0.0) + jnp.log1p(jnp.exp(-jnp.abs(z))))

    t = lax.broadcasted_iota(jnp.int32, (n, 1), 0)

    def project(hh):
        h = h_scr[...]
        return (_mm(h, w_in_ref[:, hh * blk:(hh + 1) * blk]),
                _mm(h, w_in_ref[:, width + hh * blk:width + (hh + 1) * blk]))

    def mix(hh, u_pre, g):
        cols = slice(hh * blk, (hh + 1) * blk)
        u = (cw_ref[0:1, cols] * jnp.where(t >= 2, pltpu.roll(u_pre, 2, 0), 0.0)
             + cw_ref[1:2, cols] * jnp.where(t >= 1, pltpu.roll(u_pre, 1, 0), 0.0)
             + cw_ref[2:3, cols] * u_pre
             + cw_ref[3:4, cols] * jnp.where(t < n - 1, pltpu.roll(u_pre, n - 1, 0), 0.0)
             + cb_ref[:, cols])
        gs_scr[:, cols] = _silu(g)
        gates = [jax.nn.sigmoid(_mm(u, wg_ref[dr, hh]) + bg_ref[dr, hh]) for dr in range(2)]
        for dr in range(2):
            a = jnp.exp(coef[dr:dr + 1, cols] * gates[dr][:, :blk])
            xin = jnp.exp(0.5 * jnp.log(1.0 - a * a)) * (gates[dr][:, blk:] * u)
            a_blk, h_blk = _block_scan(a, xin, reverse=(dr == 1))
            a_scr[dr, :, cols] = a_blk
            b_scr[dr, :, cols] = h_blk

    nxt = project(0)
    for hh in range(RG_HEADS):
        cur = nxt
        if hh + 1 < RG_HEADS:
            nxt = project(hh + 1)
        mix(hh, *cur)

    if has_s0:
        carry0 = (s0_ref[0, 0, 0:1, :], s0_ref[0, 0, 1:2, :])
    else:
        carry0 = (jnp.zeros((1, width), F32), jnp.zeros((1, width), F32))

    def carry_body(i, carry):
        cf, cb = carry
        rf = pl.ds(pl.multiple_of(i * SUBLANES, SUBLANES), SUBLANES)
        rb = pl.ds(pl.multiple_of((nb - 1 - i) * SUBLANES, SUBLANES), SUBLANES)
        hf = a_scr[0, rf, :] * cf + b_scr[0, rf, :]
        hb = a_scr[1, rb, :] * cb + b_scr[1, rb, :]
        b_scr[0, rf, :] = hf
        b_scr[1, rb, :] = hb
        return hf[SUBLANES - 1:SUBLANES, :], hb[0:1, :]

    cf, cb = carry0
    for i in range(nb):
        cf, cb = carry_body(i, (cf, cb))
    if emit_state:
        s_out_ref[0, 0, 0:1, :] = cf
        s_out_ref[0, 0, 1:2, :] = cb

    for r0 in range(0, n, ROW_TILE):
        rows = slice(r0, r0 + ROW_TILE)
        mix = ((b_scr[0, rows, :] + b_scr[1, rows, :]) * gs_scr[rows, :]).astype(BF16)
        y = _mm(mix, w_out_ref[...])
        out_ref[0, rows, :] = _residual_ln(x_ref[0, rows, :], y, gate, lng_ref[0], lnb_ref[0])


def _rglru_layer(x, mod, layer, ln_g, ln_b, w_in, conv_w, conv_b, w_gate, b_gate, lam, w_out, state,
                 *, row_of_batch):
    bsz, n, d = x.shape
    width = w_out.shape[0]
    has_s0 = state is not None
    emit_state = not has_s0
    b_gate = b_gate.reshape(2, RG_HEADS, 1, b_gate.shape[-1])
    state_block = (1, 1, 2, width)
    state_spec = pl.BlockSpec(state_block, lambda b: (b, 0, 0, 0))
    in_specs = [
        _seq_spec(n, d),
        pl.BlockSpec((1, 3, SUBLANES, d), lambda b: (layer, 0, 0, 0)),
        pl.BlockSpec((1, 1, d), lambda b: (layer, 0, 0)),
        pl.BlockSpec((1, 1, d), lambda b: (layer, 0, 0)),
        _const_spec(w_in.shape),
        _const_spec(conv_w.shape),
        _const_spec(conv_b.shape),
        _const_spec(w_gate.shape),
        _const_spec(b_gate.shape),
        _const_spec(lam.shape),
        _const_spec(w_out.shape),
    ]
    args = [x, mod, ln_g, ln_b, w_in, conv_w, conv_b, w_gate, b_gate, lam, w_out]
    out_specs = [_seq_spec(n, d)]
    out_shape = [jax.ShapeDtypeStruct((bsz, n, d), F32)]
    if has_s0:
        in_specs.append(state_spec)
        args.append(state)
    if emit_state:
        out_specs.append(state_spec)
        out_shape.append(jax.ShapeDtypeStruct((bsz,) + state_block[1:], F32))
    kernel = functools.partial(_rglru_kernel, row_of_batch=row_of_batch, has_s0=has_s0,
                               emit_state=emit_state)
    res = pl.pallas_call(
        kernel,
        grid=(bsz,),
        in_specs=in_specs,
        out_specs=out_specs,
        out_shape=out_shape,
        scratch_shapes=[
            pltpu.VMEM((n, d), BF16),
            pltpu.VMEM((2, n, width), F32),
            pltpu.VMEM((2, n, width), F32),
            pltpu.VMEM((n, width), F32),
        ],
        compiler_params=_compiler_params(1),
        name="rglru_layer",
    )(*args)
    return res[0] if has_s0 else tuple(res)


MLA_QK = MLA_NOPE + MLA_ROPE


def _mla_weights(w_in, w_qb, w_kvb, w_out):
    w_in, w_qb, w_kvb, w_out = (w.astype(BF16) for w in (w_in, w_qb, w_kvb, w_out))
    d = w_in.shape[0]
    a = MLA_Q_RANK + MLA_KV_RANK
    pad_l, pad_r = MLA_NOPE, MLA_HEAD_PAD - MLA_QK
    w_in_r = jnp.concatenate(
        [w_in[:, :a], jnp.zeros((d, pad_l), w_in.dtype), w_in[:, a:a + MLA_ROPE],
         jnp.zeros((d, pad_r), w_in.dtype), w_in[:, a + MLA_ROPE:]], axis=1)
    wq = w_qb.reshape(MLA_Q_RANK, MLA_HEADS, MLA_QK)
    wq = jnp.pad(wq, ((0, 0), (0, 0), (0, pad_r))).reshape(MLA_Q_RANK, MLA_HEADS * MLA_HEAD_PAD)
    wkv = w_kvb.reshape(MLA_KV_RANK, MLA_HEADS, MLA_NOPE + MLA_VDIM)
    wk = jnp.pad(wkv[:, :, :MLA_NOPE], ((0, 0), (0, 0), (0, MLA_HEAD_PAD - MLA_NOPE)))
    wk = wk.reshape(MLA_KV_RANK, MLA_HEADS * MLA_HEAD_PAD)
    wv_t = wkv[:, :, MLA_NOPE:].reshape(MLA_KV_RANK, MLA_HEADS * MLA_VDIM).T
    return w_in_r, wq, wk, wv_t, w_out


def _rope_tables(n):
    half = MLA_ROPE // 2
    quarter = half // 2
    t = np.arange(n)
    inv_freq = ROPE_BASE ** (-np.arange(0, half, 2, dtype=np.float32) / half)
    cos = np.ones((n, MLA_HEAD_PAD), np.float32)
    sin = np.zeros((n, MLA_HEAD_PAD), np.float32)
    for k, pos in enumerate((t // GRID_W, t % GRID_W)):
        ang = pos[:, None].astype(np.float32) * inv_freq[None].astype(np.float32)
        c, s = np.cos(ang), np.sin(ang)
        base = MLA_NOPE + k * half
        cos[:, base:base + quarter] = c
        cos[:, base + quarter:base + half] = c
        sin[:, base:base + quarter] = -s
        sin[:, base + quarter:base + half] = s
    return jnp.asarray(cos), jnp.asarray(sin)


def _rope(x, cos, sin):
    quarter = MLA_ROPE // 4
    lane = lax.broadcasted_iota(jnp.int32, (1, MLA_HEAD_PAD), 1)
    first = ((lane - MLA_NOPE) % (2 * quarter)) < quarter
    partner = jnp.where(first, pltpu.roll(x, MLA_HEAD_PAD - quarter, 1), pltpu.roll(x, quarter, 1))
    return x * cos + partner * sin


def _rms(x, g):
    return x * lax.rsqrt(jnp.mean(x * x, axis=-1, keepdims=True) + RMS_EPS) * g


def _mla_kernel(*refs, row_of_batch, has_ctx):
    it = iter(refs)
    (x_ref, mod_ref, lng_ref, lnb_ref, w_in_ref, qn_ref, kvn_ref, wq_ref, wk_ref, wv_ref,
     w_out_ref) = (next(it) for _ in range(11))
    if has_ctx:
        ctx_ckv_ref, ctx_kpe_ref, cos_ref, sin_ref = (next(it) for _ in range(4))
    out_ref = next(it)
    if not has_ctx:
        ckv_out_ref, kpe_out_ref = next(it), next(it)
    h_scr, q_scr, k_scr, v_scr = it

    n = x_ref.shape[1]
    n_ctx = k_scr.shape[0] - n
    hp = MLA_HEAD_PAD
    c_ckv = MLA_Q_RANK
    c_kpe = c_ckv + MLA_KV_RANK
    c_g = c_kpe + hp

    row = pl.program_id(0) + 1 if row_of_batch else 0
    shift, scale, gate = _mod_rows(mod_ref, row)
    sm_scale = MLA_QK ** -0.5
    n_tiles = n // ROW_TILE
    group = 2
    while group < MLA_HEADS and 2 * group * ROW_TILE * (n_ctx + n) <= MLA_GROUP_SCORES:
        group *= 2

    def store_keys(ckv, kpe, krows):
        k_all = _mm(ckv, wk_ref[...])
        v_scr[:, krows] = _mm_nt(wv_ref[...], ckv).astype(BF16)
        for hd in range(MLA_HEADS):
            cols = slice(hd * hp, (hd + 1) * hp)
            k_scr[krows, cols] = (k_all[:, cols] + kpe).astype(BF16)

    def proj_body(ti):
        r0 = ti * ROW_TILE
        rows = slice(r0, r0 + ROW_TILE)
        h = (x_ref[0, rows, :] * (1.0 + scale) + shift).astype(BF16)
        h_scr[rows, :] = h
        cq = _rms(_mm(h, w_in_ref[:, :c_ckv]), qn_ref[...])
        ckv = _rms(_mm(h, w_in_ref[:, c_ckv:c_kpe]), kvn_ref[...])
        kpe = _mm(h, w_in_ref[:, c_kpe:c_g])
        if has_ctx:
            cos, sin = cos_ref[rows, :], sin_ref[rows, :]
            kpe = _rope(kpe, cos, sin)
        else:
            ckv_out_ref[0, 0, rows, :] = ckv
            kpe_out_ref[0, 0, rows, :] = kpe[:, MLA_NOPE:MLA_QK]
        q_all = _mm(cq, wq_ref[...])
        for hd in range(MLA_HEADS):
            cols = slice(hd * hp, (hd + 1) * hp)
            qh = q_all[:, cols]
            if has_ctx:
                qh = _rope(qh, cos, sin)
            q_scr[rows, cols] = (qh * sm_scale).astype(BF16)
        store_keys(ckv, kpe, slice(n_ctx + r0, n_ctx + r0 + ROW_TILE))

    def ctx_body(ti):
        rows = slice(ti * ROW_TILE, (ti + 1) * ROW_TILE)
        store_keys(ctx_ckv_ref[0, 0, rows, :], ctx_kpe_ref[0, 0, rows, :], rows)

    for ti in range(n_tiles):
        proj_body(ti)
    for ti in range(n_ctx // ROW_TILE):
        ctx_body(ti)

    def tile_body(ti, carry):
        rows = pl.ds(pl.multiple_of(ti * ROW_TILE, ROW_TILE), ROW_TILE)
        o_t = []
        for g0 in range(0, MLA_HEADS, group):
            heads = range(g0, g0 + group)
            ss = [_mm_nt(k_scr[:, hd * hp:(hd + 1) * hp], q_scr[rows, hd * hp:(hd + 1) * hp]) for hd in heads]
            if g0 == 0:
                g = _silu(_mm(h_scr[rows, :], w_in_ref[:, c_g:]))
            ms = [jnp.max(s, axis=0, keepdims=True) for s in ss]
            es = [jnp.exp(s - m) for s, m in zip(ss, ms)]
            ls = [jnp.sum(e, axis=0, keepdims=True) for e in es]
            o_t += [_mm(v_scr[hd * MLA_VDIM:(hd + 1) * MLA_VDIM, :], e) / l for e, l, hd in zip(es, ls, heads)]
        o = jnp.concatenate(o_t, axis=0).T
        y = _mm(o * g, w_out_ref[...])
        out_ref[0, rows, :] = _residual_ln(x_ref[0, rows, :], y, gate, lng_ref[0], lnb_ref[0])
        return carry

    if n_tiles == 1:
        tile_body(0, 0)
    else:
        lax.fori_loop(0, n_tiles, tile_body, 0)


def _mla_layer(x, mod, layer, ln_g, ln_b, weights, q_norm, kv_norm, ctx_ckv, ctx_kpe, *, row_of_batch):
    bsz, n, d = x.shape
    w_in, wq, wk, wv, w_out = weights
    has_ctx = ctx_ckv is not None
    in_specs = [
        _seq_spec(n, d),
        pl.BlockSpec((1, 3, SUBLANES, d), lambda b: (layer, 0, 0, 0)),
        pl.BlockSpec((1, 1, d), lambda b: (layer, 0, 0)),
        pl.BlockSpec((1, 1, d), lambda b: (layer, 0, 0)),
        _const_spec(w_in.shape),
        _const_spec(q_norm.shape),
        _const_spec(kv_norm.shape),
        _const_spec(wq.shape),
        _const_spec(wk.shape),
        _const_spec(wv.shape),
        _const_spec(w_out.shape),
    ]
    args = [x, mod, ln_g, ln_b, w_in, q_norm, kv_norm, wq, wk, wv, w_out]
    out_specs = [_seq_spec(n, d, single_buffer=has_ctx)]
    out_shape = [jax.ShapeDtypeStruct((bsz, n, d), F32)]
    n_ctx = 0
    if has_ctx:
        n_ctx = ctx_ckv.shape[2]
        ctx_kpe = jnp.pad(ctx_kpe, ((0, 0), (0, 0), (0, 0), (MLA_NOPE, MLA_HEAD_PAD - MLA_QK)))
        cos, sin = _rope_tables(n)
        in_specs += [
            pl.BlockSpec((1, 1, n_ctx, MLA_KV_RANK), lambda b: (b, 0, 0, 0)),
            pl.BlockSpec((1, 1, n_ctx, MLA_HEAD_PAD), lambda b: (b, 0, 0, 0)),
            _const_spec(cos.shape),
            _const_spec(sin.shape),
        ]
        args += [ctx_ckv, ctx_kpe, cos, sin]
    else:
        out_specs += [pl.BlockSpec((1, 1, n, MLA_KV_RANK), lambda b: (b, 0, 0, 0)),
                      pl.BlockSpec((1, 1, n, MLA_ROPE), lambda b: (b, 0, 0, 0))]
        out_shape += [jax.ShapeDtypeStruct((bsz, 1, n, MLA_KV_RANK), F32),
                      jax.ShapeDtypeStruct((bsz, 1, n, MLA_ROPE), F32)]
    kernel = functools.partial(_mla_kernel, row_of_batch=row_of_batch, has_ctx=has_ctx)
    res = pl.pallas_call(
        kernel,
        grid=(bsz,),
        in_specs=in_specs,
        out_specs=out_specs,
        out_shape=out_shape,
        scratch_shapes=[
            pltpu.VMEM((n, d), BF16),
            pltpu.VMEM((n, MLA_HEADS * MLA_HEAD_PAD), BF16),
            pltpu.VMEM((n_ctx + n, MLA_HEADS * MLA_HEAD_PAD), BF16),
            pltpu.VMEM((MLA_HEADS * MLA_VDIM, n_ctx + n), BF16),
        ],
        compiler_params=_compiler_params(1),
        name="mla_layer",
    )(*args)
    return res[0] if has_ctx else tuple(res)


def kernel(x_prompt, x_sample, state_hgrn, state_rglru, cache_mla_ckv, cache_mla_kpe, c, c_ctx, ada_w, ada_b, ln_g, ln_b, hg_w_in, hg_lb_logits, hg_norm_g, hg_w_out, sc_w_in, sc_conv_w, sc_conv_b, sc_w_out, rg_w_in, rg_conv_w, rg_conv_b, rg_w_gate, rg_b_gate, rg_lambda, rg_w_out, mla_w_in, mla_q_norm, mla_kv_norm, mla_w_qb, mla_w_kvb, mla_w_out):
    d = x_prompt.shape[-1]
    n_dec = c.shape[0]
    cond = jnp.concatenate([c_ctx[None], c, jnp.zeros((SUBLANES - 1 - n_dec, d), F32)], axis=0)
    mod = _modulation(cond, ada_w, ada_b)
    lng = ln_g.reshape(DEPTH, 1, d)
    lnb = ln_b.reshape(DEPTH, 1, d)
    hg_wi, hg_wo = hg_w_in[0].astype(BF16), hg_w_out[0].astype(BF16)
    sc_wi, sc_wo = sc_w_in[0].astype(BF16), sc_w_out[0].astype(BF16)
    rg_wi, rg_wg, rg_wo = rg_w_in[0].astype(BF16), rg_w_gate[0].astype(BF16), rg_w_out[0].astype(BF16)
    mla_w = _mla_weights(mla_w_in[0], mla_w_qb[0], mla_w_kvb[0], mla_w_out[0])

    def run(x, rob, st_hg, st_rg, ctx_ckv, ctx_kpe):
        r0 = _hgrn_layer(x, mod, 0, lng, lnb, hg_wi, hg_lb_logits, hg_norm_g, hg_wo, st_hg, row_of_batch=rob)
        x, new_hg = r0 if st_hg is None else (r0, None)
        x = _sconv_layer(x, mod, 1, lng, lnb, sc_wi, sc_conv_w[0], sc_conv_b, sc_wo, row_of_batch=rob)
        r2 = _rglru_layer(x, mod, 2, lng, lnb, rg_wi, rg_conv_w[0], rg_conv_b, rg_wg, rg_b_gate[0],
                          rg_lambda[0], rg_wo, st_rg, row_of_batch=rob)
        x, new_rg = r2 if st_rg is None else (r2, None)
        r3 = _mla_layer(x, mod, 3, lng, lnb, mla_w, mla_q_norm, mla_kv_norm, ctx_ckv, ctx_kpe,
                        row_of_batch=rob)
        if ctx_ckv is None:
            return (r3[0], new_hg, new_rg, r3[1], r3[2])
        return (r3,)

    y_prompt, new_hg, new_rg, new_ckv, new_kpe = run(x_prompt, False, None, None, None, None)
    (y_sample,) = run(x_sample, True, state_hgrn, state_rglru, cache_mla_ckv, cache_mla_kpe)
    return (y_prompt, y_sample, new_hg, new_rg, new_ckv, new_kpe)
```
